```python
import math
import jax, jax.numpy as jnp
from jax import lax
import numpy as np

D_MODEL = 1024
BATCH = 4
SEQ = 4096
DEPTH = 4

N_MIXERS = 3
EPS = 1e-6
NEG_INF = -1e30

NUM_BUCKETS = 32
MAX_DISTANCE = 1024
N_BIAS_HEADS = 16

A_HEADS = 16
A_KV_HEADS = 4
A_GROUP = A_HEADS // A_KV_HEADS
A_HEAD_DIM = D_MODEL // A_HEADS
A_WINDOW = 128
A_QKV_WIDTH = (A_HEADS + 2 * A_KV_HEADS) * A_HEAD_DIM

B_HEADS = 8
B_HEAD_DIM = D_MODEL // (2 * B_HEADS)
B_QBLOCK = 128

C_BRANCHES = ((128, 1), (512, 4), (2048, 16))
C_HEADS_PER_GROUP = 4
C_KV_HEADS = 4
C_HEAD_DIM = 128
C_Q_WIDTH = len(C_BRANCHES) * C_HEADS_PER_GROUP * C_HEAD_DIM
C_QKV_WIDTH = C_Q_WIDTH + 2 * C_KV_HEADS * C_HEAD_DIM

D_FF = 2816
CONV_WIDTH = 3

kernel_name = 'hybrid_interleaved_encoder_block'


def rmsnorm(x, g):
    xf = x.astype(jnp.float32)
    y = xf * lax.rsqrt(jnp.mean(xf * xf, axis=-1, keepdims=True) + EPS)
    return (y * g.astype(jnp.float32)).astype(x.dtype)


def rel_bucket(rel):
    nb = NUM_BUCKETS // 2
    max_exact = nb // 2
    n = jnp.abs(rel)
    nf = jnp.maximum(n, 1).astype(jnp.float32)
    large = max_exact + (jnp.log(nf / max_exact) / math.log(MAX_DISTANCE / max_exact) * (nb - max_exact)).astype(jnp.int32)
    large = jnp.minimum(large, nb - 1)
    return jnp.where(rel > 0, nb, 0) + jnp.where(n < max_exact, n, large)


def banded_attention(q, k, v, kvalid, table_cols, half, dil):
    N, L, Hk, G, dh = q.shape
    blk = half
    nb = L // blk

    def windows(t):
        tp = jnp.pad(t, [(0, 0), (blk, blk)] + [(0, 0)] * (t.ndim - 2))
        tp = tp.reshape((N, nb + 2, blk) + t.shape[2:])
        return jnp.concatenate([tp[:, :-2], tp[:, 1:-1], tp[:, 2:]], axis=2)

    kw, vw, mw = windows(k), windows(v), windows(kvalid)
    qb = q.reshape(N, nb, blk, Hk, G, dh)
    s = jnp.einsum('nbqhgd,nbkhd->nbhgqk', qb, kw, preferred_element_type=jnp.float32)
    offs = jnp.arange(3 * blk)[None, :] - blk - jnp.arange(blk)[:, None]
    bias = table_cols[rel_bucket(offs * dil)].astype(jnp.float32)
    bias = bias.transpose(2, 0, 1).reshape(Hk, G, blk, 3 * blk)
    mask = (jnp.abs(offs) <= half) & mw[:, :, None, None, None, :]
    logits = jnp.where(mask, s * (dh ** -0.5) + bias, NEG_INF)
    lse = jax.nn.logsumexp(logits, axis=-1)
    p = jnp.exp(logits - lse[..., None])
    o = jnp.einsum('nbhgqk,nbkhd->nbqhgd', p.astype(v.dtype), vw)
    return o.reshape(N, L, Hk, G, -1), lse.transpose(0, 1, 4, 2, 3).reshape(N, L, Hk, G)


def windowed_gqa_sink(h, w_qkv, q_gain, k_gain, sink, w_o, rel_table):
    B, S, _ = h.shape
    q, k, v = jnp.split(h @ w_qkv, [A_HEADS * A_HEAD_DIM, (A_HEADS + A_KV_HEADS) * A_HEAD_DIM], axis=-1)
    q = rmsnorm(q.reshape(B, S, A_KV_HEADS, A_GROUP, A_HEAD_DIM), q_gain)
    k = rmsnorm(k.reshape(B, S, A_KV_HEADS, A_HEAD_DIM), k_gain)
    v = v.reshape(B, S, A_KV_HEADS, A_HEAD_DIM)
    kvalid = jnp.ones((B, S), dtype=bool)
    o, lse = banded_attention(q, k, v, kvalid, rel_table, A_WINDOW, 1)
    keep = jax.nn.sigmoid(lse - sink.reshape(A_KV_HEADS, A_GROUP).astype(jnp.float32))
    o = o * keep[..., None].astype(o.dtype)
    return o.reshape(B, S, -1) @ w_o


def diff_attention(h, w_qkv, q_gain, k_gain, lam_q1, lam_k1, lam_q2, lam_k2, sub_gain, w_o, rel_table, lambda_init):
    B, S, _ = h.shape
    H, d = B_HEADS, B_HEAD_DIM
    q, k, v = jnp.split(h @ w_qkv, 3, axis=-1)
    q = rmsnorm(q.reshape(B, S, H, 2, d), q_gain)
    k = rmsnorm(k.reshape(B, S, H, 2, d), k_gain)
    v = v.reshape(B, S, H, 2 * d)
    lam = (jnp.exp(jnp.sum(lam_q1.astype(jnp.float32) * lam_k1.astype(jnp.float32)))
           - jnp.exp(jnp.sum(lam_q2.astype(jnp.float32) * lam_k2.astype(jnp.float32))) + lambda_init)
    nqb = S // B_QBLOCK
    qb = q.reshape(B, nqb, B_QBLOCK, H, 2, d).transpose(1, 0, 2, 3, 4, 5)
    kpos = jnp.arange(S)
    table = rel_table.reshape(NUM_BUCKETS, H, 2)

    def block(args):
        qblk, b = args
        qpos = b * B_QBLOCK + jnp.arange(B_QBLOCK)
        bias = table[rel_bucket(kpos[None, :] - qpos[:, None])].astype(jnp.float32)
        s = jnp.einsum('bqhjd,bkhjd->bhjqk', qblk, k, preferred_element_type=jnp.float32) * (d ** -0.5)
        a = jax.nn.softmax(s + bias.transpose(2, 3, 0, 1), axis=-1)
        a = a[:, :, 0] - lam * a[:, :, 1]
        return jnp.einsum('bhqk,bkhe->bqhe', a.astype(v.dtype), v)

    o = lax.map(block, (qb, jnp.arange(nqb)))
    o = o.transpose(1, 0, 2, 3, 4).reshape(B, S, H, 2 * d)
    o = rmsnorm(o, sub_gain) * (1.0 - lambda_init)
    return o.reshape(B, S, -1) @ w_o


def dilated_branch(q, k, v, dil, half, table_cols):
    B, S = q.shape[:2]
    span = dil * half
    L = -(-S // span) * span

    def pad(t):
        return jnp.pad(t, [(0, 0), (0, L - S)] + [(0, 0)] * (t.ndim - 2))

    def to_sub(t):
        rest = t.shape[2:]
        t = jnp.moveaxis(t.reshape((B, L // dil, dil) + rest), 2, 1)
        return t.reshape((B * dil, L // dil) + rest)

    def from_sub(t):
        rest = t.shape[2:]
        t = jnp.moveaxis(t.reshape((B, dil, L // dil) + rest), 1, 2)
        return t.reshape((B, L) + rest)[:, :S]

    kvalid = jnp.broadcast_to(jnp.arange(L) < S, (B, L))
    o, lse = banded_attention(to_sub(pad(q)), to_sub(pad(k)), to_sub(pad(v)), to_sub(kvalid), table_cols, half, dil)
    return from_sub(o), from_sub(lse)


def dilated_attention(h, w_qkv, q_gain, k_gain, w_o, rel_table):
    B, S, _ = h.shape
    Hg, Hk, d = C_HEADS_PER_GROUP, C_KV_HEADS, C_HEAD_DIM
    q, k, v = jnp.split(h @ w_qkv, [C_Q_WIDTH, C_Q_WIDTH + Hk * d], axis=-1)
    q = rmsnorm(q.reshape(B, S, len(C_BRANCHES), Hk, 1, d), q_gain)
    k = rmsnorm(k.reshape(B, S, Hk, d), k_gain)
    v = v.reshape(B, S, Hk, d)
    outs, lses = [], []
    for g, (window, dil) in enumerate(C_BRANCHES):
        half = window // (2 * dil)
        o, lse = dilated_branch(q[:, :, g], k, v, dil, half, rel_table[:, g * Hg:(g + 1) * Hg])
        outs.append(o)
        lses.append(lse)
    wts = jax.nn.softmax(jnp.stack(lses), axis=0)
    o = jnp.sum(wts[..., None].astype(v.dtype) * jnp.stack(outs), axis=0)
    return o.reshape(B, S, -1) @ w_o


def conv_ffn(h, w_up, conv_w, conv_b, w_down):
    u = h @ w_up
    u = lax.conv_general_dilated(u, conv_w, window_strides=(1,),
                                 padding=((CONV_WIDTH // 2, CONV_WIDTH // 2),),
                                 dimension_numbers=('NWC', 'WIO', 'NWC'),
                                 feature_group_count=u.shape[-1]) + conv_b
    gate, val = jnp.split(u, 2, axis=-1)
    return (jax.nn.silu(gate) * val) @ w_down


def lambda_init_fn(layer):
    return 0.8 - 0.6 * math.exp(-0.3 * layer)


def setup_inputs(seed: int = 0) -> dict:
    key = jax.random.key(seed)
    keys = iter(jax.random.split(key, 128))

    def normal(shape, scale):
        return jax.random.normal(next(keys), shape, jnp.float32) * scale

    def gain(n):
        return 1.0 + normal((n,), 0.02)

    inp = {'x': normal((BATCH, SEQ, D_MODEL), 1.0),
           'rel_bias': normal((NUM_BUCKETS, N_BIAS_HEADS), 0.5)}
    for i in range(DEPTH):
        p = 'l%d_' % i
        kind = i % N_MIXERS
        inp[p + 'attn_norm'] = gain(D_MODEL)
        if kind == 0:
            inp[p + 'w_qkv'] = normal((D_MODEL, A_QKV_WIDTH), D_MODEL ** -0.5)
            inp[p + 'q_gain'] = gain(A_HEAD_DIM)
            inp[p + 'k_gain'] = gain(A_HEAD_DIM)
            inp[p + 'sink'] = normal((A_HEADS,), 0.5)
            inp[p + 'w_o'] = normal((A_HEADS * A_HEAD_DIM, D_MODEL), (A_HEADS * A_HEAD_DIM) ** -0.5)
        elif kind == 1:
            inp[p + 'w_qkv'] = normal((D_MODEL, 3 * D_MODEL), D_MODEL ** -0.5)
            inp[p + 'q_gain'] = gain(B_HEAD_DIM)
            inp[p + 'k_gain'] = gain(B_HEAD_DIM)
            inp[p + 'lambda_q1'] = normal((B_HEAD_DIM,), 0.1)
            inp[p + 'lambda_k1'] = normal((B_HEAD_DIM,), 0.1)
            inp[p + 'lambda_q2'] = normal((B_HEAD_DIM,), 0.1)
            inp[p + 'lambda_k2'] = normal((B_HEAD_DIM,), 0.1)
            inp[p + 'sub_gain'] = gain(2 * B_HEAD_DIM)
            inp[p + 'w_o'] = normal((D_MODEL, D_MODEL), D_MODEL ** -0.5)
        else:
            inp[p + 'w_qkv'] = normal((D_MODEL, C_QKV_WIDTH), D_MODEL ** -0.5)
            inp[p + 'q_gain'] = gain(C_HEAD_DIM)
            inp[p + 'k_gain'] = gain(C_HEAD_DIM)
            inp[p + 'w_o'] = normal((C_KV_HEADS * C_HEAD_DIM, D_MODEL), (C_KV_HEADS * C_HEAD_DIM) ** -0.5)
        inp[p + 'ffn_norm'] = gain(D_MODEL)
        inp[p + 'w_up'] = normal((D_MODEL, 2 * D_FF), D_MODEL ** -0.5)
        inp[p + 'conv_w'] = normal((CONV_WIDTH, 1, 2 * D_FF), CONV_WIDTH ** -0.5)
        inp[p + 'conv_b'] = normal((2 * D_FF,), 0.01)
        inp[p + 'w_down'] = normal((D_FF, D_MODEL), D_FF ** -0.5)
    return inp


def reference(x, rel_bias,
              l0_attn_norm, l0_w_qkv, l0_q_gain, l0_k_gain, l0_sink, l0_w_o,
              l0_ffn_norm, l0_w_up, l0_conv_w, l0_conv_b, l0_w_down,
              l1_attn_norm, l1_w_qkv, l1_q_gain, l1_k_gain, l1_lambda_q1, l1_lambda_k1,
              l1_lambda_q2, l1_lambda_k2, l1_sub_gain, l1_w_o,
              l1_ffn_norm, l1_w_up, l1_conv_w, l1_conv_b, l1_w_down,
              l2_attn_norm, l2_w_qkv, l2_q_gain, l2_k_gain, l2_w_o,
              l2_ffn_norm, l2_w_up, l2_conv_w, l2_conv_b, l2_w_down,
              l3_attn_norm, l3_w_qkv, l3_q_gain, l3_k_gain, l3_sink, l3_w_o,
              l3_ffn_norm, l3_w_up, l3_conv_w, l3_conv_b, l3_w_down):
    attn_norms = [l0_attn_norm, l1_attn_norm, l2_attn_norm, l3_attn_norm]
    mixer_params = [
        (l0_w_qkv, l0_q_gain, l0_k_gain, l0_sink, l0_w_o),
        (l1_w_qkv, l1_q_gain, l1_k_gain, l1_lambda_q1, l1_lambda_k1, l1_lambda_q2, l1_lambda_k2, l1_sub_gain, l1_w_o),
        (l2_w_qkv, l2_q_gain, l2_k_gain, l2_w_o),
        (l3_w_qkv, l3_q_gain, l3_k_gain, l3_sink, l3_w_o),
    ]
    ffn_norms = [l0_ffn_norm, l1_ffn_norm, l2_ffn_norm, l3_ffn_norm]
    ffn_params = [
        (l0_w_up, l0_conv_w, l0_conv_b, l0_w_down),
        (l1_w_up, l1_conv_w, l1_conv_b, l1_w_down),
        (l2_w_up, l2_conv_w, l2_conv_b, l2_w_down),
        (l3_w_up, l3_conv_w, l3_conv_b, l3_w_down),
    ]
    for i in range(DEPTH):
        kind = i % N_MIXERS
        h = rmsnorm(x, attn_norms[i])
        if kind == 0:
            y = windowed_gqa_sink(h, *mixer_params[i], rel_bias)
        elif kind == 1:
            y = diff_attention(h, *mixer_params[i], rel_bias, lambda_init_fn(i))
        else:
            y = dilated_attention(h, *mixer_params[i], rel_bias)
        x = x + y
        x = x + conv_ffn(rmsnorm(x, ffn_norms[i]), *ffn_params[i])
    return x
```

```python
import functools
import math

import jax
import jax.numpy as jnp
from jax import lax
from jax.experimental import pallas as pl
from jax.experimental.pallas import tpu as pltpu

F32 = jnp.float32
BF16 = jnp.bfloat16

D_MODEL = 1024
EPS = 1e-6
NEG_INF = -1e30
LANES = 128
MXU_TILE = 256
VMEM_LIMIT = 56 * 1024 * 1024

NUM_BUCKETS = 32
MAX_DISTANCE = 1024
N_BIAS_HEADS = 16

A_HEADS, A_KV_HEADS, A_HEAD_DIM, A_WINDOW = 16, 4, 64, 128
B_HEADS, B_HEAD_DIM = 8, 64
B_FAR_TILES = 6
C_BRANCHES = ((128, 1), (512, 4), (2048, 16))
C_KV_HEADS, C_HEAD_DIM = 4, 128
C_HALF = 64
D_FF = 2816
FFN_CHUNK = 256
FFN_HALO = 16


def _params(sem, vmem=VMEM_LIMIT):
    return pltpu.CompilerParams(dimension_semantics=sem, vmem_limit_bytes=vmem)


def _rms(x, gain):
    return x * lax.rsqrt(jnp.mean(x * x, axis=-1, keepdims=True) + EPS) * gain


def _bias_tile_kernel(tab_ref, col_ref, dil_ref, off_ref, out_ref, *, half):
    hd = pl.program_id(0)
    t = pl.program_id(1)
    col = col_ref[hd]
    dil = dil_ref[hd]
    off = off_ref[t]
    q = lax.broadcasted_iota(jnp.int32, (LANES, LANES), 0)
    c = lax.broadcasted_iota(jnp.int32, (LANES, LANES), 1)
    x = c - q + off
    rel = x * dil
    nb = NUM_BUCKETS // 2
    max_exact = nb // 2
    n = jnp.abs(rel)
    nf = jnp.maximum(n, 1).astype(F32)
    large = max_exact + (jnp.log(nf * (1.0 / max_exact)) / math.log(MAX_DISTANCE / max_exact)
                         * (nb - max_exact)).astype(jnp.int32)
    large = jnp.minimum(large, nb - 1)
    bucket = jnp.where(rel > 0, nb, 0) + jnp.where(n < max_exact, n, large)
    vals = [tab_ref[b, col] for b in range(NUM_BUCKETS)]
    level = None
    for bit in range(5):
        mask = (bucket & (1 << bit)) != 0
        if level is None:
            level = [jnp.where(mask, vals[2 * k + 1], vals[2 * k]) for k in range(NUM_BUCKETS // 2)]
        else:
            level = [jnp.where(mask, level[2 * k + 1], level[2 * k]) for k in range(len(level) // 2)]
    bias = level[0]
    if half is not None:
        bias = jnp.where(jnp.abs(x) <= half, bias, NEG_INF)
    out_ref[...] = bias


def _bias_tiles(rel_bias, cols, dils, offs, half):
    nh, nt = len(cols), len(offs)
    smem = pl.BlockSpec(memory_space=pltpu.SMEM)
    return pl.pallas_call(
        functools.partial(_bias_tile_kernel, half=half),
        out_shape=jax.ShapeDtypeStruct((nh, nt, LANES, LANES), F32),
        grid=(nh, nt),
        in_specs=[smem, smem, smem, smem],
        out_specs=pl.BlockSpec((None, None, LANES, LANES), lambda h, t: (h, t, 0, 0)),
        compiler_params=_params(("arbitrary", "arbitrary")),
        name="bias_tiles",
    )(rel_bias, jnp.asarray(cols, jnp.int32), jnp.asarray(dils, jnp.int32), jnp.asarray(offs, jnp.int32))


def _proj_kernel(x_ref, g_ref, w_ref, cs_ref, p_ref, *out_refs, widths, n_norm_cols, group):
    h = _rms(x_ref[...], g_ref[...]).astype(BF16)
    col = 0
    for o_ref, width in zip(out_refs, widths):
        for s in range(width // MXU_TILE):
            y = jnp.dot(h, w_ref[:, col:col + MXU_TILE], preferred_element_type=F32)
            if col < n_norm_cols:
                ss = jnp.dot((y * y).astype(BF16), p_ref[...], preferred_element_type=F32)
                y = y * lax.rsqrt(ss * (1.0 / group) + EPS) * cs_ref[:, col:col + MXU_TILE]
            o_ref[:, s * MXU_TILE:(s + 1) * MXU_TILE] = y.astype(BF16)
            col += MXU_TILE


def _project(x2, gain, w, colscale, widths, n_norm_cols, group, tm=512):
    m = x2.shape[0]
    n = w.shape[1]
    assert sum(widths) == n and m % tm == 0 and n_norm_cols % MXU_TILE == 0
    r = jnp.arange(MXU_TILE) // group
    ones_bd = (r[:, None] == r[None, :]).astype(BF16)
    const = lambda i: (0, 0)
    return pl.pallas_call(
        functools.partial(_proj_kernel, widths=tuple(widths), n_norm_cols=n_norm_cols, group=group),
        out_shape=[jax.ShapeDtypeStruct((m, wd), BF16) for wd in widths],
        grid=(m // tm,),
        in_specs=[pl.BlockSpec((tm, D_MODEL), lambda i: (i, 0)),
                  pl.BlockSpec((1, D_MODEL), const),
                  pl.BlockSpec((D_MODEL, n), const),
                  pl.BlockSpec((1, n), const),
                  pl.BlockSpec((MXU_TILE, MXU_TILE), const)],
        out_specs=[pl.BlockSpec((tm, wd), lambda i: (i, 0)) for wd in widths],
        compiler_params=_params(("arbitrary",)),
        name="qkv_proj",
    )(x2, gain.reshape(1, D_MODEL), w, colscale.reshape(1, n), ones_bd)


def _attn_a_kernel(sink_ref, q_ref, k_ref, v_ref, bias_ref, o_ref, *, nq, n_tiles):
    i = pl.program_id(1)
    lane = lax.broadcasted_iota(jnp.int32, (LANES, LANES), 1)
    lo = lane < A_HEAD_DIM
    group = A_HEADS // A_KV_HEADS
    for s in range(nq):
        t = i * nq + s
        base = jnp.clip(t - 1, 0, n_tiles - 3)
        row0 = pl.multiple_of(base * LANES, LANES)
        rows = slice(s * LANES, (s + 1) * LANES)
        for hk in range(A_KV_HEADS):
            kv_cols = slice(hk * LANES, (hk + 1) * LANES)
            kw = k_ref[pl.ds(row0, 3 * LANES), kv_cols]
            vw = v_ref[pl.ds(row0, 3 * LANES), kv_cols]
            parts = []
            for c in (2 * hk, 2 * hk + 1):
                qc = q_ref[rows, c * LANES:(c + 1) * LANES]
                parts += [jnp.where(lo, qc, jnp.zeros_like(qc)), jnp.where(lo, jnp.zeros_like(qc), qc)]
            qs = jnp.concatenate(parts, axis=0)
            sc = lax.dot_general(qs, kw, (((1,), (1,)), ((), ())), preferred_element_type=F32)
            blocks = []
            for g in range(group):
                hq = hk * group + g
                blocks.append(jnp.concatenate(
                    [sc[g * LANES:(g + 1) * LANES, w * LANES:(w + 1) * LANES] + bias_ref[hq, base + w - t + 2]
                     for w in range(3)], axis=1))
            sc = jnp.concatenate(blocks, axis=0)
            mx = jnp.max(sc, axis=-1, keepdims=True)
            p = jnp.exp(sc - mx)
            den = jnp.sum(p, axis=-1, keepdims=True)
            pv = jnp.dot(p.astype(BF16), vw, preferred_element_type=F32)
            outs = []
            for g in range(group):
                hq = hk * group + g
                gr = slice(g * LANES, (g + 1) * LANES)
                outs.append(pv[gr] / (den[gr] + jnp.exp(sink_ref[hq] - mx[gr])))
            o_ref[rows, (2 * hk) * LANES:(2 * hk + 1) * LANES] = jnp.where(lo, outs[0], outs[1]).astype(BF16)
            o_ref[rows, (2 * hk + 1) * LANES:(2 * hk + 2) * LANES] = jnp.where(lo, outs[2], outs[3]).astype(BF16)


def _attention_a(qkv, bias, sink, batch, seq, tq=256):
    nq = tq // LANES
    n_tiles = seq // LANES
    qkv3 = qkv.reshape(batch, seq, 2 * D_MODEL)
    return pl.pallas_call(
        functools.partial(_attn_a_kernel, nq=nq, n_tiles=n_tiles),
        out_shape=jax.ShapeDtypeStruct((batch, seq, D_MODEL), BF16),
        grid=(batch, seq // tq),
        in_specs=[pl.BlockSpec(memory_space=pltpu.SMEM),
                  pl.BlockSpec((None, tq, D_MODEL), lambda b, i: (b, i, 0)),
                  pl.BlockSpec((None, seq, 512), lambda b, i: (b, 0, 2)),
                  pl.BlockSpec((None, seq, 512), lambda b, i: (b, 0, 3)),
                  pl.BlockSpec((A_HEADS, 5, LANES, LANES), lambda b, i: (0, 0, 0, 0))],
        out_specs=pl.BlockSpec((None, tq, D_MODEL), lambda b, i: (b, i, 0)),
        compiler_params=_params(("arbitrary", "arbitrary")),
        name="attn_window",
    )(sink, qkv3, qkv3, qkv3, bias)


def _attn_b_kernel(lam_ref, sg_ref, q_ref, k_ref, v_ref, bias_ref, o_ref, m_ref, l_ref, acc_ref,
                   *, tq, tk, n_kc, lambda_init):
    i = pl.program_id(2)
    nqt = tq // LANES
    nkt = tk // LANES
    lane = lax.broadcasted_iota(jnp.int32, (tq, LANES), 1)
    lo = lane < B_HEAD_DIM
    q = q_ref[...]
    zero = jnp.zeros_like(q)
    qs = jnp.concatenate([jnp.where(lo, q, zero), jnp.where(lo, zero, q)], axis=0)
    m_ref[...] = jnp.full(m_ref.shape, -jnp.inf, F32)
    l_ref[...] = jnp.zeros(l_ref.shape, F32)
    acc_ref[...] = jnp.zeros(acc_ref.shape, F32)

    def chunk(kc, carry):
        k0 = pl.multiple_of(kc * tk, tk)
        kw = k_ref[pl.ds(k0, tk), :]
        vw = v_ref[pl.ds(k0, tk), :]
        sc = lax.dot_general(qs, kw, (((1,), (1,)), ((), ())), preferred_element_type=F32)
        blocks = []
        for mp in range(2):
            for a in range(nqt):
                r0 = mp * tq + a * LANES
                row = []
                for w in range(nkt):
                    d = jnp.clip(kc * nkt + w - (i * nqt + a), -B_FAR_TILES, B_FAR_TILES) + B_FAR_TILES
                    row.append(sc[r0:r0 + LANES, w * LANES:(w + 1) * LANES] + bias_ref[mp, d])
                blocks.append(jnp.concatenate(row, axis=1))
        sc = jnp.concatenate(blocks, axis=0)
        m_old = m_ref[...]
        m_new = jnp.maximum(m_old, jnp.max(sc, axis=-1, keepdims=True))
        alpha = jnp.exp(m_old - m_new)
        p = jnp.exp(sc - m_new)
        l_ref[...] = alpha * l_ref[...] + jnp.sum(p, axis=-1, keepdims=True)
        acc_ref[...] = alpha * acc_ref[...] + jnp.dot(p.astype(BF16), vw, preferred_element_type=F32)
        m_ref[...] = m_new
        return carry

    lax.fori_loop(0, n_kc, chunk, 0)
    lv = lam_ref[...]
    lam = (jnp.exp(jnp.sum(lv[0:1] * lv[1:2], axis=-1, keepdims=True))
           - jnp.exp(jnp.sum(lv[2:3] * lv[3:4], axis=-1, keepdims=True)) + lambda_init)
    o = acc_ref[...] / l_ref[...]
    o = o[:tq] - lam * o[tq:]
    o_ref[...] = (_rms(o, sg_ref[...]) * (1.0 - lambda_init)).astype(BF16)


def _attention_b(qkv, bias, lam_vecs, sub_gain, lambda_init, batch, seq, tq=256, tk=512):
    qkv3 = qkv.reshape(batch, seq, 3 * D_MODEL)
    nb = 2 * B_FAR_TILES + 1
    return pl.pallas_call(
        functools.partial(_attn_b_kernel, tq=tq, tk=tk, n_kc=seq // tk, lambda_init=lambda_init),
        out_shape=jax.ShapeDtypeStruct((batch, seq, D_MODEL), BF16),
        grid=(batch, B_HEADS, seq // tq),
        in_specs=[pl.BlockSpec((4, B_HEAD_DIM), lambda b, h, i: (0, 0)),
                  pl.BlockSpec((1, LANES), lambda b, h, i: (0, 0)),
                  pl.BlockSpec((None, tq, LANES), lambda b, h, i: (b, i, h)),
                  pl.BlockSpec((None, seq, LANES), lambda b, h, i: (b, 0, B_HEADS + h)),
                  pl.BlockSpec((None, seq, LANES), lambda b, h, i: (b, 0, 2 * B_HEADS + h)),
                  pl.BlockSpec((None, 2, nb, LANES, LANES), lambda b, h, i: (h, 0, 0, 0, 0))],
        out_specs=pl.BlockSpec((None, tq, LANES), lambda b, h, i: (b, i, h)),
        scratch_shapes=[pltpu.VMEM((2 * tq, 1), F32), pltpu.VMEM((2 * tq, 1), F32),
                        pltpu.VMEM((2 * tq, LANES), F32)],
        compiler_params=_params(("arbitrary", "arbitrary", "arbitrary")),
        name="attn_diff",
    )(lam_vecs, sub_gain.reshape(1, LANES), qkv3, qkv3, qkv3, bias)


def _attn_c_kernel(q_ref, k_ref, v_ref, bias_ref, o_ref, lse_ref, *, dil, nt, sub_len):
    i = pl.program_id(1)
    width = C_KV_HEADS * C_HEAD_DIM
    for tt in range(nt):
        t = i * nt + tt
        start = pl.multiple_of(jnp.clip(t * LANES - C_HALF, 0, sub_len - 2 * LANES), C_HALF)
        rows = slice(tt * LANES, (tt + 1) * LANES)
        for r in range(dil):
            for j in range(C_KV_HEADS):
                cols = slice(r * width + j * C_HEAD_DIM, r * width + (j + 1) * C_HEAD_DIM)
                kw = k_ref[pl.ds(start, 2 * LANES), cols]
                vw = v_ref[pl.ds(start, 2 * LANES), cols]
                sc = lax.dot_general(q_ref[rows, cols], kw, (((1,), (1,)), ((), ())),
                                     preferred_element_type=F32)
                sc = jnp.concatenate(
                    [sc[:, w * LANES:(w + 1) * LANES]
                     + bias_ref[j, (start + w * LANES - t * LANES) // C_HALF + 2] for w in range(2)], axis=1)
                mx = jnp.max(sc, axis=-1, keepdims=True)
                p = jnp.exp(sc - mx)
                den = jnp.sum(p, axis=-1, keepdims=True)
                o = jnp.dot(p.astype(BF16), vw, preferred_element_type=F32) / den
                o_ref[rows, cols] = o.astype(BF16)
                lse_ref[rows, r * C_KV_HEADS + j:r * C_KV_HEADS + j + 1] = mx + jnp.log(den)


def _attention_c_branch(q, k, v, bias, group, dil, batch, seq):
    width = C_KV_HEADS * C_HEAD_DIM
    sub_len = seq // dil
    nt = max(1, 8 // dil)
    rows = nt * LANES
    view = lambda a: a.reshape(batch, sub_len, dil * width)
    o, lse = pl.pallas_call(
        functools.partial(_attn_c_kernel, dil=dil, nt=nt, sub_len=sub_len),
        out_shape=[jax.ShapeDtypeStruct((batch, sub_len, dil * width), BF16),
                   jax.ShapeDtypeStruct((batch, sub_len, dil * C_KV_HEADS), F32)],
        grid=(batch, sub_len // rows),
        in_specs=[pl.BlockSpec((None, rows, dil * width), lambda b, i: (b, i, 0)),
                  pl.BlockSpec((None, sub_len, dil * width), lambda b, i: (b, 0, 0)),
                  pl.BlockSpec((None, sub_len, dil * width), lambda b, i: (b, 0, 0)),
                  pl.BlockSpec((C_KV_HEADS, 5, LANES, LANES), lambda b, i: (group, 0, 0, 0))],
        out_specs=[pl.BlockSpec((None, rows, dil * width), lambda b, i: (b, i, 0)),
                   pl.BlockSpec((None, rows, dil * C_KV_HEADS), lambda b, i: (b, i, 0))],
        compiler_params=_params(("arbitrary", "arbitrary")),
        name="attn_dilated_%d" % dil,
    )(view(q), view(k), view(v), bias)
    return o.reshape(batch * seq, width), lse.reshape(batch * seq, C_KV_HEADS)


def _out_proj_kernel(x_ref, o_ref, w_ref, y_ref):
    o = o_ref[...]
    for s in range(D_MODEL // MXU_TILE):
        cols = slice(s * MXU_TILE, (s + 1) * MXU_TILE)
        y_ref[:, cols] = x_ref[:, cols] + jnp.dot(o, w_ref[:, cols], preferred_element_type=F32)


def _out_proj(x2, o2, w, tm=512):
    m, kdim = o2.shape
    return pl.pallas_call(
        _out_proj_kernel,
        out_shape=jax.ShapeDtypeStruct((m, D_MODEL), F32),
        grid=(m // tm,),
        in_specs=[pl.BlockSpec((tm, D_MODEL), lambda i: (i, 0)),
                  pl.BlockSpec((tm, kdim), lambda i: (i, 0)),
                  pl.BlockSpec((kdim, D_MODEL), lambda i: (0, 0))],
        out_specs=pl.BlockSpec((tm, D_MODEL), lambda i: (i, 0)),
        compiler_params=_params(("arbitrary",)),
        name="out_proj",
    )(x2, o2, w)


def _out_proj_c_kernel(x_ref, o0_ref, o1_ref, o2_ref, l0_ref, l1_ref, l2_ref, w_ref, y_ref):
    lses = [l0_ref[...], l1_ref[...], l2_ref[...]]
    mx = jnp.maximum(jnp.maximum(lses[0], lses[1]), lses[2])
    es = [jnp.exp(l - mx) for l in lses]
    inv = 1.0 / (es[0] + es[1] + es[2])
    parts = []
    for j in range(C_KV_HEADS):
        cols = slice(j * C_HEAD_DIM, (j + 1) * C_HEAD_DIM)
        acc = None
        for e, o_ref in zip(es, (o0_ref, o1_ref, o2_ref)):
            term = (e[:, j:j + 1] * inv[:, j:j + 1]) * o_ref[:, cols].astype(F32)
            acc = term if acc is None else acc + term
        parts.append(acc.astype(BF16))
    o = jnp.concatenate(parts, axis=1)
    for s in range(D_MODEL // MXU_TILE):
        cols = slice(s * MXU_TILE, (s + 1) * MXU_TILE)
        y_ref[:, cols] = x_ref[:, cols] + jnp.dot(o, w_ref[:, cols], preferred_element_type=F32)


def _out_proj_c(x2, outs, lses, w, tm=512):
    m = x2.shape[0]
    width = C_KV_HEADS * C_HEAD_DIM
    row = lambda i: (i, 0)
    return pl.pallas_call(
        _out_proj_c_kernel,
        out_shape=jax.ShapeDtypeStruct((m, D_MODEL), F32),
        grid=(m // tm,),
        in_specs=[pl.BlockSpec((tm, D_MODEL), row)]
                 + [pl.BlockSpec((tm, width), row)] * 3
                 + [pl.BlockSpec((tm, C_KV_HEADS), row)] * 3
                 + [pl.BlockSpec((width, D_MODEL), lambda i: (0, 0))],
        out_specs=pl.BlockSpec((tm, D_MODEL), row),
        compiler_params=_params(("arbitrary",)),
        name="out_proj_merge",
    )(x2, *outs, *lses, w)


def _ffn_kernel(x_ref, xn_ref, xp_ref, g_ref, wup_ref, cw_ref, cb_ref, wdn_ref, y_ref, h_ref, acc_ref,
                *, tm, tiles_per_seq, n_chunks):
    i = pl.program_id(0)
    pos = i % tiles_per_seq
    g = g_ref[...]
    h_ref[0:tm, :] = _rms(x_ref[...], g).astype(BF16)
    hn = jnp.where(pos == tiles_per_seq - 1, 0.0, _rms(xn_ref[...], g))
    hp = jnp.where(pos == 0, 0.0, _rms(xp_ref[...], g))
    h_ref[tm:tm + FFN_HALO, :] = jnp.concatenate([hn, hp], axis=0).astype(BF16)
    acc_ref[...] = x_ref[...]
    rows = tm + FFN_HALO

    def conv(u, k):
        w = cw_ref[k]
        up = pltpu.roll(u, 1, 0)[:tm]
        dn = pltpu.roll(u, rows - 1, 0)[:tm]
        return w[0:1] * up + w[1:2] * u[:tm] + w[2:3] * dn + cb_ref[k]

    def chunk(j, carry):
        h = h_ref[...]
        gate = conv(jnp.dot(h, wup_ref[j], preferred_element_type=F32), j)
        val = conv(jnp.dot(h, wup_ref[j + n_chunks], preferred_element_type=F32), j + n_chunks)
        act = (gate / (1.0 + jnp.exp(-gate)) * val).astype(BF16)
        acc_ref[...] += jnp.dot(act, wdn_ref[j], preferred_element_type=F32)
        return carry

    lax.fori_loop(0, n_chunks, chunk, 0)
    y_ref[...] = acc_ref[...]


def _conv_ffn(x2, gain, w_up, conv_w, conv_b, w_down, seq, tm=512):
    m = x2.shape[0]
    n_chunks = D_FF // FFN_CHUNK
    tiles_per_seq = seq // tm
    n_tiles = m // tm
    halo_blocks = tm // 8
    wup3 = w_up.astype(BF16).reshape(D_MODEL, 2 * n_chunks, FFN_CHUNK).transpose(1, 0, 2)
    cw3 = conv_w.reshape(3, 2 * n_chunks, FFN_CHUNK).transpose(1, 0, 2)
    cb3 = conv_b.reshape(2 * n_chunks, 1, FFN_CHUNK)
    wdn3 = w_down.astype(BF16).reshape(n_chunks, FFN_CHUNK, D_MODEL)
    last_halo = m // 8 - 1
    const3 = lambda i: (0, 0, 0)
    once = pl.Buffered(1)
    return pl.pallas_call(
        functools.partial(_ffn_kernel, tm=tm, tiles_per_seq=tiles_per_seq, n_chunks=n_chunks),
        out_shape=jax.ShapeDtypeStruct((m, D_MODEL), F32),
        grid=(n_tiles,),
        in_specs=[pl.BlockSpec((tm, D_MODEL), lambda i: (i, 0)),
                  pl.BlockSpec((8, D_MODEL), lambda i: (jnp.minimum((i + 1) * halo_blocks, last_halo), 0)),
                  pl.BlockSpec((8, D_MODEL), lambda i: (jnp.maximum(i * halo_blocks - 1, 0), 0)),
                  pl.BlockSpec((1, D_MODEL), lambda i: (0, 0)),
                  pl.BlockSpec((2 * n_chunks, D_MODEL, FFN_CHUNK), const3, pipeline_mode=once),
                  pl.BlockSpec((2 * n_chunks, 3, FFN_CHUNK), const3),
                  pl.BlockSpec((2 * n_chunks, 1, FFN_CHUNK), const3),
                  pl.BlockSpec((n_chunks, FFN_CHUNK, D_MODEL), const3, pipeline_mode=once)],
        out_specs=pl.BlockSpec((tm, D_MODEL), lambda i: (i, 0)),
        scratch_shapes=[pltpu.VMEM((tm + FFN_HALO, D_MODEL), BF16), pltpu.VMEM((tm, D_MODEL), F32)],
        compiler_params=_params(("arbitrary",)),
        name="conv_ffn",
    )(x2, x2, x2, gain.reshape(1, D_MODEL), wup3, cw3, cb3, wdn3)


def _dup_heads(w, n_heads, dim):
    w = w.reshape(D_MODEL, n_heads, 1, dim)
    return jnp.broadcast_to(w, (D_MODEL, n_heads, 2, dim)).reshape(D_MODEL, 2 * n_heads * dim)


def _mixer_a(x2, norm_gain, w_qkv, q_gain, k_gain, sink, w_o, bias, batch, seq):
    nq = A_HEADS * A_HEAD_DIM
    nk = A_KV_HEADS * A_HEAD_DIM
    w = jnp.concatenate([w_qkv[:, :nq], _dup_heads(w_qkv[:, nq:nq + nk], A_KV_HEADS, A_HEAD_DIM),
                         _dup_heads(w_qkv[:, nq + nk:], A_KV_HEADS, A_HEAD_DIM)], axis=1).astype(BF16)
    colscale = jnp.concatenate([jnp.tile(q_gain, A_HEADS) * A_HEAD_DIM ** -0.5, jnp.tile(k_gain, 2 * A_KV_HEADS),
                                jnp.ones((2 * nk,), F32)])
    (qkv,) = _project(x2, norm_gain, w, colscale, [2 * D_MODEL], nq + 2 * nk, A_HEAD_DIM)
    o = _attention_a(qkv, bias, sink, batch, seq)
    return _out_proj(x2, o.reshape(batch * seq, D_MODEL), w_o.astype(BF16))


def _mixer_b(x2, norm_gain, w_qkv, q_gain, k_gain, lam_q1, lam_k1, lam_q2, lam_k2, sub_gain, w_o, bias,
             lambda_init, batch, seq):
    colscale = jnp.concatenate([jnp.tile(q_gain, 2 * B_HEADS) * B_HEAD_DIM ** -0.5, jnp.tile(k_gain, 2 * B_HEADS),
                                jnp.ones((D_MODEL,), F32)])
    (qkv,) = _project(x2, norm_gain, w_qkv.astype(BF16), colscale, [3 * D_MODEL], 2 * D_MODEL, B_HEAD_DIM)
    lam_vecs = jnp.stack([lam_q1, lam_k1, lam_q2, lam_k2])
    o = _attention_b(qkv, bias, lam_vecs, sub_gain, lambda_init, batch, seq)
    return _out_proj(x2, o.reshape(batch * seq, D_MODEL), w_o.astype(BF16))


def _mixer_c(x2, norm_gain, w_qkv, q_gain, k_gain, w_o, bias, batch, seq):
    width = C_KV_HEADS * C_HEAD_DIM
    n_groups = len(C_BRANCHES)
    colscale = jnp.concatenate([jnp.tile(q_gain, n_groups * C_KV_HEADS) * C_HEAD_DIM ** -0.5,
                                jnp.tile(k_gain, C_KV_HEADS), jnp.ones((width,), F32)])
    *qs, k, v = _project(x2, norm_gain, w_qkv.astype(BF16), colscale, [width] * (n_groups + 2),
                         (n_groups + 1) * width, C_HEAD_DIM)
    outs, lses = [], []
    for g, (window, dil) in enumerate(C_BRANCHES):
        assert window // (2 * dil) == C_HALF
        o, lse = _attention_c_branch(qs[g], k, v, bias, g, dil, batch, seq)
        outs.append(o)
        lses.append(lse)
    return _out_proj_c(x2, outs, lses, w_o.astype(BF16))


def _lambda_init(layer):
    return 0.8 - 0.6 * math.exp(-0.3 * layer)


def kernel(x, rel_bias, l0_attn_norm, l0_w_qkv, l0_q_gain, l0_k_gain, l0_sink, l0_w_o, l0_ffn_norm, l0_w_up, l0_conv_w, l0_conv_b, l0_w_down, l1_attn_norm, l1_w_qkv, l1_q_gain, l1_k_gain, l1_lambda_q1, l1_lambda_k1, l1_lambda_q2, l1_lambda_k2, l1_sub_gain, l1_w_o, l1_ffn_norm, l1_w_up, l1_conv_w, l1_conv_b, l1_w_down, l2_attn_norm, l2_w_qkv, l2_q_gain, l2_k_gain, l2_w_o, l2_ffn_norm, l2_w_up, l2_conv_w, l2_conv_b, l2_w_down, l3_attn_norm, l3_w_qkv, l3_q_gain, l3_k_gain, l3_sink, l3_w_o, l3_ffn_norm, l3_w_up, l3_conv_w, l3_conv_b, l3_w_down):
    batch, seq, _ = x.shape
    assert seq % (2 * LANES * C_BRANCHES[-1][1]) == 0
    heads = list(range(N_BIAS_HEADS))
    bias_a = _bias_tiles(rel_bias, heads, [1] * 16, [LANES * d for d in range(-2, 3)], A_WINDOW)
    bias_b = _bias_tiles(rel_bias, heads, [1] * 16, [LANES * d for d in range(-B_FAR_TILES, B_FAR_TILES + 1)], None)
    c_heads = len(C_BRANCHES) * C_KV_HEADS
    bias_c = _bias_tiles(rel_bias, list(range(c_heads)), [C_BRANCHES[h // C_KV_HEADS][1] for h in range(c_heads)],
                         [C_HALF * d for d in range(-2, 3)], C_HALF)
    bias_b = bias_b.reshape(B_HEADS, 2, 2 * B_FAR_TILES + 1, LANES, LANES)

    x2 = x.reshape(batch * seq, D_MODEL)
    x2 = _mixer_a(x2, l0_attn_norm, l0_w_qkv, l0_q_gain, l0_k_gain, l0_sink, l0_w_o, bias_a, batch, seq)
    x2 = _conv_ffn(x2, l0_ffn_norm, l0_w_up, l0_conv_w, l0_conv_b, l0_w_down, seq)
    x2 = _mixer_b(x2, l1_attn_norm, l1_w_qkv, l1_q_gain, l1_k_gain, l1_lambda_q1, l1_lambda_k1, l1_lambda_q2,
                  l1_lambda_k2, l1_sub_gain, l1_w_o, bias_b, _lambda_init(1), batch, seq)
    x2 = _conv_ffn(x2, l1_ffn_norm, l1_w_up, l1_conv_w, l1_conv_b, l1_w_down, seq)
    x2 = _mixer_c(x2, l2_attn_norm, l2_w_qkv, l2_q_gain, l2_k_gain, l2_w_o, bias_c, batch, seq)
    x2 = _conv_ffn(x2, l2_ffn_norm, l2_w_up, l2_conv_w, l2_conv_b, l2_w_down, seq)
    x2 = _mixer_a(x2, l3_attn_norm, l3_w_qkv, l3_q_gain, l3_k_gain, l3_sink, l3_w_o, bias_a, batch, seq)
    x2 = _conv_ffn(x2, l3_ffn_norm, l3_w_up, l3_conv_w, l3_conv_b, l3_w_down, seq)
    return x2.reshape(batch, seq, D_MODEL)
```

```python
import functools
import math

import jax
import jax.numpy as jnp
from jax import lax
from jax.experimental import pallas as pl
from jax.experimental.pallas import tpu as pltpu

F32 = jnp.float32
BF16 = jnp.bfloat16

D_MODEL = 1024
EPS = 1e-6
NEG_INF = -1e30
LOG2E = math.log2(math.e)
LANES = 128
MXU_TILE = 256
VMEM_LIMIT = 56 * 1024 * 1024

NUM_BUCKETS = 32
MAX_DISTANCE = 1024
N_BIAS_HEADS = 16

A_HEADS, A_KV_HEADS, A_HEAD_DIM, A_WINDOW = 16, 4, 64, 128
B_HEADS, B_HEAD_DIM = 8, 64
B_FAR_TILES = 6
C_BRANCHES = ((128, 1), (512, 4), (2048, 16))
C_KV_HEADS, C_HEAD_DIM = 4, 128
C_HALF = 64
D_FF = 2816
FFN_CHUNK = 256
FFN_HALO = 16


def _params(sem, vmem=VMEM_LIMIT):
    return pltpu.CompilerParams(dimension_semantics=sem, vmem_limit_bytes=vmem)


def _rms(x, gain):
    return x * lax.rsqrt(jnp.mean(x * x, axis=-1, keepdims=True) + EPS) * gain


def _bias_tile_kernel(tab_ref, col_ref, dil_ref, off_ref, out_ref, *, half):
    hd = pl.program_id(0)
    t = pl.program_id(1)
    col = col_ref[hd]
    dil = dil_ref[hd]
    off = off_ref[t]
    q = lax.broadcasted_iota(jnp.int32, (LANES, LANES), 0)
    c = lax.broadcasted_iota(jnp.int32, (LANES, LANES), 1)
    x = c - q + off
    rel = x * dil
    nb = NUM_BUCKETS // 2
    max_exact = nb // 2
    n = jnp.abs(rel)
    nf = jnp.maximum(n, 1).astype(F32)
    large = max_exact + (jnp.log(nf * (1.0 / max_exact)) / math.log(MAX_DISTANCE / max_exact)
                         * (nb - max_exact)).astype(jnp.int32)
    large = jnp.minimum(large, nb - 1)
    bucket = jnp.where(rel > 0, nb, 0) + jnp.where(n < max_exact, n, large)
    vals = [tab_ref[b, col] for b in range(NUM_BUCKETS)]
    level = None
    for bit in range(5):
        mask = (bucket & (1 << bit)) != 0
        if level is None:
            level = [jnp.where(mask, vals[2 * k + 1], vals[2 * k]) for k in range(NUM_BUCKETS // 2)]
        else:
            level = [jnp.where(mask, level[2 * k + 1], level[2 * k]) for k in range(len(level) // 2)]
    bias = level[0] * LOG2E
    if half is not None:
        bias = jnp.where(jnp.abs(x) <= half, bias, NEG_INF)
    out_ref[...] = bias


def _bias_tiles(rel_bias, cols, dils, offs, half):
    nh, nt = len(cols), len(offs)
    smem = pl.BlockSpec(memory_space=pltpu.SMEM)
    return pl.pallas_call(
        functools.partial(_bias_tile_kernel, half=half),
        out_shape=jax.ShapeDtypeStruct((nh, nt, LANES, LANES), F32),
        grid=(nh, nt),
        in_specs=[smem, smem, smem, smem],
        out_specs=pl.BlockSpec((None, None, LANES, LANES), lambda h, t: (h, t, 0, 0)),
        compiler_params=_params(("arbitrary", "arbitrary")),
        name="bias_tiles",
    )(rel_bias, jnp.asarray(cols, jnp.int32), jnp.asarray(dils, jnp.int32), jnp.asarray(offs, jnp.int32))


def _proj_kernel(x_ref, g_ref, w_ref, cs_ref, p_ref, *out_refs, widths, n_norm_cols, group):
    h = _rms(x_ref[...], g_ref[...]).astype(BF16)
    col = 0
    for o_ref, width in zip(out_refs, widths):
        for s in range(width // MXU_TILE):
            y = jnp.dot(h, w_ref[:, col:col + MXU_TILE], preferred_element_type=F32)
            if col < n_norm_cols:
                ss = jnp.dot((y * y).astype(BF16), p_ref[...], preferred_element_type=F32)
                y = y * lax.rsqrt(ss * (1.0 / group) + EPS) * cs_ref[:, col:col + MXU_TILE]
            o_ref[:, s * MXU_TILE:(s + 1) * MXU_TILE] = y.astype(BF16)
            col += MXU_TILE


def _project(x2, gain, w, colscale, widths, n_norm_cols, group, tm=512):
    m = x2.shape[0]
    n = w.shape[1]
    assert sum(widths) == n and m % tm == 0 and n_norm_cols % MXU_TILE == 0
    r = jnp.arange(MXU_TILE) // group
    ones_bd = (r[:, None] == r[None, :]).astype(BF16)
    const = lambda i: (0, 0)
    return pl.pallas_call(
        functools.partial(_proj_kernel, widths=tuple(widths), n_norm_cols=n_norm_cols, group=group),
        out_shape=[jax.ShapeDtypeStruct((m, wd), BF16) for wd in widths],
        grid=(m // tm,),
        in_specs=[pl.BlockSpec((tm, D_MODEL), lambda i: (i, 0)),
                  pl.BlockSpec((1, D_MODEL), const),
                  pl.BlockSpec((D_MODEL, n), const),
                  pl.BlockSpec((1, n), const),
                  pl.BlockSpec((MXU_TILE, MXU_TILE), const)],
        out_specs=[pl.BlockSpec((tm, wd), lambda i: (i, 0)) for wd in widths],
        compiler_params=_params(("arbitrary",)),
        name="qkv_proj",
    )(x2, gain.reshape(1, D_MODEL), w, colscale.reshape(1, n), ones_bd)


def _attn_a_kernel(sink_ref, q_ref, k_ref, v_ref, bias_ref, o_ref, *, nq, n_tiles):
    i = pl.program_id(1)
    lane = lax.broadcasted_iota(jnp.int32, (LANES, LANES), 1)
    lo = lane < A_HEAD_DIM
    group = A_HEADS // A_KV_HEADS
    for s in range(nq):
        t = i * nq + s
        base = jnp.clip(t - 1, 0, n_tiles - 3)
        row0 = pl.multiple_of(base * LANES, LANES)
        rows = slice(s * LANES, (s + 1) * LANES)
        for hk in range(A_KV_HEADS):
            kv_cols = slice(hk * LANES, (hk + 1) * LANES)
            kw = k_ref[pl.ds(row0, 3 * LANES), kv_cols]
            vw = v_ref[pl.ds(row0, 3 * LANES), kv_cols]
            parts = []
            for c in (2 * hk, 2 * hk + 1):
                qc = q_ref[rows, c * LANES:(c + 1) * LANES]
                parts += [jnp.where(lo, qc, jnp.zeros_like(qc)), jnp.where(lo, jnp.zeros_like(qc), qc)]
            qs = jnp.concatenate(parts, axis=0)
            sc = lax.dot_general(qs, kw, (((1,), (1,)), ((), ())), preferred_element_type=F32)
            blocks = []
            for g in range(group):
                hq = hk * group + g
                blocks.append(jnp.concatenate(
                    [sc[g * LANES:(g + 1) * LANES, w * LANES:(w + 1) * LANES] + bias_ref[hq, base + w - t + 2]
                     for w in range(3)], axis=1))
            sc = jnp.concatenate(blocks, axis=0)
            mx = jnp.max(sc, axis=-1, keepdims=True)
            p = jnp.exp2(sc - mx)
            den = jnp.sum(p, axis=-1, keepdims=True)
            pv = jnp.dot(p.astype(BF16), vw, preferred_element_type=F32)
            outs = []
            for g in range(group):
                hq = hk * group + g
                gr = slice(g * LANES, (g + 1) * LANES)
                outs.append(pv[gr] / (den[gr] + jnp.exp2(sink_ref[hq] * LOG2E - mx[gr])))
            o_ref[rows, (2 * hk) * LANES:(2 * hk + 1) * LANES] = jnp.where(lo, outs[0], outs[1]).astype(BF16)
            o_ref[rows, (2 * hk + 1) * LANES:(2 * hk + 2) * LANES] = jnp.where(lo, outs[2], outs[3]).astype(BF16)


def _attention_a(qkv, bias, sink, batch, seq, tq=256):
    nq = tq // LANES
    n_tiles = seq // LANES
    qkv3 = qkv.reshape(batch, seq, 2 * D_MODEL)
    return pl.pallas_call(
        functools.partial(_attn_a_kernel, nq=nq, n_tiles=n_tiles),
        out_shape=jax.ShapeDtypeStruct((batch, seq, D_MODEL), BF16),
        grid=(batch, seq // tq),
        in_specs=[pl.BlockSpec(memory_space=pltpu.SMEM),
                  pl.BlockSpec((None, tq, D_MODEL), lambda b, i: (b, i, 0)),
                  pl.BlockSpec((None, seq, 512), lambda b, i: (b, 0, 2)),
                  pl.BlockSpec((None, seq, 512), lambda b, i: (b, 0, 3)),
                  pl.BlockSpec((A_HEADS, 5, LANES, LANES), lambda b, i: (0, 0, 0, 0))],
        out_specs=pl.BlockSpec((None, tq, D_MODEL), lambda b, i: (b, i, 0)),
        compiler_params=_params(("arbitrary", "arbitrary")),
        name="attn_window",
    )(sink, qkv3, qkv3, qkv3, bias)


def _attn_b_kernel(lam_ref, sg_ref, q_ref, k_ref, v_ref, bias_ref, o_ref, qs_ref, m_ref, l_ref, acc_ref,
                   *, tq, tk, n_kc, unroll, lambda_init):
    i = pl.program_id(2)
    nqt = tq // LANES
    nkt = tk // LANES
    lo = lax.broadcasted_iota(jnp.int32, (tq, LANES), 1) < B_HEAD_DIM
    q = q_ref[...]
    zero = jnp.zeros_like(q)
    qs_ref[0:tq, :] = jnp.where(lo, q, zero)
    qs_ref[tq:2 * tq, :] = jnp.where(lo, zero, q)
    m_ref[...] = jnp.full(m_ref.shape, -jnp.inf, F32)
    l_ref[...] = jnp.zeros(l_ref.shape, F32)
    acc_ref[...] = jnp.zeros(acc_ref.shape, F32)

    def chunk(kc, carry):
        k0 = pl.multiple_of(kc * tk, tk)
        kw = k_ref[pl.ds(k0, tk), :]
        vw = v_ref[pl.ds(k0, tk), :]
        sc = lax.dot_general(qs_ref[...], kw, (((1,), (1,)), ((), ())), preferred_element_type=F32)
        probs = []
        for rb in range(2 * nqt):
            mp, a = divmod(rb, nqt)
            rows = slice(rb * LANES, (rb + 1) * LANES)
            tiles = []
            for w in range(nkt):
                d = jnp.clip(kc * nkt + w - (i * nqt + a), -B_FAR_TILES, B_FAR_TILES) + B_FAR_TILES
                tiles.append(sc[rows, w * LANES:(w + 1) * LANES] + bias_ref[mp, d])
            m_old = m_ref[rows, :]
            m_new = jnp.maximum(m_old, jnp.max(functools.reduce(jnp.maximum, tiles), axis=-1, keepdims=True))
            alpha = jnp.exp2(m_old - m_new)
            p = [jnp.exp2(t - m_new) for t in tiles]
            l_ref[rows, :] = alpha * l_ref[rows, :] + functools.reduce(jnp.add, p)
            acc_ref[rows, :] = alpha * acc_ref[rows, :]
            m_ref[rows, :] = m_new
            probs.append(jnp.concatenate(p, axis=1).astype(BF16))
        acc_ref[...] += jnp.dot(jnp.concatenate(probs, axis=0), vw, preferred_element_type=F32)
        return carry

    lax.fori_loop(0, n_kc, chunk, 0, unroll=unroll)
    lv = lam_ref[...]
    lam = (jnp.exp(jnp.sum(lv[0:1] * lv[1:2], axis=-1, keepdims=True))
           - jnp.exp(jnp.sum(lv[2:3] * lv[3:4], axis=-1, keepdims=True)) + lambda_init)
    o = acc_ref[...] / jnp.sum(l_ref[...], axis=-1, keepdims=True)
    o = o[:tq] - lam * o[tq:]
    o_ref[...] = (_rms(o, sg_ref[...]) * (1.0 - lambda_init)).astype(BF16)


def _attention_b(qkv, bias, lam_vecs, sub_gain, lambda_init, batch, seq, tq=512, tk=512, unroll=4):
    qkv3 = qkv.reshape(batch, seq, 3 * D_MODEL)
    nb = 2 * B_FAR_TILES + 1
    return pl.pallas_call(
        functools.partial(_attn_b_kernel, tq=tq, tk=tk, n_kc=seq // tk, unroll=unroll, lambda_init=lambda_init),
        out_shape=jax.ShapeDtypeStruct((batch, seq, D_MODEL), BF16),
        grid=(batch, B_HEADS, seq // tq),
        in_specs=[pl.BlockSpec((4, B_HEAD_DIM), lambda b, h, i: (0, 0)),
                  pl.BlockSpec((1, LANES), lambda b, h, i: (0, 0)),
                  pl.BlockSpec((None, tq, LANES), lambda b, h, i: (b, i, h)),
                  pl.BlockSpec((None, seq, LANES), lambda b, h, i: (b, 0, B_HEADS + h)),
                  pl.BlockSpec((None, seq, LANES), lambda b, h, i: (b, 0, 2 * B_HEADS + h)),
                  pl.BlockSpec((None, 2, nb, LANES, LANES), lambda b, h, i: (h, 0, 0, 0, 0))],
        out_specs=pl.BlockSpec((None, tq, LANES), lambda b, h, i: (b, i, h)),
        scratch_shapes=[pltpu.VMEM((2 * tq, LANES), BF16), pltpu.VMEM((2 * tq, LANES), F32),
                        pltpu.VMEM((2 * tq, LANES), F32), pltpu.VMEM((2 * tq, LANES), F32)],
        compiler_params=_params(("arbitrary", "arbitrary", "arbitrary")),
        name="attn_diff",
    )(lam_vecs, sub_gain.reshape(1, LANES), qkv3, qkv3, qkv3, bias)


def _attn_c_kernel(q_ref, k_ref, v_ref, bias_ref, o_ref, lse_ref, *, dil, nt, sub_len):
    i = pl.program_id(1)
    width = C_KV_HEADS * C_HEAD_DIM
    for tt in range(nt):
        t = i * nt + tt
        start = pl.multiple_of(jnp.clip(t * LANES - C_HALF, 0, sub_len - 2 * LANES), C_HALF)
        rows = slice(tt * LANES, (tt + 1) * LANES)
        for r in range(dil):
            for j in range(C_KV_HEADS):
                cols = slice(r * width + j * C_HEAD_DIM, r * width + (j + 1) * C_HEAD_DIM)
                kw = k_ref[pl.ds(start, 2 * LANES), cols]
                vw = v_ref[pl.ds(start, 2 * LANES), cols]
                sc = lax.dot_general(q_ref[rows, cols], kw, (((1,), (1,)), ((), ())),
                                     preferred_element_type=F32)
                sc = jnp.concatenate(
                    [sc[:, w * LANES:(w + 1) * LANES]
                     + bias_ref[j, (start + w * LANES - t * LANES) // C_HALF + 2] for w in range(2)], axis=1)
                mx = jnp.max(sc, axis=-1, keepdims=True)
                p = jnp.exp2(sc - mx)
                den = jnp.sum(p, axis=-1, keepdims=True)
                o = jnp.dot(p.astype(BF16), vw, preferred_element_type=F32) / den
                o_ref[rows, cols] = o.astype(BF16)
                lse_ref[rows, r * C_KV_HEADS + j:r * C_KV_HEADS + j + 1] = mx + jnp.log2(den)


def _attention_c_branch(q, k, v, bias, group, dil, batch, seq):
    width = C_KV_HEADS * C_HEAD_DIM
    sub_len = seq // dil
    nt = max(1, 8 // dil)
    rows = nt * LANES
    view = lambda a: a.reshape(batch, sub_len, dil * width)
    o, lse = pl.pallas_call(
        functools.partial(_attn_c_kernel, dil=dil, nt=nt, sub_len=sub_len),
        out_shape=[jax.ShapeDtypeStruct((batch, sub_len, dil * width), BF16),
                   jax.ShapeDtypeStruct((batch, sub_len, dil * C_KV_HEADS), F32)],
        grid=(batch, sub_len // rows),
        in_specs=[pl.BlockSpec((None, rows, dil * width), lambda b, i: (b, i, 0)),
                  pl.BlockSpec((None, sub_len, dil * width), lambda b, i: (b, 0, 0)),
                  pl.BlockSpec((None, sub_len, dil * width), lambda b, i: (b, 0, 0)),
                  pl.BlockSpec((C_KV_HEADS, 5, LANES, LANES), lambda b, i: (group, 0, 0, 0))],
        out_specs=[pl.BlockSpec((None, rows, dil * width), lambda b, i: (b, i, 0)),
                   pl.BlockSpec((None, rows, dil * C_KV_HEADS), lambda b, i: (b, i, 0))],
        compiler_params=_params(("arbitrary", "arbitrary")),
        name="attn_dilated_%d" % dil,
    )(view(q), view(k), view(v), bias)
    return o.reshape(batch * seq, width), lse.reshape(batch * seq, C_KV_HEADS)


def _out_proj_kernel(x_ref, o_ref, w_ref, y_ref):
    o = o_ref[...]
    for s in range(D_MODEL // MXU_TILE):
        cols = slice(s * MXU_TILE, (s + 1) * MXU_TILE)
        y_ref[:, cols] = x_ref[:, cols] + jnp.dot(o, w_ref[:, cols], preferred_element_type=F32)


def _out_proj(x2, o2, w, tm=512):
    m, kdim = o2.shape
    return pl.pallas_call(
        _out_proj_kernel,
        out_shape=jax.ShapeDtypeStruct((m, D_MODEL), F32),
        grid=(m // tm,),
        in_specs=[pl.BlockSpec((tm, D_MODEL), lambda i: (i, 0)),
                  pl.BlockSpec((tm, kdim), lambda i: (i, 0)),
                  pl.BlockSpec((kdim, D_MODEL), lambda i: (0, 0))],
        out_specs=pl.BlockSpec((tm, D_MODEL), lambda i: (i, 0)),
        compiler_params=_params(("arbitrary",)),
        name="out_proj",
    )(x2, o2, w)


def _out_proj_c_kernel(x_ref, o0_ref, o1_ref, o2_ref, l0_ref, l1_ref, l2_ref, w_ref, y_ref):
    lses = [l0_ref[...], l1_ref[...], l2_ref[...]]
    mx = jnp.maximum(jnp.maximum(lses[0], lses[1]), lses[2])
    es = [jnp.exp2(l - mx) for l in lses]
    inv = 1.0 / (es[0] + es[1] + es[2])
    parts = []
    for j in range(C_KV_HEADS):
        cols = slice(j * C_HEAD_DIM, (j + 1) * C_HEAD_DIM)
        acc = None
        for e, o_ref in zip(es, (o0_ref, o1_ref, o2_ref)):
            term = (e[:, j:j + 1] * inv[:, j:j + 1]) * o_ref[:, cols].astype(F32)
            acc = term if acc is None else acc + term
        parts.append(acc.astype(BF16))
    o = jnp.concatenate(parts, axis=1)
    for s in range(D_MODEL // MXU_TILE):
        cols = slice(s * MXU_TILE, (s + 1) * MXU_TILE)
        y_ref[:, cols] = x_ref[:, cols] + jnp.dot(o, w_ref[:, cols], preferred_element_type=F32)


def _out_proj_c(x2, outs, lses, w, tm=512):
    m = x2.shape[0]
    width = C_KV_HEADS * C_HEAD_DIM
    row = lambda i: (i, 0)
    return pl.pallas_call(
        _out_proj_c_kernel,
        out_shape=jax.ShapeDtypeStruct((m, D_MODEL), F32),
        grid=(m // tm,),
        in_specs=[pl.BlockSpec((tm, D_MODEL), row)]
                 + [pl.BlockSpec((tm, width), row)] * 3
                 + [pl.BlockSpec((tm, C_KV_HEADS), row)] * 3
                 + [pl.BlockSpec((width, D_MODEL), lambda i: (0, 0))],
        out_specs=pl.BlockSpec((tm, D_MODEL), row),
        compiler_params=_params(("arbitrary",)),
        name="out_proj_merge",
    )(x2, *outs, *lses, w)


def _ffn_kernel(x_ref, xn_ref, xp_ref, g_ref, wup_ref, cw_ref, cb_ref, wdn_ref, y_ref, h_ref, acc_ref,
                ua_ref, ub_ref, *, tm, tiles_per_seq, n_chunks):
    i = pl.program_id(0)
    pos = i % tiles_per_seq
    g = g_ref[...]
    h_ref[0:tm, :] = _rms(x_ref[...], g).astype(BF16)
    hn = jnp.where(pos == tiles_per_seq - 1, 0.0, _rms(xn_ref[...], g))
    hp = jnp.where(pos == 0, 0.0, _rms(xp_ref[...], g))
    h_ref[tm:tm + FFN_HALO, :] = jnp.concatenate([hn, hp], axis=0).astype(BF16)
    acc_ref[...] = x_ref[...]
    rows = tm + FFN_HALO

    slabs = FFN_CHUNK // LANES

    def up_proj(j, u_ref):
        h = h_ref[...]
        for half, k in enumerate((j, j + n_chunks)):
            u = jnp.dot(h, wup_ref[k], preferred_element_type=F32)
            for s in range(slabs):
                cols = slice(s * LANES, (s + 1) * LANES)
                slab = half * slabs + s
                u_ref[slab, 8:8 + tm, :] = u[0:tm, cols]
                u_ref[slab, 0:8, :] = u[tm + 8:tm + 16, cols]
                u_ref[slab, tm + 8:tm + 16, :] = u[tm:tm + 8, cols]

    def conv(u_ref, slab, k, cols):
        w = cw_ref[k]
        return (w[0:1, cols] * u_ref[slab, pl.ds(7, tm), :] + w[1:2, cols] * u_ref[slab, pl.ds(8, tm), :]
                + w[2:3, cols] * u_ref[slab, pl.ds(9, tm), :] + cb_ref[k][:, cols])

    def gate_down(j, u_ref):
        acts = []
        for s in range(slabs):
            cols = slice(s * LANES, (s + 1) * LANES)
            gate = conv(u_ref, s, j, cols)
            val = conv(u_ref, slabs + s, j + n_chunks, cols)
            acts.append((gate / (1.0 + jnp.exp(-gate)) * val).astype(BF16))
        acc_ref[...] += jnp.dot(jnp.concatenate(acts, axis=1), wdn_ref[j], preferred_element_type=F32)

    up_proj(0, ua_ref)

    def pair(c, carry):
        j = 2 * c
        up_proj(j + 1, ub_ref)
        gate_down(j, ua_ref)
        up_proj(j + 2, ua_ref)
        gate_down(j + 1, ub_ref)
        return carry

    assert n_chunks % 2 == 1
    lax.fori_loop(0, n_chunks // 2, pair, 0)
    gate_down(n_chunks - 1, ua_ref)
    y_ref[...] = acc_ref[...]


def _conv_ffn(x2, gain, w_up, conv_w, conv_b, w_down, seq, tm=512):
    m = x2.shape[0]
    n_chunks = D_FF // FFN_CHUNK
    tiles_per_seq = seq // tm
    n_tiles = m // tm
    halo_blocks = tm // 8
    wup3 = w_up.astype(BF16).reshape(D_MODEL, 2 * n_chunks, FFN_CHUNK).transpose(1, 0, 2)
    cw3 = conv_w.reshape(3, 2 * n_chunks, FFN_CHUNK).transpose(1, 0, 2)
    cb3 = conv_b.reshape(2 * n_chunks, 1, FFN_CHUNK)
    wdn3 = w_down.astype(BF16).reshape(n_chunks, FFN_CHUNK, D_MODEL)
    last_halo = m // 8 - 1
    const3 = lambda i: (0, 0, 0)
    once = pl.Buffered(1)
    return pl.pallas_call(
        functools.partial(_ffn_kernel, tm=tm, tiles_per_seq=tiles_per_seq, n_chunks=n_chunks),
        out_shape=jax.ShapeDtypeStruct((m, D_MODEL), F32),
        grid=(n_tiles,),
        in_specs=[pl.BlockSpec((tm, D_MODEL), lambda i: (i, 0)),
                  pl.BlockSpec((8, D_MODEL), lambda i: (jnp.minimum((i + 1) * halo_blocks, last_halo), 0)),
                  pl.BlockSpec((8, D_MODEL), lambda i: (jnp.maximum(i * halo_blocks - 1, 0), 0)),
                  pl.BlockSpec((1, D_MODEL), lambda i: (0, 0)),
                  pl.BlockSpec((2 * n_chunks, D_MODEL, FFN_CHUNK), const3, pipeline_mode=once),
                  pl.BlockSpec((2 * n_chunks, 3, FFN_CHUNK), const3),
                  pl.BlockSpec((2 * n_chunks, 1, FFN_CHUNK), const3),
                  pl.BlockSpec((n_chunks, FFN_CHUNK, D_MODEL), const3, pipeline_mode=once)],
        out_specs=pl.BlockSpec((tm, D_MODEL), lambda i: (i, 0)),
        scratch_shapes=[pltpu.VMEM((tm + FFN_HALO, D_MODEL), BF16), pltpu.VMEM((tm, D_MODEL), F32),
                        pltpu.VMEM((2 * FFN_CHUNK // LANES, tm + FFN_HALO, LANES), F32),
                        pltpu.VMEM((2 * FFN_CHUNK // LANES, tm + FFN_HALO, LANES), F32)],
        compiler_params=_params(("arbitrary",)),
        name="conv_ffn",
    )(x2, x2, x2, gain.reshape(1, D_MODEL), wup3, cw3, cb3, wdn3)


def _dup_heads(w, n_heads, dim):
    w = w.reshape(D_MODEL, n_heads, 1, dim)
    return jnp.broadcast_to(w, (D_MODEL, n_heads, 2, dim)).reshape(D_MODEL, 2 * n_heads * dim)


def _mixer_a(x2, norm_gain, w_qkv, q_gain, k_gain, sink, w_o, bias, batch, seq):
    nq = A_HEADS * A_HEAD_DIM
    nk = A_KV_HEADS * A_HEAD_DIM
    w = jnp.concatenate([w_qkv[:, :nq], _dup_heads(w_qkv[:, nq:nq + nk], A_KV_HEADS, A_HEAD_DIM),
                         _dup_heads(w_qkv[:, nq + nk:], A_KV_HEADS, A_HEAD_DIM)], axis=1).astype(BF16)
    colscale = jnp.concatenate([jnp.tile(q_gain, A_HEADS) * (LOG2E * A_HEAD_DIM ** -0.5), jnp.tile(k_gain, 2 * A_KV_HEADS),
                                jnp.ones((2 * nk,), F32)])
    (qkv,) = _project(x2, norm_gain, w, colscale, [2 * D_MODEL], nq + 2 * nk, A_HEAD_DIM)
    o = _attention_a(qkv, bias, sink, batch, seq)
    return _out_proj(x2, o.reshape(batch * seq, D_MODEL), w_o.astype(BF16))


def _mixer_b(x2, norm_gain, w_qkv, q_gain, k_gain, lam_q1, lam_k1, lam_q2, lam_k2, sub_gain, w_o, bias,
             lambda_init, batch, seq):
    colscale = jnp.concatenate([jnp.tile(q_gain, 2 * B_HEADS) * (LOG2E * B_HEAD_DIM ** -0.5), jnp.tile(k_gain, 2 * B_HEADS),
                                jnp.ones((D_MODEL,), F32)])
    (qkv,) = _project(x2, norm_gain, w_qkv.astype(BF16), colscale, [3 * D_MODEL], 2 * D_MODEL, B_HEAD_DIM)
    lam_vecs = jnp.stack([lam_q1, lam_k1, lam_q2, lam_k2])
    o = _attention_b(qkv, bias, lam_vecs, sub_gain, lambda_init, batch, seq)
    return _out_proj(x2, o.reshape(batch * seq, D_MODEL), w_o.astype(BF16))


def _mixer_c(x2, norm_gain, w_qkv, q_gain, k_gain, w_o, bias, batch, seq):
    width = C_KV_HEADS * C_HEAD_DIM
    n_groups = len(C_BRANCHES)
    colscale = jnp.concatenate([jnp.tile(q_gain, n_groups * C_KV_HEADS) * (LOG2E * C_HEAD_DIM ** -0.5),
                                jnp.tile(k_gain, C_KV_HEADS), jnp.ones((width,), F32)])
    *qs, k, v = _project(x2, norm_gain, w_qkv.astype(BF16), colscale, [width] * (n_groups + 2),
                         (n_groups + 1) * width, C_HEAD_DIM)
    outs, lses = [], []
    for g, (window, dil) in enumerate(C_BRANCHES):
        assert window // (2 * dil) == C_HALF
        o, lse = _attention_c_branch(qs[g], k, v, bias, g, dil, batch, seq)
        outs.append(o)
        lses.append(lse)
    return _out_proj_c(x2, outs, lses, w_o.astype(BF16))


def _lambda_init(layer):
    return 0.8 - 0.6 * math.exp(-0.3 * layer)


def kernel(x, rel_bias, l0_attn_norm, l0_w_qkv, l0_q_gain, l0_k_gain, l0_sink, l0_w_o, l0_ffn_norm, l0_w_up, l0_conv_w, l0_conv_b, l0_w_down, l1_attn_norm, l1_w_qkv, l1_q_gain, l1_k_gain, l1_lambda_q1, l1_lambda_k1, l1_lambda_q2, l1_lambda_k2, l1_sub_gain, l1_w_o, l1_ffn_norm, l1_w_up, l1_conv_w, l1_conv_b, l1_w_down, l2_attn_norm, l2_w_qkv, l2_q_gain, l2_k_gain, l2_w_o, l2_ffn_norm, l2_w_up, l2_conv_w, l2_conv_b, l2_w_down, l3_attn_norm, l3_w_qkv, l3_q_gain, l3_k_gain, l3_sink, l3_w_o, l3_ffn_norm, l3_w_up, l3_conv_w, l3_conv_b, l3_w_down):
    batch, seq, _ = x.shape
    assert seq % (2 * LANES * C_BRANCHES[-1][1]) == 0
    heads = list(range(N_BIAS_HEADS))
    bias_a = _bias_tiles(rel_bias, heads, [1] * 16, [LANES * d for d in range(-2, 3)], A_WINDOW)
    bias_b = _bias_tiles(rel_bias, heads, [1] * 16, [LANES * d for d in range(-B_FAR_TILES, B_FAR_TILES + 1)], None)
    c_heads = len(C_BRANCHES) * C_KV_HEADS
    bias_c = _bias_tiles(rel_bias, list(range(c_heads)), [C_BRANCHES[h // C_KV_HEADS][1] for h in range(c_heads)],
                         [C_HALF * d for d in range(-2, 3)], C_HALF)
    bias_b = bias_b.reshape(B_HEADS, 2, 2 * B_FAR_TILES + 1, LANES, LANES)

    x2 = x.reshape(batch * seq, D_MODEL)
    x2 = _mixer_a(x2, l0_attn_norm, l0_w_qkv, l0_q_gain, l0_k_gain, l0_sink, l0_w_o, bias_a, batch, seq)
    x2 = _conv_ffn(x2, l0_ffn_norm, l0_w_up, l0_conv_w, l0_conv_b, l0_w_down, seq)
    x2 = _mixer_b(x2, l1_attn_norm, l1_w_qkv, l1_q_gain, l1_k_gain, l1_lambda_q1, l1_lambda_k1, l1_lambda_q2,
                  l1_lambda_k2, l1_sub_gain, l1_w_o, bias_b, _lambda_init(1), batch, seq)
    x2 = _conv_ffn(x2, l1_ffn_norm, l1_w_up, l1_conv_w, l1_conv_b, l1_w_down, seq)
    x2 = _mixer_c(x2, l2_attn_norm, l2_w_qkv, l2_q_gain, l2_k_gain, l2_w_o, bias_c, batch, seq)
    x2 = _conv_ffn(x2, l2_ffn_norm, l2_w_up, l2_conv_w, l2_conv_b, l2_w_down, seq)
    x2 = _mixer_a(x2, l3_attn_norm, l3_w_qkv, l3_q_gain, l3_k_gain, l3_sink, l3_w_o, bias_a, batch, seq)
    x2 = _conv_ffn(x2, l3_ffn_norm, l3_w_up, l3_conv_w, l3_conv_b, l3_w_down, seq)
    return x2.reshape(batch, seq, D_MODEL)
```

```python
import functools
import math

import jax
import jax.numpy as jnp
from jax import lax
from jax.experimental import pallas as pl
from jax.experimental.pallas import tpu as pltpu

F32 = jnp.float32
BF16 = jnp.bfloat16

D_MODEL = 1024
EPS = 1e-6
NEG_INF = -1e30
LOG2E = math.log2(math.e)
LANES = 128
MXU_TILE = 256
VMEM_LIMIT = 56 * 1024 * 1024

NUM_BUCKETS = 32
MAX_DISTANCE = 1024
N_BIAS_HEADS = 16

A_HEADS, A_KV_HEADS, A_HEAD_DIM, A_WINDOW = 16, 4, 64, 128
B_HEADS, B_HEAD_DIM = 8, 64
B_FAR_TILES = 6
C_BRANCHES = ((128, 1), (512, 4), (2048, 16))
C_KV_HEADS, C_HEAD_DIM = 4, 128
C_HALF = 64
D_FF = 2816
FFN_CHUNK = 256
FFN_HALO = 16


def _params(sem, vmem=VMEM_LIMIT):
    return pltpu.CompilerParams(dimension_semantics=sem, vmem_limit_bytes=vmem)


def _rms(x, gain):
    return x * lax.rsqrt(jnp.mean(x * x, axis=-1, keepdims=True) + EPS) * gain


def _bias_tile_kernel(tab_ref, dil_ref, off_ref, out_ref, *, half, heads_per_group):
    grp = pl.program_id(0)
    t = pl.program_id(1)
    dil = dil_ref[grp]
    off = off_ref[t]
    q = lax.broadcasted_iota(jnp.int32, (LANES, LANES), 0)
    c = lax.broadcasted_iota(jnp.int32, (LANES, LANES), 1)
    x = c - q + off
    rel = x * dil
    nb = NUM_BUCKETS // 2
    max_exact = nb // 2
    n = jnp.abs(rel)
    nf = jnp.maximum(n, 1).astype(F32)
    large = max_exact + (jnp.log(nf * (1.0 / max_exact)) / math.log(MAX_DISTANCE / max_exact)
                         * (nb - max_exact)).astype(jnp.int32)
    large = jnp.minimum(large, nb - 1)
    bucket = jnp.where(rel > 0, nb, 0) + jnp.where(n < max_exact, n, large)
    masks = [(bucket & (1 << bit)) != 0 for bit in range(5)]
    inside = None if half is None else jnp.abs(x) <= half
    for hh in range(heads_per_group):
        col = grp * heads_per_group + hh
        level = [tab_ref[b, col] for b in range(NUM_BUCKETS)]
        for mask in masks:
            level = [jnp.where(mask, level[2 * k + 1], level[2 * k]) for k in range(len(level) // 2)]
        bias = level[0] * LOG2E
        if inside is not None:
            bias = jnp.where(inside, bias, NEG_INF)
        out_ref[hh] = bias


def _bias_tiles(rel_bias, n_heads, group_dils, offs, half):
    n_groups, nt = len(group_dils), len(offs)
    hpg = n_heads // n_groups
    smem = pl.BlockSpec(memory_space=pltpu.SMEM)
    return pl.pallas_call(
        functools.partial(_bias_tile_kernel, half=half, heads_per_group=hpg),
        out_shape=jax.ShapeDtypeStruct((n_heads, nt, LANES, LANES), F32),
        grid=(n_groups, nt),
        in_specs=[smem, smem, smem],
        out_specs=pl.BlockSpec((hpg, None, LANES, LANES), lambda g, t: (g, t, 0, 0)),
        compiler_params=_params(("arbitrary", "arbitrary")),
        name="bias_tiles",
    )(rel_bias, jnp.asarray(group_dils, jnp.int32), jnp.asarray(offs, jnp.int32))


def _proj_kernel(x_ref, g_ref, w_ref, cs_ref, p_ref, *out_refs, widths, n_norm_cols, group):
    h = _rms(x_ref[...], g_ref[...]).astype(BF16)
    group_w = 2 * MXU_TILE
    tiles = [(o_ref, off) for o_ref, width in zip(out_refs, widths) for off in range(0, width, group_w)]

    def project(t):
        return jnp.dot(h, w_ref[:, t * group_w:(t + 1) * group_w], preferred_element_type=F32)

    def finish(t, y):
        o_ref, off = tiles[t]
        for s in range(group_w // MXU_TILE):
            col = t * group_w + s * MXU_TILE
            ys = y[:, s * MXU_TILE:(s + 1) * MXU_TILE]
            if col < n_norm_cols:
                ss = jnp.dot((ys * ys).astype(BF16), p_ref[...], preferred_element_type=F32)
                ys = ys * lax.rsqrt(ss * (1.0 / group) + EPS) * cs_ref[:, col:col + MXU_TILE]
            o_ref[:, off + s * MXU_TILE:off + (s + 1) * MXU_TILE] = ys.astype(BF16)

    y = project(0)
    for t in range(len(tiles)):
        y_next = project(t + 1) if t + 1 < len(tiles) else None
        finish(t, y)
        y = y_next


def _project(x2, gain, w, colscale, widths, n_norm_cols, group, tm=512):
    m = x2.shape[0]
    n = w.shape[1]
    assert sum(widths) == n and m % tm == 0 and n_norm_cols % MXU_TILE == 0
    r = jnp.arange(MXU_TILE) // group
    ones_bd = (r[:, None] == r[None, :]).astype(BF16)
    const = lambda i: (0, 0)
    return pl.pallas_call(
        functools.partial(_proj_kernel, widths=tuple(widths), n_norm_cols=n_norm_cols, group=group),
        out_shape=[jax.ShapeDtypeStruct((m, wd), BF16) for wd in widths],
        grid=(m // tm,),
        in_specs=[pl.BlockSpec((tm, D_MODEL), lambda i: (i, 0)),
                  pl.BlockSpec((1, D_MODEL), const),
                  pl.BlockSpec((D_MODEL, n), const),
                  pl.BlockSpec((1, n), const),
                  pl.BlockSpec((MXU_TILE, MXU_TILE), const)],
        out_specs=[pl.BlockSpec((tm, wd), lambda i: (i, 0)) for wd in widths],
        compiler_params=_params(("arbitrary",)),
        name="qkv_proj",
    )(x2, gain.reshape(1, D_MODEL), w, colscale.reshape(1, n), ones_bd)


def _attn_a_kernel(sink_ref, q_ref, k_ref, v_ref, bias_ref, o_ref, *, nq, n_tiles):
    i = pl.program_id(1)
    lane = lax.broadcasted_iota(jnp.int32, (LANES, LANES), 1)
    lo = lane < A_HEAD_DIM
    group = A_HEADS // A_KV_HEADS
    for s in range(nq):
        t = i * nq + s
        base = jnp.clip(t - 1, 0, n_tiles - 3)
        row0 = pl.multiple_of(base * LANES, LANES)
        rows = slice(s * LANES, (s + 1) * LANES)
        for hk in range(A_KV_HEADS):
            kv_cols = slice(hk * LANES, (hk + 1) * LANES)
            kw = k_ref[pl.ds(row0, 3 * LANES), kv_cols]
            vw = v_ref[pl.ds(row0, 3 * LANES), kv_cols]
            parts = []
            for c in (2 * hk, 2 * hk + 1):
                qc = q_ref[rows, c * LANES:(c + 1) * LANES]
                parts += [jnp.where(lo, qc, jnp.zeros_like(qc)), jnp.where(lo, jnp.zeros_like(qc), qc)]
            qs = jnp.concatenate(parts, axis=0)
            sc = lax.dot_general(qs, kw, (((1,), (1,)), ((), ())), preferred_element_type=F32)
            blocks = []
            for g in range(group):
                hq = hk * group + g
                blocks.append(jnp.concatenate(
                    [sc[g * LANES:(g + 1) * LANES, w * LANES:(w + 1) * LANES] + bias_ref[hq, base + w - t + 2]
                     for w in range(3)], axis=1))
            sc = jnp.concatenate(blocks, axis=0)
            mx = jnp.max(sc, axis=-1, keepdims=True)
            p = jnp.exp2(sc - mx)
            den = jnp.sum(p, axis=-1, keepdims=True)
            pv = jnp.dot(p.astype(BF16), vw, preferred_element_type=F32)
            outs = []
            for g in range(group):
                hq = hk * group + g
                gr = slice(g * LANES, (g + 1) * LANES)
                outs.append(pv[gr] / (den[gr] + jnp.exp2(sink_ref[hq] * LOG2E - mx[gr])))
            o_ref[rows, (2 * hk) * LANES:(2 * hk + 1) * LANES] = jnp.where(lo, outs[0], outs[1]).astype(BF16)
            o_ref[rows, (2 * hk + 1) * LANES:(2 * hk + 2) * LANES] = jnp.where(lo, outs[2], outs[3]).astype(BF16)


def _attention_a(qkv, bias, sink, batch, seq, tq=256):
    nq = tq // LANES
    n_tiles = seq // LANES
    qkv3 = qkv.reshape(batch, seq, 2 * D_MODEL)
    return pl.pallas_call(
        functools.partial(_attn_a_kernel, nq=nq, n_tiles=n_tiles),
        out_shape=jax.ShapeDtypeStruct((batch, seq, D_MODEL), BF16),
        grid=(batch, seq // tq),
        in_specs=[pl.BlockSpec(memory_space=pltpu.SMEM),
                  pl.BlockSpec((None, tq, D_MODEL), lambda b, i: (b, i, 0)),
                  pl.BlockSpec((None, seq, 512), lambda b, i: (b, 0, 2)),
                  pl.BlockSpec((None, seq, 512), lambda b, i: (b, 0, 3)),
                  pl.BlockSpec((A_HEADS, 5, LANES, LANES), lambda b, i: (0, 0, 0, 0))],
        out_specs=pl.BlockSpec((None, tq, D_MODEL), lambda b, i: (b, i, 0)),
        compiler_params=_params(("arbitrary", "arbitrary")),
        name="attn_window",
    )(sink, qkv3, qkv3, qkv3, bias)


def _attn_b_kernel(lam_ref, sg_ref, q_ref, k_ref, v_ref, bias_ref, o_ref, qs_ref, m_ref, l_ref, acc_ref,
                   *, tq, tk, n_kc, unroll, lambda_init):
    i = pl.program_id(2)
    nqt = tq // LANES
    nkt = tk // LANES
    lo = lax.broadcasted_iota(jnp.int32, (tq, LANES), 1) < B_HEAD_DIM
    q = q_ref[...]
    zero = jnp.zeros_like(q)
    qs_ref[0:tq, :] = jnp.where(lo, q, zero)
    qs_ref[tq:2 * tq, :] = jnp.where(lo, zero, q)
    m_ref[...] = jnp.full(m_ref.shape, -jnp.inf, F32)
    l_ref[...] = jnp.zeros(l_ref.shape, F32)
    acc_ref[...] = jnp.zeros(acc_ref.shape, F32)

    def chunk(kc, carry):
        k0 = pl.multiple_of(kc * tk, tk)
        kw = k_ref[pl.ds(k0, tk), :]
        vw = v_ref[pl.ds(k0, tk), :]
        sc = lax.dot_general(qs_ref[...], kw, (((1,), (1,)), ((), ())), preferred_element_type=F32)
        probs = []
        for rb in range(2 * nqt):
            mp, a = divmod(rb, nqt)
            rows = slice(rb * LANES, (rb + 1) * LANES)
            tiles = []
            for w in range(nkt):
                d = jnp.clip(kc * nkt + w - (i * nqt + a), -B_FAR_TILES, B_FAR_TILES) + B_FAR_TILES
                tiles.append(sc[rows, w * LANES:(w + 1) * LANES] + bias_ref[mp, d])
            m_old = m_ref[rows, :]
            m_new = jnp.maximum(m_old, jnp.max(functools.reduce(jnp.maximum, tiles), axis=-1, keepdims=True))
            alpha = jnp.exp2(m_old - m_new)
            p = [jnp.exp2(t - m_new) for t in tiles]
            l_ref[rows, :] = alpha * l_ref[rows, :] + functools.reduce(jnp.add, p)
            acc_ref[rows, :] = alpha * acc_ref[rows, :]
            m_ref[rows, :] = m_new
            probs.append(jnp.concatenate(p, axis=1).astype(BF16))
        acc_ref[...] += jnp.dot(jnp.concatenate(probs, axis=0), vw, preferred_element_type=F32)
        return carry

    lax.fori_loop(0, n_kc, chunk, 0, unroll=unroll)
    lv = lam_ref[...]
    lam = (jnp.exp(jnp.sum(lv[0:1] * lv[1:2], axis=-1, keepdims=True))
           - jnp.exp(jnp.sum(lv[2:3] * lv[3:4], axis=-1, keepdims=True)) + lambda_init)
    o = acc_ref[...] / jnp.sum(l_ref[...], axis=-1, keepdims=True)
    o = o[:tq] - lam * o[tq:]
    o_ref[...] = (_rms(o, sg_ref[...]) * (1.0 - lambda_init)).astype(BF16)


def _attention_b(qkv, bias, lam_vecs, sub_gain, lambda_init, batch, seq, tq=512, tk=512, unroll=True):
    qkv3 = qkv.reshape(batch, seq, 3 * D_MODEL)
    nb = 2 * B_FAR_TILES + 1
    return pl.pallas_call(
        functools.partial(_attn_b_kernel, tq=tq, tk=tk, n_kc=seq // tk, unroll=unroll, lambda_init=lambda_init),
        out_shape=jax.ShapeDtypeStruct((batch, seq, D_MODEL), BF16),
        grid=(batch, B_HEADS, seq // tq),
        in_specs=[pl.BlockSpec((4, B_HEAD_DIM), lambda b, h, i: (0, 0)),
                  pl.BlockSpec((1, LANES), lambda b, h, i: (0, 0)),
                  pl.BlockSpec((None, tq, LANES), lambda b, h, i: (b, i, h)),
                  pl.BlockSpec((None, seq, LANES), lambda b, h, i: (b, 0, B_HEADS + h)),
                  pl.BlockSpec((None, seq, LANES), lambda b, h, i: (b, 0, 2 * B_HEADS + h)),
                  pl.BlockSpec((None, 2, nb, LANES, LANES), lambda b, h, i: (h, 0, 0, 0, 0))],
        out_specs=pl.BlockSpec((None, tq, LANES), lambda b, h, i: (b, i, h)),
        scratch_shapes=[pltpu.VMEM((2 * tq, LANES), BF16), pltpu.VMEM((2 * tq, LANES), F32),
                        pltpu.VMEM((2 * tq, LANES), F32), pltpu.VMEM((2 * tq, LANES), F32)],
        compiler_params=_params(("arbitrary", "arbitrary", "arbitrary")),
        name="attn_diff",
    )(lam_vecs, sub_gain.reshape(1, LANES), qkv3, qkv3, qkv3, bias)


def _attn_c_kernel(q_ref, k_ref, v_ref, bias_ref, o_ref, lse_ref, *, dil, nt, sub_len):
    i = pl.program_id(1)
    width = C_KV_HEADS * C_HEAD_DIM
    for tt in range(nt):
        t = i * nt + tt
        start = pl.multiple_of(jnp.clip(t * LANES - C_HALF, 0, sub_len - 2 * LANES), C_HALF)
        rows = slice(tt * LANES, (tt + 1) * LANES)
        for r in range(dil):
            for j in range(C_KV_HEADS):
                cols = slice(r * width + j * C_HEAD_DIM, r * width + (j + 1) * C_HEAD_DIM)
                kw = k_ref[pl.ds(start, 2 * LANES), cols]
                vw = v_ref[pl.ds(start, 2 * LANES), cols]
                sc = lax.dot_general(q_ref[rows, cols], kw, (((1,), (1,)), ((), ())),
                                     preferred_element_type=F32)
                sc = jnp.concatenate(
                    [sc[:, w * LANES:(w + 1) * LANES]
                     + bias_ref[j, (start + w * LANES - t * LANES) // C_HALF + 2] for w in range(2)], axis=1)
                mx = jnp.max(sc, axis=-1, keepdims=True)
                p = jnp.exp2(sc - mx)
                den = jnp.sum(p, axis=-1, keepdims=True)
                o = jnp.dot(p.astype(BF16), vw, preferred_element_type=F32) / den
                o_ref[rows, cols] = o.astype(BF16)
                lse_ref[rows, r * C_KV_HEADS + j:r * C_KV_HEADS + j + 1] = mx + jnp.log2(den)


def _attention_c_branch(q, k, v, bias, group, dil, batch, seq):
    width = C_KV_HEADS * C_HEAD_DIM
    sub_len = seq // dil
    nt = max(1, 8 // dil)
    rows = nt * LANES
    view = lambda a: a.reshape(batch, sub_len, dil * width)
    o, lse = pl.pallas_call(
        functools.partial(_attn_c_kernel, dil=dil, nt=nt, sub_len=sub_len),
        out_shape=[jax.ShapeDtypeStruct((batch, sub_len, dil * width), BF16),
                   jax.ShapeDtypeStruct((batch, sub_len, dil * C_KV_HEADS), F32)],
        grid=(batch, sub_len // rows),
        in_specs=[pl.BlockSpec((None, rows, dil * width), lambda b, i: (b, i, 0)),
                  pl.BlockSpec((None, sub_len, dil * width), lambda b, i: (b, 0, 0)),
                  pl.BlockSpec((None, sub_len, dil * width), lambda b, i: (b, 0, 0)),
                  pl.BlockSpec((C_KV_HEADS, 5, LANES, LANES), lambda b, i: (group, 0, 0, 0))],
        out_specs=[pl.BlockSpec((None, rows, dil * width), lambda b, i: (b, i, 0)),
                   pl.BlockSpec((None, rows, dil * C_KV_HEADS), lambda b, i: (b, i, 0))],
        compiler_params=_params(("arbitrary", "arbitrary")),
        name="attn_dilated_%d" % dil,
    )(view(q), view(k), view(v), bias)
    return o.reshape(batch * seq, width), lse.reshape(batch * seq, C_KV_HEADS)


def _merge_c_kernel(o0_ref, o1_ref, o2_ref, l0_ref, l1_ref, l2_ref, y_ref):
    lses = [l0_ref[...], l1_ref[...], l2_ref[...]]
    mx = jnp.maximum(jnp.maximum(lses[0], lses[1]), lses[2])
    es = [jnp.exp2(l - mx) for l in lses]
    inv = 1.0 / (es[0] + es[1] + es[2])
    for j in range(C_KV_HEADS):
        cols = slice(j * C_HEAD_DIM, (j + 1) * C_HEAD_DIM)
        acc = None
        for e, o_ref in zip(es, (o0_ref, o1_ref, o2_ref)):
            term = (e[:, j:j + 1] * inv[:, j:j + 1]) * o_ref[:, cols].astype(F32)
            acc = term if acc is None else acc + term
        y_ref[:, cols] = acc.astype(BF16)


def _merge_c(outs, lses, tm=1024):
    m, width = outs[0].shape
    row = lambda i: (i, 0)
    return pl.pallas_call(
        _merge_c_kernel,
        out_shape=jax.ShapeDtypeStruct((m, width), BF16),
        grid=(m // tm,),
        in_specs=[pl.BlockSpec((tm, width), row)] * 3 + [pl.BlockSpec((tm, C_KV_HEADS), row)] * 3,
        out_specs=pl.BlockSpec((tm, width), row),
        compiler_params=_params(("arbitrary",)),
        name="branch_merge",
    )(*outs, *lses)


def _ffn_kernel(x_ref, xn_ref, xp_ref, o_ref, on_ref, op_ref, wo_ref, g_ref, wup_ref, cw_ref, cb_ref, wdn_ref,
                y_ref, oe_ref, h_ref, acc_ref, ua_ref, ub_ref, *, tm, tiles_per_seq, n_chunks):
    i = pl.program_id(0)
    pos = i % tiles_per_seq
    kdim = o_ref.shape[1]
    halo_row = lax.broadcasted_iota(jnp.int32, (FFN_HALO, kdim), 0)
    oe_ref[0:tm, :] = o_ref[...]
    oe_ref[tm:tm + FFN_HALO, :] = jnp.where(halo_row < 8, on_ref[...], op_ref[...])
    attn = jnp.dot(oe_ref[...], wo_ref[...], preferred_element_type=F32)
    x1 = x_ref[...] + attn[0:tm]
    x1n = xn_ref[...] + attn[tm:tm + 8]
    x1p = xp_ref[...] + attn[tm + 8:tm + FFN_HALO]
    g = g_ref[...]
    h_ref[0:tm, :] = _rms(x1, g).astype(BF16)
    hn = jnp.where(pos == tiles_per_seq - 1, 0.0, _rms(x1n, g))
    hp = jnp.where(pos == 0, 0.0, _rms(x1p, g))
    h_ref[tm:tm + FFN_HALO, :] = jnp.concatenate([hn, hp], axis=0).astype(BF16)
    acc_ref[...] = x1
    slabs = FFN_CHUNK // LANES

    def chunk_cols(k):
        return pl.ds(pl.multiple_of(k * FFN_CHUNK, FFN_CHUNK), FFN_CHUNK)

    def up_proj(j, u_ref):
        h = h_ref[...]
        for half, k in enumerate((j, j + n_chunks)):
            u = jnp.dot(h, wup_ref[:, chunk_cols(k)], preferred_element_type=F32)
            for s in range(slabs):
                cols = slice(s * LANES, (s + 1) * LANES)
                slab = half * slabs + s
                u_ref[slab, 8:8 + tm, :] = u[0:tm, cols]
                u_ref[slab, 0:8, :] = u[tm + 8:tm + 16, cols]
                u_ref[slab, tm + 8:tm + 16, :] = u[tm:tm + 8, cols]

    def conv(u_ref, slab, k, cols):
        w = cw_ref[:, chunk_cols(k)]
        return (w[0:1, cols] * u_ref[slab, pl.ds(7, tm), :] + w[1:2, cols] * u_ref[slab, pl.ds(8, tm), :]
                + w[2:3, cols] * u_ref[slab, pl.ds(9, tm), :] + cb_ref[:, chunk_cols(k)][:, cols])

    def gate_down(j, u_ref):
        acts = []
        for s in range(slabs):
            cols = slice(s * LANES, (s + 1) * LANES)
            gate = conv(u_ref, s, j, cols)
            val = conv(u_ref, slabs + s, j + n_chunks, cols)
            acts.append((gate / (1.0 + jnp.exp(-gate)) * val).astype(BF16))
        acc_ref[...] += jnp.dot(jnp.concatenate(acts, axis=1), wdn_ref[chunk_cols(j), :],
                                preferred_element_type=F32)

    up_proj(0, ua_ref)

    def pair(c, carry):
        j = 2 * c
        up_proj(j + 1, ub_ref)
        gate_down(j, ua_ref)
        up_proj(j + 2, ua_ref)
        gate_down(j + 1, ub_ref)
        return carry

    assert n_chunks % 2 == 1
    lax.fori_loop(0, n_chunks // 2, pair, 0)
    gate_down(n_chunks - 1, ua_ref)
    y_ref[...] = acc_ref[...]


def _proj_ffn(x2, o2, w_o, gain, w_up, conv_w, conv_b, w_down, seq, tm=512):
    m = x2.shape[0]
    kdim = o2.shape[1]
    n_chunks = D_FF // FFN_CHUNK
    tiles_per_seq = seq // tm
    x_halo = tm // 8
    o_halo = tm // FFN_HALO
    const = lambda i: (0, 0)
    once = pl.Buffered(1)
    nxt = lambda blocks, per_tile: (lambda i: (jnp.minimum((i + 1) * per_tile, blocks - 1), 0))
    prv = lambda per_tile: (lambda i: (jnp.maximum(i * per_tile - 1, 0), 0))
    return pl.pallas_call(
        functools.partial(_ffn_kernel, tm=tm, tiles_per_seq=tiles_per_seq, n_chunks=n_chunks),
        out_shape=jax.ShapeDtypeStruct((m, D_MODEL), F32),
        grid=(m // tm,),
        in_specs=[pl.BlockSpec((tm, D_MODEL), lambda i: (i, 0)),
                  pl.BlockSpec((8, D_MODEL), nxt(m // 8, x_halo)),
                  pl.BlockSpec((8, D_MODEL), prv(x_halo)),
                  pl.BlockSpec((tm, kdim), lambda i: (i, 0)),
                  pl.BlockSpec((FFN_HALO, kdim), nxt(m // FFN_HALO, o_halo)),
                  pl.BlockSpec((FFN_HALO, kdim), prv(o_halo)),
                  pl.BlockSpec((kdim, D_MODEL), const, pipeline_mode=once),
                  pl.BlockSpec((1, D_MODEL), const),
                  pl.BlockSpec((D_MODEL, 2 * D_FF), const, pipeline_mode=once),
                  pl.BlockSpec((3, 2 * D_FF), const),
                  pl.BlockSpec((1, 2 * D_FF), const),
                  pl.BlockSpec((D_FF, D_MODEL), const, pipeline_mode=once)],
        out_specs=pl.BlockSpec((tm, D_MODEL), lambda i: (i, 0)),
        scratch_shapes=[pltpu.VMEM((tm + FFN_HALO, kdim), BF16),
                        pltpu.VMEM((tm + FFN_HALO, D_MODEL), BF16), pltpu.VMEM((tm, D_MODEL), F32),
                        pltpu.VMEM((2 * FFN_CHUNK // LANES, tm + FFN_HALO, LANES), F32),
                        pltpu.VMEM((2 * FFN_CHUNK // LANES, tm + FFN_HALO, LANES), F32)],
        compiler_params=_params(("arbitrary",)),
        name="proj_conv_ffn",
    )(x2, x2, x2, o2, o2, o2, w_o.astype(BF16), gain.reshape(1, D_MODEL), w_up.astype(BF16),
      conv_w.reshape(3, 2 * D_FF), conv_b.reshape(1, 2 * D_FF), w_down.astype(BF16))


def _dup_heads(w, n_heads, dim):
    w = w.reshape(D_MODEL, n_heads, 1, dim)
    return jnp.broadcast_to(w, (D_MODEL, n_heads, 2, dim)).reshape(D_MODEL, 2 * n_heads * dim)


def _mixer_a(x2, norm_gain, w_qkv, q_gain, k_gain, sink, bias, batch, seq):
    nq = A_HEADS * A_HEAD_DIM
    nk = A_KV_HEADS * A_HEAD_DIM
    w = jnp.concatenate([w_qkv[:, :nq], _dup_heads(w_qkv[:, nq:nq + nk], A_KV_HEADS, A_HEAD_DIM),
                         _dup_heads(w_qkv[:, nq + nk:], A_KV_HEADS, A_HEAD_DIM)], axis=1).astype(BF16)
    colscale = jnp.concatenate([jnp.tile(q_gain, A_HEADS) * (LOG2E * A_HEAD_DIM ** -0.5), jnp.tile(k_gain, 2 * A_KV_HEADS),
                                jnp.ones((2 * nk,), F32)])
    (qkv,) = _project(x2, norm_gain, w, colscale, [2 * D_MODEL], nq + 2 * nk, A_HEAD_DIM)
    return _attention_a(qkv, bias, sink, batch, seq).reshape(batch * seq, D_MODEL)


def _mixer_b(x2, norm_gain, w_qkv, q_gain, k_gain, lam_q1, lam_k1, lam_q2, lam_k2, sub_gain, bias,
             lambda_init, batch, seq):
    colscale = jnp.concatenate([jnp.tile(q_gain, 2 * B_HEADS) * (LOG2E * B_HEAD_DIM ** -0.5), jnp.tile(k_gain, 2 * B_HEADS),
                                jnp.ones((D_MODEL,), F32)])
    (qkv,) = _project(x2, norm_gain, w_qkv.astype(BF16), colscale, [3 * D_MODEL], 2 * D_MODEL, B_HEAD_DIM)
    lam_vecs = jnp.stack([lam_q1, lam_k1, lam_q2, lam_k2])
    return _attention_b(qkv, bias, lam_vecs, sub_gain, lambda_init, batch, seq).reshape(batch * seq, D_MODEL)


def _mixer_c(x2, norm_gain, w_qkv, q_gain, k_gain, bias, batch, seq):
    width = C_KV_HEADS * C_HEAD_DIM
    n_groups = len(C_BRANCHES)
    colscale = jnp.concatenate([jnp.tile(q_gain, n_groups * C_KV_HEADS) * (LOG2E * C_HEAD_DIM ** -0.5),
                                jnp.tile(k_gain, C_KV_HEADS), jnp.ones((width,), F32)])
    *qs, k, v = _project(x2, norm_gain, w_qkv.astype(BF16), colscale, [width] * (n_groups + 2),
                         (n_groups + 1) * width, C_HEAD_DIM)
    outs, lses = [], []
    for g, (window, dil) in enumerate(C_BRANCHES):
        assert window // (2 * dil) == C_HALF
        o, lse = _attention_c_branch(qs[g], k, v, bias, g, dil, batch, seq)
        outs.append(o)
        lses.append(lse)
    return _merge_c(outs, lses)


def _lambda_init(layer):
    return 0.8 - 0.6 * math.exp(-0.3 * layer)


def kernel(x, rel_bias, l0_attn_norm, l0_w_qkv, l0_q_gain, l0_k_gain, l0_sink, l0_w_o, l0_ffn_norm, l0_w_up, l0_conv_w, l0_conv_b, l0_w_down, l1_attn_norm, l1_w_qkv, l1_q_gain, l1_k_gain, l1_lambda_q1, l1_lambda_k1, l1_lambda_q2, l1_lambda_k2, l1_sub_gain, l1_w_o, l1_ffn_norm, l1_w_up, l1_conv_w, l1_conv_b, l1_w_down, l2_attn_norm, l2_w_qkv, l2_q_gain, l2_k_gain, l2_w_o, l2_ffn_norm, l2_w_up, l2_conv_w, l2_conv_b, l2_w_down, l3_attn_norm, l3_w_qkv, l3_q_gain, l3_k_gain, l3_sink, l3_w_o, l3_ffn_norm, l3_w_up, l3_conv_w, l3_conv_b, l3_w_down):
    batch, seq, _ = x.shape
    assert seq % (2 * LANES * C_BRANCHES[-1][1]) == 0
    bias_a = _bias_tiles(rel_bias, A_HEADS, [1], [LANES * d for d in range(-2, 3)], A_WINDOW)
    bias_b = _bias_tiles(rel_bias, 2 * B_HEADS, [1], [LANES * d for d in range(-B_FAR_TILES, B_FAR_TILES + 1)], None)
    bias_c = _bias_tiles(rel_bias, len(C_BRANCHES) * C_KV_HEADS, [dil for _, dil in C_BRANCHES],
                         [C_HALF * d for d in range(-2, 3)], C_HALF)
    bias_b = bias_b.reshape(B_HEADS, 2, 2 * B_FAR_TILES + 1, LANES, LANES)

    x2 = x.reshape(batch * seq, D_MODEL)
    o = _mixer_a(x2, l0_attn_norm, l0_w_qkv, l0_q_gain, l0_k_gain, l0_sink, bias_a, batch, seq)
    x2 = _proj_ffn(x2, o, l0_w_o, l0_ffn_norm, l0_w_up, l0_conv_w, l0_conv_b, l0_w_down, seq)
    o = _mixer_b(x2, l1_attn_norm, l1_w_qkv, l1_q_gain, l1_k_gain, l1_lambda_q1, l1_lambda_k1, l1_lambda_q2,
                 l1_lambda_k2, l1_sub_gain, bias_b, _lambda_init(1), batch, seq)
    x2 = _proj_ffn(x2, o, l1_w_o, l1_ffn_norm, l1_w_up, l1_conv_w, l1_conv_b, l1_w_down, seq)
    o = _mixer_c(x2, l2_attn_norm, l2_w_qkv, l2_q_gain, l2_k_gain, bias_c, batch, seq)
    x2 = _proj_ffn(x2, o, l2_w_o, l2_ffn_norm, l2_w_up, l2_conv_w, l2_conv_b, l2_w_down, seq)
    o = _mixer_a(x2, l3_attn_norm, l3_w_qkv, l3_q_gain, l3_k_gain, l3_sink, bias_a, batch, seq)
    x2 = _proj_ffn(x2, o, l3_w_o, l3_ffn_norm, l3_w_up, l3_conv_w, l3_conv_b, l3_w_down, seq)
    return x2.reshape(batch, seq, D_MODEL)
```

```python
import functools
import math

import jax
import jax.numpy as jnp
from jax import lax
from jax.experimental import pallas as pl
from jax.experimental.pallas import tpu as pltpu

F32 = jnp.float32
BF16 = jnp.bfloat16

D_MODEL = 1024
EPS = 1e-6
NEG_INF = -1e30
LOG2E = math.log2(math.e)
LANES = 128
MXU_TILE = 256
VMEM_LIMIT = 56 * 1024 * 1024

NUM_BUCKETS = 32
MAX_DISTANCE = 1024
N_BIAS_HEADS = 16

A_HEADS, A_KV_HEADS, A_HEAD_DIM, A_WINDOW = 16, 4, 64, 128
B_HEADS, B_HEAD_DIM = 8, 64
B_FAR_TILES = 6
C_BRANCHES = ((128, 1), (512, 4), (2048, 16))
C_KV_HEADS, C_HEAD_DIM = 4, 128
C_HALF = 64
D_FF = 2816
FFN_CHUNK = 256
FFN_HALO = 16


def _params(sem, vmem=VMEM_LIMIT):
    return pltpu.CompilerParams(dimension_semantics=sem, vmem_limit_bytes=vmem)


def _rms(x, gain):
    return x * lax.rsqrt(jnp.mean(x * x, axis=-1, keepdims=True) + EPS) * gain


def _bias_tile_kernel(tab_ref, dil_ref, off_ref, out_ref, *, half, heads_per_group):
    grp = pl.program_id(0)
    t = pl.program_id(1)
    dil = dil_ref[grp]
    off = off_ref[t]
    q = lax.broadcasted_iota(jnp.int32, (LANES, LANES), 0)
    c = lax.broadcasted_iota(jnp.int32, (LANES, LANES), 1)
    x = c - q + off
    rel = x * dil
    nb = NUM_BUCKETS // 2
    max_exact = nb // 2
    n = jnp.abs(rel)
    nf = jnp.maximum(n, 1).astype(F32)
    large = max_exact + (jnp.log(nf * (1.0 / max_exact)) / math.log(MAX_DISTANCE / max_exact)
                         * (nb - max_exact)).astype(jnp.int32)
    large = jnp.minimum(large, nb - 1)
    bucket = jnp.where(rel > 0, nb, 0) + jnp.where(n < max_exact, n, large)
    masks = [(bucket & (1 << bit)) != 0 for bit in range(5)]
    inside = None if half is None else jnp.abs(x) <= half
    for hh in range(heads_per_group):
        col = grp * heads_per_group + hh
        level = [tab_ref[b, col] for b in range(NUM_BUCKETS)]
        for mask in masks:
            level = [jnp.where(mask, level[2 * k + 1], level[2 * k]) for k in range(len(level) // 2)]
        bias = level[0] * LOG2E
        if inside is not None:
            bias = jnp.where(inside, bias, NEG_INF)
        out_ref[hh] = bias


def _bias_tiles(rel_bias, n_heads, group_dils, offs, half):
    n_groups, nt = len(group_dils), len(offs)
    hpg = n_heads // n_groups
    smem = pl.BlockSpec(memory_space=pltpu.SMEM)
    return pl.pallas_call(
        functools.partial(_bias_tile_kernel, half=half, heads_per_group=hpg),
        out_shape=jax.ShapeDtypeStruct((n_heads, nt, LANES, LANES), F32),
        grid=(n_groups, nt),
        in_specs=[smem, smem, smem],
        out_specs=pl.BlockSpec((hpg, None, LANES, LANES), lambda g, t: (g, t, 0, 0)),
        compiler_params=_params(("arbitrary", "arbitrary")),
        name="bias_tiles",
    )(rel_bias, jnp.asarray(group_dils, jnp.int32), jnp.asarray(offs, jnp.int32))


def _proj_kernel(x_ref, g_ref, w_ref, cs_ref, p_ref, *out_refs, widths, n_norm_cols, group, slabs):
    h = _rms(x_ref[...], g_ref[...]).astype(BF16)
    group_w = 2 * MXU_TILE
    tiles = [(o_ref, off) for o_ref, width in zip(out_refs, widths) for off in range(0, width, group_w)]

    def project(t):
        return jnp.dot(h, w_ref[:, t * group_w:(t + 1) * group_w], preferred_element_type=F32)

    def finish(t, y):
        o_ref, off = tiles[t]
        for s in range(group_w // MXU_TILE):
            col = t * group_w + s * MXU_TILE
            ys = y[:, s * MXU_TILE:(s + 1) * MXU_TILE]
            if col < n_norm_cols:
                ss = jnp.dot((ys * ys).astype(BF16), p_ref[...], preferred_element_type=F32)
                ys = ys * lax.rsqrt(ss * (1.0 / group) + EPS) * cs_ref[:, col:col + MXU_TILE]
            lo = off + s * MXU_TILE
            if slabs:
                for half in range(MXU_TILE // LANES):
                    o_ref[lo // LANES + half] = ys[:, half * LANES:(half + 1) * LANES]
            else:
                o_ref[:, lo:lo + MXU_TILE] = ys.astype(BF16)

    y = project(0)
    for t in range(len(tiles)):
        y_next = project(t + 1) if t + 1 < len(tiles) else None
        finish(t, y)
        y = y_next


def _project(x2, gain, w, colscale, widths, n_norm_cols, group, slabs=False, tm=512):
    m = x2.shape[0]
    n = w.shape[1]
    assert sum(widths) == n and m % tm == 0 and n_norm_cols % MXU_TILE == 0
    assert all(wd % (2 * MXU_TILE) == 0 for wd in widths)
    r = jnp.arange(MXU_TILE) // group
    ones_bd = (r[:, None] == r[None, :]).astype(BF16)
    const = lambda i: (0, 0)
    if slabs:
        out_shape = [jax.ShapeDtypeStruct((wd // LANES, m, LANES), F32) for wd in widths]
        out_specs = [pl.BlockSpec((wd // LANES, tm, LANES), lambda i: (0, i, 0)) for wd in widths]
    else:
        out_shape = [jax.ShapeDtypeStruct((m, wd), BF16) for wd in widths]
        out_specs = [pl.BlockSpec((tm, wd), lambda i: (i, 0)) for wd in widths]
    return pl.pallas_call(
        functools.partial(_proj_kernel, widths=tuple(widths), n_norm_cols=n_norm_cols, group=group, slabs=slabs),
        out_shape=out_shape,
        grid=(m // tm,),
        in_specs=[pl.BlockSpec((tm, D_MODEL), lambda i: (i, 0)),
                  pl.BlockSpec((1, D_MODEL), const),
                  pl.BlockSpec((D_MODEL, n), const),
                  pl.BlockSpec((1, n), const),
                  pl.BlockSpec((MXU_TILE, MXU_TILE), const)],
        out_specs=out_specs,
        compiler_params=_params(("arbitrary",)),
        name="qkv_proj",
    )(x2, gain.reshape(1, D_MODEL), w, colscale.reshape(1, n), ones_bd)


def _attn_a_kernel(sink_ref, q_ref, k_ref, v_ref, bias_ref, o_ref, *, nq, n_tiles):
    i = pl.program_id(1)
    lane = lax.broadcasted_iota(jnp.int32, (LANES, LANES), 1)
    lo = lane < A_HEAD_DIM
    group = A_HEADS // A_KV_HEADS
    for s in range(nq):
        t = i * nq + s
        base = jnp.clip(t - 1, 0, n_tiles - 3)
        row0 = pl.multiple_of(base * LANES, LANES)
        rows = slice(s * LANES, (s + 1) * LANES)
        for hk in range(A_KV_HEADS):
            kv_cols = slice(hk * LANES, (hk + 1) * LANES)
            kw = k_ref[pl.ds(row0, 3 * LANES), kv_cols]
            vw = v_ref[pl.ds(row0, 3 * LANES), kv_cols]
            parts = []
            for c in (2 * hk, 2 * hk + 1):
                qc = q_ref[rows, c * LANES:(c + 1) * LANES]
                parts += [jnp.where(lo, qc, jnp.zeros_like(qc)), jnp.where(lo, jnp.zeros_like(qc), qc)]
            qs = jnp.concatenate(parts, axis=0)
            sc = lax.dot_general(qs, kw, (((1,), (1,)), ((), ())), preferred_element_type=F32)
            blocks = []
            for g in range(group):
                hq = hk * group + g
                blocks.append(jnp.concatenate(
                    [sc[g * LANES:(g + 1) * LANES, w * LANES:(w + 1) * LANES] + bias_ref[hq, base + w - t + 2]
                     for w in range(3)], axis=1))
            sc = jnp.concatenate(blocks, axis=0)
            mx = jnp.max(sc, axis=-1, keepdims=True)
            p = jnp.exp2(sc - mx)
            den = jnp.sum(p, axis=-1, keepdims=True)
            pv = jnp.dot(p.astype(BF16), vw, preferred_element_type=F32)
            outs = []
            for g in range(group):
                hq = hk * group + g
                gr = slice(g * LANES, (g + 1) * LANES)
                outs.append(pv[gr] / (den[gr] + jnp.exp2(sink_ref[hq] * LOG2E - mx[gr])))
            o_ref[rows, (2 * hk) * LANES:(2 * hk + 1) * LANES] = jnp.where(lo, outs[0], outs[1]).astype(BF16)
            o_ref[rows, (2 * hk + 1) * LANES:(2 * hk + 2) * LANES] = jnp.where(lo, outs[2], outs[3]).astype(BF16)


def _attention_a(qkv, bias, sink, batch, seq, tq=256):
    nq = tq // LANES
    n_tiles = seq // LANES
    qkv3 = qkv.reshape(batch, seq, 2 * D_MODEL)
    return pl.pallas_call(
        functools.partial(_attn_a_kernel, nq=nq, n_tiles=n_tiles),
        out_shape=jax.ShapeDtypeStruct((batch, seq, D_MODEL), BF16),
        grid=(batch, seq // tq),
        in_specs=[pl.BlockSpec(memory_space=pltpu.SMEM),
                  pl.BlockSpec((None, tq, D_MODEL), lambda b, i: (b, i, 0)),
                  pl.BlockSpec((None, seq, 512), lambda b, i: (b, 0, 2)),
                  pl.BlockSpec((None, seq, 512), lambda b, i: (b, 0, 3)),
                  pl.BlockSpec((A_HEADS, 5, LANES, LANES), lambda b, i: (0, 0, 0, 0))],
        out_specs=pl.BlockSpec((None, tq, D_MODEL), lambda b, i: (b, i, 0)),
        compiler_params=_params(("arbitrary", "arbitrary")),
        name="attn_window",
    )(sink, qkv3, qkv3, qkv3, bias)


def _attn_b_kernel(lam_ref, sg_ref, q_ref, k_ref, v_ref, bias_ref, o_ref, qs_ref, m_ref, l_ref, acc_ref,
                   *, tq, tk, n_kc, unroll, lambda_init):
    i = pl.program_id(2)
    nqt = tq // LANES
    nkt = tk // LANES
    lo = lax.broadcasted_iota(jnp.int32, (tq, LANES), 1) < B_HEAD_DIM
    q = q_ref[...]
    zero = jnp.zeros_like(q)
    qs_ref[0:tq, :] = jnp.where(lo, q, zero)
    qs_ref[tq:2 * tq, :] = jnp.where(lo, zero, q)
    m_ref[...] = jnp.full(m_ref.shape, -jnp.inf, F32)
    l_ref[...] = jnp.zeros(l_ref.shape, F32)
    acc_ref[...] = jnp.zeros(acc_ref.shape, F32)

    def chunk(kc, carry):
        k0 = pl.multiple_of(kc * tk, tk)
        kw = k_ref[pl.ds(k0, tk), :]
        vw = v_ref[pl.ds(k0, tk), :]
        sc = lax.dot_general(qs_ref[...], kw, (((1,), (1,)), ((), ())), preferred_element_type=F32)
        probs = []
        for rb in range(2 * nqt):
            mp, a = divmod(rb, nqt)
            rows = slice(rb * LANES, (rb + 1) * LANES)
            tiles = []
            for w in range(nkt):
                d = jnp.clip(kc * nkt + w - (i * nqt + a), -B_FAR_TILES, B_FAR_TILES) + B_FAR_TILES
                tiles.append(sc[rows, w * LANES:(w + 1) * LANES] + bias_ref[mp, d])
            m_old = m_ref[rows, :]
            m_new = jnp.maximum(m_old, jnp.max(functools.reduce(jnp.maximum, tiles), axis=-1, keepdims=True))
            alpha = jnp.exp2(m_old - m_new)
            p = [jnp.exp2(t - m_new) for t in tiles]
            l_ref[rows, :] = alpha * l_ref[rows, :] + functools.reduce(jnp.add, p)
            acc_ref[rows, :] = alpha * acc_ref[rows, :]
            m_ref[rows, :] = m_new
            probs.append(jnp.concatenate(p, axis=1).astype(BF16))
        acc_ref[...] += jnp.dot(jnp.concatenate(probs, axis=0), vw, preferred_element_type=F32)
        return carry

    lax.fori_loop(0, n_kc, chunk, 0, unroll=unroll)
    lv = lam_ref[...]
    lam = (jnp.exp(jnp.sum(lv[0:1] * lv[1:2], axis=-1, keepdims=True))
           - jnp.exp(jnp.sum(lv[2:3] * lv[3:4], axis=-1, keepdims=True)) + lambda_init)
    o = acc_ref[...] / jnp.sum(l_ref[...], axis=-1, keepdims=True)
    o = o[:tq] - lam * o[tq:]
    o_ref[...] = (_rms(o, sg_ref[...]) * (1.0 - lambda_init)).astype(BF16)


def _attention_b(qkv, bias, lam_vecs, sub_gain, lambda_init, batch, seq, tq=512, tk=512, unroll=True):
    qkv3 = qkv.reshape(batch, seq, 3 * D_MODEL)
    nb = 2 * B_FAR_TILES + 1
    return pl.pallas_call(
        functools.partial(_attn_b_kernel, tq=tq, tk=tk, n_kc=seq // tk, unroll=unroll, lambda_init=lambda_init),
        out_shape=jax.ShapeDtypeStruct((batch, seq, D_MODEL), BF16),
        grid=(batch, B_HEADS, seq // tq),
        in_specs=[pl.BlockSpec((4, B_HEAD_DIM), lambda b, h, i: (0, 0)),
                  pl.BlockSpec((1, LANES), lambda b, h, i: (0, 0)),
                  pl.BlockSpec((None, tq, LANES), lambda b, h, i: (b, i, h)),
                  pl.BlockSpec((None, seq, LANES), lambda b, h, i: (b, 0, B_HEADS + h)),
                  pl.BlockSpec((None, seq, LANES), lambda b, h, i: (b, 0, 2 * B_HEADS + h)),
                  pl.BlockSpec((None, 2, nb, LANES, LANES), lambda b, h, i: (h, 0, 0, 0, 0))],
        out_specs=pl.BlockSpec((None, tq, LANES), lambda b, h, i: (b, i, h)),
        scratch_shapes=[pltpu.VMEM((2 * tq, LANES), BF16), pltpu.VMEM((2 * tq, LANES), F32),
                        pltpu.VMEM((2 * tq, LANES), F32), pltpu.VMEM((2 * tq, LANES), F32)],
        compiler_params=_params(("arbitrary", "arbitrary", "arbitrary")),
        name="attn_diff",
    )(lam_vecs, sub_gain.reshape(1, LANES), qkv3, qkv3, qkv3, bias)


def _attn_c_kernel(q_ref, k_ref, v_ref, bias_ref, o_ref, og_ref, lg_ref, *, seq, npos, merge_rows):
    c = pl.program_id(2)

    def rows_of(start, n, dil):
        return pl.ds(start, n) if dil == 1 else pl.ds(start, n, stride=dil)

    for g, (window, dil) in enumerate(C_BRANCHES):
        sub_len = seq // dil
        tiles = npos // (dil * LANES)
        for r in range(dil):
            for tt in range(tiles):
                t0 = c * (npos // dil) + tt * LANES
                ws = jnp.clip(t0 - C_HALF, 0, sub_len - 2 * LANES)
                qrows = rows_of(r + dil * LANES * tt, LANES, dil)
                krows = rows_of(r + dil * ws, 2 * LANES, dil)
                qt = q_ref[g, qrows, :].astype(BF16)
                kw = k_ref[krows, :].astype(BF16)
                vw = v_ref[krows, :].astype(BF16)
                sc = lax.dot_general(qt, kw, (((1,), (1,)), ((), ())), preferred_element_type=F32)
                sc = jnp.concatenate(
                    [sc[:, w * LANES:(w + 1) * LANES] + bias_ref[g, (ws + w * LANES - t0) // C_HALF + 2]
                     for w in range(2)], axis=1)
                mx = jnp.max(sc, axis=-1, keepdims=True)
                p = jnp.exp2(sc - mx)
                den = jnp.sum(p, axis=-1, keepdims=True)
                og_ref[g, qrows, :] = jnp.dot(p.astype(BF16), vw, preferred_element_type=F32) / den
                lg_ref[g, qrows, :] = jnp.broadcast_to(mx + jnp.log2(den), (LANES, LANES))
    for blk in range(npos // merge_rows):
        rows = slice(blk * merge_rows, (blk + 1) * merge_rows)
        lses = [lg_ref[g, rows, :] for g in range(len(C_BRANCHES))]
        mx = functools.reduce(jnp.maximum, lses)
        es = [jnp.exp2(l - mx) for l in lses]
        num = functools.reduce(jnp.add, [e * og_ref[g, rows, :] for g, e in enumerate(es)])
        o_ref[rows, :] = (num / functools.reduce(jnp.add, es)).astype(BF16)


def _attention_c(q, k, v, bias, batch, seq, npos=2048, merge_rows=256):
    n_groups = len(C_BRANCHES)
    m = batch * seq
    chunks = seq // npos
    q4 = q.reshape(n_groups, C_KV_HEADS, m, LANES)
    bias5 = bias.reshape(n_groups, C_KV_HEADS, 5, LANES, LANES)
    return pl.pallas_call(
        functools.partial(_attn_c_kernel, seq=seq, npos=npos, merge_rows=merge_rows),
        out_shape=jax.ShapeDtypeStruct((m, C_KV_HEADS * C_HEAD_DIM), BF16),
        grid=(batch, C_KV_HEADS, chunks),
        in_specs=[pl.BlockSpec((n_groups, None, npos, LANES), lambda b, j, c: (0, j, b * chunks + c, 0)),
                  pl.BlockSpec((None, seq, LANES), lambda b, j, c: (j, b, 0)),
                  pl.BlockSpec((None, seq, LANES), lambda b, j, c: (j, b, 0)),
                  pl.BlockSpec((n_groups, None, 5, LANES, LANES), lambda b, j, c: (0, j, 0, 0, 0))],
        out_specs=pl.BlockSpec((npos, LANES), lambda b, j, c: (b * chunks + c, j)),
        scratch_shapes=[pltpu.VMEM((n_groups, npos, LANES), F32), pltpu.VMEM((n_groups, npos, LANES), F32)],
        compiler_params=_params(("arbitrary", "arbitrary", "arbitrary")),
        name="attn_dilated",
    )(q4, k, v, bias5)


def _ffn_kernel(x_ref, xn_ref, xp_ref, o_ref, on_ref, op_ref, wo_ref, g_ref, wup_ref, cw_ref, cb_ref, wdn_ref,
                y_ref, oe_ref, h_ref, acc_ref, ua_ref, ub_ref, *, tm, tiles_per_seq, n_chunks):
    i = pl.program_id(0)
    pos = i % tiles_per_seq
    kdim = o_ref.shape[1]
    halo_row = lax.broadcasted_iota(jnp.int32, (FFN_HALO, kdim), 0)
    oe_ref[0:tm, :] = o_ref[...]
    oe_ref[tm:tm + FFN_HALO, :] = jnp.where(halo_row < 8, on_ref[...], op_ref[...])
    attn = jnp.dot(oe_ref[...], wo_ref[...], preferred_element_type=F32)
    x1 = x_ref[...] + attn[0:tm]
    x1n = xn_ref[...] + attn[tm:tm + 8]
    x1p = xp_ref[...] + attn[tm + 8:tm + FFN_HALO]
    g = g_ref[...]
    h_ref[0:tm, :] = _rms(x1, g).astype(BF16)
    hn = jnp.where(pos == tiles_per_seq - 1, 0.0, _rms(x1n, g))
    hp = jnp.where(pos == 0, 0.0, _rms(x1p, g))
    h_ref[tm:tm + FFN_HALO, :] = jnp.concatenate([hn, hp], axis=0).astype(BF16)
    acc_ref[...] = x1
    slabs = FFN_CHUNK // LANES

    def chunk_cols(k):
        return pl.ds(pl.multiple_of(k * FFN_CHUNK, FFN_CHUNK), FFN_CHUNK)

    def up_proj(j, u_ref):
        h = h_ref[...]
        for half, k in enumerate((j, j + n_chunks)):
            u = jnp.dot(h, wup_ref[:, chunk_cols(k)], preferred_element_type=F32)
            for s in range(slabs):
                cols = slice(s * LANES, (s + 1) * LANES)
                slab = half * slabs + s
                u_ref[slab, 8:8 + tm, :] = u[0:tm, cols]
                u_ref[slab, 0:8, :] = u[tm + 8:tm + 16, cols]
                u_ref[slab, tm + 8:tm + 16, :] = u[tm:tm + 8, cols]

    def conv(u_ref, slab, k, cols):
        w = cw_ref[:, chunk_cols(k)]
        return (w[0:1, cols] * u_ref[slab, pl.ds(7, tm), :] + w[1:2, cols] * u_ref[slab, pl.ds(8, tm), :]
                + w[2:3, cols] * u_ref[slab, pl.ds(9, tm), :] + cb_ref[:, chunk_cols(k)][:, cols])

    def gate_down(j, u_ref):
        acts = []
        for s in range(slabs):
            cols = slice(s * LANES, (s + 1) * LANES)
            gate = conv(u_ref, s, j, cols)
            val = conv(u_ref, slabs + s, j + n_chunks, cols)
            acts.append((gate / (1.0 + jnp.exp(-gate)) * val).astype(BF16))
        acc_ref[...] += jnp.dot(jnp.concatenate(acts, axis=1), wdn_ref[chunk_cols(j), :],
                                preferred_element_type=F32)

    up_proj(0, ua_ref)

    def pair(c, carry):
        j = 2 * c
        up_proj(j + 1, ub_ref)
        gate_down(j, ua_ref)
        up_proj(j + 2, ua_ref)
        gate_down(j + 1, ub_ref)
        return carry

    assert n_chunks % 2 == 1
    lax.fori_loop(0, n_chunks // 2, pair, 0, unroll=True)
    gate_down(n_chunks - 1, ua_ref)
    y_ref[...] = acc_ref[...]


def _proj_ffn(x2, o2, w_o, gain, w_up, conv_w, conv_b, w_down, seq, tm=512):
    m = x2.shape[0]
    kdim = o2.shape[1]
    n_chunks = D_FF // FFN_CHUNK
    tiles_per_seq = seq // tm
    x_halo = tm // 8
    o_halo = tm // FFN_HALO
    const = lambda i: (0, 0)
    once = pl.Buffered(1)
    nxt = lambda blocks, per_tile: (lambda i: (jnp.minimum((i + 1) * per_tile, blocks - 1), 0))
    prv = lambda per_tile: (lambda i: (jnp.maximum(i * per_tile - 1, 0), 0))
    return pl.pallas_call(
        functools.partial(_ffn_kernel, tm=tm, tiles_per_seq=tiles_per_seq, n_chunks=n_chunks),
        out_shape=jax.ShapeDtypeStruct((m, D_MODEL), F32),
        grid=(m // tm,),
        in_specs=[pl.BlockSpec((tm, D_MODEL), lambda i: (i, 0)),
                  pl.BlockSpec((8, D_MODEL), nxt(m // 8, x_halo)),
                  pl.BlockSpec((8, D_MODEL), prv(x_halo)),
                  pl.BlockSpec((tm, kdim), lambda i: (i, 0)),
                  pl.BlockSpec((FFN_HALO, kdim), nxt(m // FFN_HALO, o_halo)),
                  pl.BlockSpec((FFN_HALO, kdim), prv(o_halo)),
                  pl.BlockSpec((kdim, D_MODEL), const, pipeline_mode=once),
                  pl.BlockSpec((1, D_MODEL), const),
                  pl.BlockSpec((D_MODEL, 2 * D_FF), const, pipeline_mode=once),
                  pl.BlockSpec((3, 2 * D_FF), const),
                  pl.BlockSpec((1, 2 * D_FF), const),
                  pl.BlockSpec((D_FF, D_MODEL), const, pipeline_mode=once)],
        out_specs=pl.BlockSpec((tm, D_MODEL), lambda i: (i, 0)),
        scratch_shapes=[pltpu.VMEM((tm + FFN_HALO, kdim), BF16),
                        pltpu.VMEM((tm + FFN_HALO, D_MODEL), BF16), pltpu.VMEM((tm, D_MODEL), F32),
                        pltpu.VMEM((2 * FFN_CHUNK // LANES, tm + FFN_HALO, LANES), F32),
                        pltpu.VMEM((2 * FFN_CHUNK // LANES, tm + FFN_HALO, LANES), F32)],
        compiler_params=_params(("arbitrary",)),
        name="proj_conv_ffn",
    )(x2, x2, x2, o2, o2, o2, w_o.astype(BF16), gain.reshape(1, D_MODEL), w_up.astype(BF16),
      conv_w.reshape(3, 2 * D_FF), conv_b.reshape(1, 2 * D_FF), w_down.astype(BF16))


def _dup_heads(w, n_heads, dim):
    w = w.reshape(D_MODEL, n_heads, 1, dim)
    return jnp.broadcast_to(w, (D_MODEL, n_heads, 2, dim)).reshape(D_MODEL, 2 * n_heads * dim)


def _mixer_a(x2, norm_gain, w_qkv, q_gain, k_gain, sink, bias, batch, seq):
    nq = A_HEADS * A_HEAD_DIM
    nk = A_KV_HEADS * A_HEAD_DIM
    w = jnp.concatenate([w_qkv[:, :nq], _dup_heads(w_qkv[:, nq:nq + nk], A_KV_HEADS, A_HEAD_DIM),
                         _dup_heads(w_qkv[:, nq + nk:], A_KV_HEADS, A_HEAD_DIM)], axis=1).astype(BF16)
    colscale = jnp.concatenate([jnp.tile(q_gain, A_HEADS) * (LOG2E * A_HEAD_DIM ** -0.5), jnp.tile(k_gain, 2 * A_KV_HEADS),
                                jnp.ones((2 * nk,), F32)])
    (qkv,) = _project(x2, norm_gain, w, colscale, [2 * D_MODEL], nq + 2 * nk, A_HEAD_DIM)
    return _attention_a(qkv, bias, sink, batch, seq).reshape(batch * seq, D_MODEL)


def _mixer_b(x2, norm_gain, w_qkv, q_gain, k_gain, lam_q1, lam_k1, lam_q2, lam_k2, sub_gain, bias,
             lambda_init, batch, seq):
    colscale = jnp.concatenate([jnp.tile(q_gain, 2 * B_HEADS) * (LOG2E * B_HEAD_DIM ** -0.5), jnp.tile(k_gain, 2 * B_HEADS),
                                jnp.ones((D_MODEL,), F32)])
    (qkv,) = _project(x2, norm_gain, w_qkv.astype(BF16), colscale, [3 * D_MODEL], 2 * D_MODEL, B_HEAD_DIM)
    lam_vecs = jnp.stack([lam_q1, lam_k1, lam_q2, lam_k2])
    return _attention_b(qkv, bias, lam_vecs, sub_gain, lambda_init, batch, seq).reshape(batch * seq, D_MODEL)


def _mixer_c(x2, norm_gain, w_qkv, q_gain, k_gain, bias, batch, seq):
    width = C_KV_HEADS * C_HEAD_DIM
    n_groups = len(C_BRANCHES)
    colscale = jnp.concatenate([jnp.tile(q_gain, n_groups * C_KV_HEADS) * (LOG2E * C_HEAD_DIM ** -0.5),
                                jnp.tile(k_gain, C_KV_HEADS), jnp.ones((width,), F32)])
    assert all(window // (2 * dil) == C_HALF for window, dil in C_BRANCHES)
    q, k, v = _project(x2, norm_gain, w_qkv.astype(BF16), colscale, [n_groups * width, width, width],
                       (n_groups + 1) * width, C_HEAD_DIM, slabs=True)
    return _attention_c(q, k, v, bias, batch, seq)


def _lambda_init(layer):
    return 0.8 - 0.6 * math.exp(-0.3 * layer)


def kernel(x, rel_bias, l0_attn_norm, l0_w_qkv, l0_q_gain, l0_k_gain, l0_sink, l0_w_o, l0_ffn_norm, l0_w_up, l0_conv_w, l0_conv_b, l0_w_down, l1_attn_norm, l1_w_qkv, l1_q_gain, l1_k_gain, l1_lambda_q1, l1_lambda_k1, l1_lambda_q2, l1_lambda_k2, l1_sub_gain, l1_w_o, l1_ffn_norm, l1_w_up, l1_conv_w, l1_conv_b, l1_w_down, l2_attn_norm, l2_w_qkv, l2_q_gain, l2_k_gain, l2_w_o, l2_ffn_norm, l2_w_up, l2_conv_w, l2_conv_b, l2_w_down, l3_attn_norm, l3_w_qkv, l3_q_gain, l3_k_gain, l3_sink, l3_w_o, l3_ffn_norm, l3_w_up, l3_conv_w, l3_conv_b, l3_w_down):
    batch, seq, _ = x.shape
    assert seq % (2 * LANES * C_BRANCHES[-1][1]) == 0
    bias_a = _bias_tiles(rel_bias, A_HEADS, [1], [LANES * d for d in range(-2, 3)], A_WINDOW)
    bias_b = _bias_tiles(rel_bias, 2 * B_HEADS, [1], [LANES * d for d in range(-B_FAR_TILES, B_FAR_TILES + 1)], None)
    bias_c = _bias_tiles(rel_bias, len(C_BRANCHES) * C_KV_HEADS, [dil for _, dil in C_BRANCHES],
                         [C_HALF * d for d in range(-2, 3)], C_HALF)
    bias_b = bias_b.reshape(B_HEADS, 2, 2 * B_FAR_TILES + 1, LANES, LANES)

    x2 = x.reshape(batch * seq, D_MODEL)
    o = _mixer_a(x2, l0_attn_norm, l0_w_qkv, l0_q_gain, l0_k_gain, l0_sink, bias_a, batch, seq)
    x2 = _proj_ffn(x2, o, l0_w_o, l0_ffn_norm, l0_w_up, l0_conv_w, l0_conv_b, l0_w_down, seq)
    o = _mixer_b(x2, l1_attn_norm, l1_w_qkv, l1_q_gain, l1_k_gain, l1_lambda_q1, l1_lambda_k1, l1_lambda_q2,
                 l1_lambda_k2, l1_sub_gain, bias_b, _lambda_init(1), batch, seq)
    x2 = _proj_ffn(x2, o, l1_w_o, l1_ffn_norm, l1_w_up, l1_conv_w, l1_conv_b, l1_w_down, seq)
    o = _mixer_c(x2, l2_attn_norm, l2_w_qkv, l2_q_gain, l2_k_gain, bias_c, batch, seq)
    x2 = _proj_ffn(x2, o, l2_w_o, l2_ffn_norm, l2_w_up, l2_conv_w, l2_conv_b, l2_w_down, seq)
    o = _mixer_a(x2, l3_attn_norm, l3_w_qkv, l3_q_gain, l3_k_gain, l3_sink, bias_a, batch, seq)
    x2 = _proj_ffn(x2, o, l3_w_o, l3_ffn_norm, l3_w_up, l3_conv_w, l3_conv_b, l3_w_down, seq)
    return x2.reshape(batch, seq, D_MODEL)
```

```python
import functools
import math

import jax
import jax.numpy as jnp
from jax import lax
from jax.experimental import pallas as pl
from jax.experimental.pallas import tpu as pltpu

F32 = jnp.float32
BF16 = jnp.bfloat16

D_MODEL = 1024
EPS = 1e-6
NEG_INF = -1e30
LOG2E = math.log2(math.e)
LANES = 128
MXU_TILE = 256
VMEM_LIMIT = 56 * 1024 * 1024

NUM_BUCKETS = 32
MAX_DISTANCE = 1024
N_BIAS_HEADS = 16

A_HEADS, A_KV_HEADS, A_HEAD_DIM, A_WINDOW = 16, 4, 64, 128
B_HEADS, B_HEAD_DIM = 8, 64
B_FAR_TILES = 6
C_BRANCHES = ((128, 1), (512, 4), (2048, 16))
C_KV_HEADS, C_HEAD_DIM = 4, 128
C_HALF = 64
D_FF = 2816
FFN_CHUNK = 256
FFN_HALO = 16


def _params(sem, vmem=VMEM_LIMIT):
    return pltpu.CompilerParams(dimension_semantics=sem, vmem_limit_bytes=vmem)


def _rms(x, gain):
    return x * lax.rsqrt(jnp.mean(x * x, axis=-1, keepdims=True) + EPS) * gain


def _bias_tile_kernel(tab_ref, dil_ref, off_ref, out_ref, *, half, heads_per_group):
    grp = pl.program_id(0)
    t = pl.program_id(1)
    dil = dil_ref[grp]
    off = off_ref[t]
    q = lax.broadcasted_iota(jnp.int32, (LANES, LANES), 0)
    c = lax.broadcasted_iota(jnp.int32, (LANES, LANES), 1)
    x = c - q + off
    rel = x * dil
    nb = NUM_BUCKETS // 2
    max_exact = nb // 2
    n = jnp.abs(rel)
    nf = jnp.maximum(n, 1).astype(F32)
    large = max_exact + (jnp.log(nf * (1.0 / max_exact)) / math.log(MAX_DISTANCE / max_exact)
                         * (nb - max_exact)).astype(jnp.int32)
    large = jnp.minimum(large, nb - 1)
    bucket = jnp.where(rel > 0, nb, 0) + jnp.where(n < max_exact, n, large)
    masks = [(bucket & (1 << bit)) != 0 for bit in range(5)]
    inside = None if half is None else jnp.abs(x) <= half
    for hh in range(heads_per_group):
        col = grp * heads_per_group + hh
        level = [tab_ref[b, col] for b in range(NUM_BUCKETS)]
        for mask in masks:
            level = [jnp.where(mask, level[2 * k + 1], level[2 * k]) for k in range(len(level) // 2)]
        bias = level[0] * LOG2E
        if inside is not None:
            bias = jnp.where(inside, bias, NEG_INF)
        out_ref[hh] = bias


def _bias_tiles(rel_bias, n_heads, group_dils, offs, half):
    n_groups, nt = len(group_dils), len(offs)
    hpg = n_heads // n_groups
    smem = pl.BlockSpec(memory_space=pltpu.SMEM)
    return pl.pallas_call(
        functools.partial(_bias_tile_kernel, half=half, heads_per_group=hpg),
        out_shape=jax.ShapeDtypeStruct((n_heads, nt, LANES, LANES), F32),
        grid=(n_groups, nt),
        in_specs=[smem, smem, smem],
        out_specs=pl.BlockSpec((hpg, None, LANES, LANES), lambda g, t: (g, t, 0, 0)),
        compiler_params=_params(("arbitrary", "arbitrary")),
        name="bias_tiles",
    )(rel_bias, jnp.asarray(group_dils, jnp.int32), jnp.asarray(offs, jnp.int32))


def _proj_kernel(x_ref, g_ref, w_ref, cs_ref, p_ref, *out_refs, widths, n_norm_cols, group, slabs):
    h = _rms(x_ref[...], g_ref[...]).astype(BF16)
    group_w = 2 * MXU_TILE
    tiles = [(o_ref, off) for o_ref, width in zip(out_refs, widths) for off in range(0, width, group_w)]

    def project(t):
        return jnp.dot(h, w_ref[:, t * group_w:(t + 1) * group_w], preferred_element_type=F32)

    def finish(t, y):
        o_ref, off = tiles[t]
        for s in range(group_w // MXU_TILE):
            col = t * group_w + s * MXU_TILE
            ys = y[:, s * MXU_TILE:(s + 1) * MXU_TILE]
            if col < n_norm_cols:
                ss = jnp.dot((ys * ys).astype(BF16), p_ref[...], preferred_element_type=F32)
                ys = ys * lax.rsqrt(ss * (1.0 / group) + EPS) * cs_ref[:, col:col + MXU_TILE]
            lo = off + s * MXU_TILE
            if slabs:
                for half in range(MXU_TILE // LANES):
                    o_ref[lo // LANES + half] = ys[:, half * LANES:(half + 1) * LANES]
            else:
                o_ref[:, lo:lo + MXU_TILE] = ys.astype(BF16)

    y = project(0)
    for t in range(len(tiles)):
        y_next = project(t + 1) if t + 1 < len(tiles) else None
        finish(t, y)
        y = y_next


def _project(x2, gain, w, colscale, widths, n_norm_cols, group, slabs=False, tm=512):
    m = x2.shape[0]
    n = w.shape[1]
    assert sum(widths) == n and m % tm == 0 and n_norm_cols % MXU_TILE == 0
    assert all(wd % (2 * MXU_TILE) == 0 for wd in widths)
    r = jnp.arange(MXU_TILE) // group
    ones_bd = (r[:, None] == r[None, :]).astype(BF16)
    const = lambda i: (0, 0)
    if slabs:
        out_shape = [jax.ShapeDtypeStruct((wd // LANES, m, LANES), F32) for wd in widths]
        out_specs = [pl.BlockSpec((wd // LANES, tm, LANES), lambda i: (0, i, 0)) for wd in widths]
    else:
        out_shape = [jax.ShapeDtypeStruct((m, wd), BF16) for wd in widths]
        out_specs = [pl.BlockSpec((tm, wd), lambda i: (i, 0)) for wd in widths]
    return pl.pallas_call(
        functools.partial(_proj_kernel, widths=tuple(widths), n_norm_cols=n_norm_cols, group=group, slabs=slabs),
        out_shape=out_shape,
        grid=(m // tm,),
        in_specs=[pl.BlockSpec((tm, D_MODEL), lambda i: (i, 0)),
                  pl.BlockSpec((1, D_MODEL), const),
                  pl.BlockSpec((D_MODEL, n), const),
                  pl.BlockSpec((1, n), const),
                  pl.BlockSpec((MXU_TILE, MXU_TILE), const)],
        out_specs=out_specs,
        compiler_params=_params(("arbitrary",)),
        name="qkv_proj",
    )(x2, gain.reshape(1, D_MODEL), w, colscale.reshape(1, n), ones_bd)


def _attn_a_kernel(sink_ref, q_ref, k_ref, v_ref, bias_ref, o_ref, *, nq, n_tiles):
    i = pl.program_id(1)
    lane = lax.broadcasted_iota(jnp.int32, (LANES, LANES), 1)
    lo = lane < A_HEAD_DIM
    group = A_HEADS // A_KV_HEADS

    def window(s):
        t = i * nq + s
        base = jnp.clip(t - 1, 0, n_tiles - 3)
        return t, base, pl.ds(pl.multiple_of(base * LANES, LANES), 3 * LANES)

    def scores(s, hk):
        rows = slice(s * LANES, (s + 1) * LANES)
        parts = []
        for c in (2 * hk, 2 * hk + 1):
            qc = q_ref[rows, c * LANES:(c + 1) * LANES]
            parts += [jnp.where(lo, qc, jnp.zeros_like(qc)), jnp.where(lo, jnp.zeros_like(qc), qc)]
        qs = jnp.concatenate(parts, axis=0)
        kw = k_ref[window(s)[2], hk * LANES:(hk + 1) * LANES]
        return lax.dot_general(qs, kw, (((1,), (1,)), ((), ())), preferred_element_type=F32)

    def finish(s, hk, sc):
        t, base, krows = window(s)
        rows = slice(s * LANES, (s + 1) * LANES)
        blocks = []
        for g in range(group):
            hq = hk * group + g
            blocks.append(jnp.concatenate(
                [sc[g * LANES:(g + 1) * LANES, w * LANES:(w + 1) * LANES] + bias_ref[hq, base + w - t + 2]
                 for w in range(3)], axis=1))
        sc = jnp.concatenate(blocks, axis=0)
        mx = jnp.max(sc, axis=-1, keepdims=True)
        p = jnp.exp2(sc - mx)
        den = jnp.sum(p, axis=-1, keepdims=True)
        pv = jnp.dot(p.astype(BF16), v_ref[krows, hk * LANES:(hk + 1) * LANES],
                     preferred_element_type=F32)
        outs = []
        for g in range(group):
            hq = hk * group + g
            gr = slice(g * LANES, (g + 1) * LANES)
            outs.append(pv[gr] / (den[gr] + jnp.exp2(sink_ref[hq] * LOG2E - mx[gr])))
        o_ref[rows, (2 * hk) * LANES:(2 * hk + 1) * LANES] = jnp.where(lo, outs[0], outs[1]).astype(BF16)
        o_ref[rows, (2 * hk + 1) * LANES:(2 * hk + 2) * LANES] = jnp.where(lo, outs[2], outs[3]).astype(BF16)

    blocks_todo = [(s, hk) for s in range(nq) for hk in range(A_KV_HEADS)]
    sc = scores(*blocks_todo[0])
    for n, blk in enumerate(blocks_todo):
        sc_next = scores(*blocks_todo[n + 1]) if n + 1 < len(blocks_todo) else None
        finish(*blk, sc)
        sc = sc_next


def _attention_a(qkv, bias, sink, batch, seq, tq=256):
    nq = tq // LANES
    n_tiles = seq // LANES
    qkv3 = qkv.reshape(batch, seq, 2 * D_MODEL)
    return pl.pallas_call(
        functools.partial(_attn_a_kernel, nq=nq, n_tiles=n_tiles),
        out_shape=jax.ShapeDtypeStruct((batch, seq, D_MODEL), BF16),
        grid=(batch, seq // tq),
        in_specs=[pl.BlockSpec(memory_space=pltpu.SMEM),
                  pl.BlockSpec((None, tq, D_MODEL), lambda b, i: (b, i, 0)),
                  pl.BlockSpec((None, seq, 512), lambda b, i: (b, 0, 2)),
                  pl.BlockSpec((None, seq, 512), lambda b, i: (b, 0, 3)),
                  pl.BlockSpec((A_HEADS, 5, LANES, LANES), lambda b, i: (0, 0, 0, 0))],
        out_specs=pl.BlockSpec((None, tq, D_MODEL), lambda b, i: (b, i, 0)),
        compiler_params=_params(("arbitrary", "arbitrary")),
        name="attn_window",
    )(sink, qkv3, qkv3, qkv3, bias)


def _attn_b_kernel(lam_ref, sg_ref, q_ref, k_ref, v_ref, bias_ref, o_ref, qs_ref, m_ref, l_ref, acc_ref,
                   *, tq, tk, n_kc, lambda_init):
    i = pl.program_id(2)
    nqt = tq // LANES
    nkt = tk // LANES
    lo = lax.broadcasted_iota(jnp.int32, (tq, LANES), 1) < B_HEAD_DIM
    q = q_ref[...]
    zero = jnp.zeros_like(q)
    qs_ref[0:tq, :] = jnp.where(lo, q, zero)
    qs_ref[tq:2 * tq, :] = jnp.where(lo, zero, q)
    m_ref[...] = jnp.full(m_ref.shape, -jnp.inf, F32)
    l_ref[...] = jnp.zeros(l_ref.shape, F32)
    acc_ref[...] = jnp.zeros(acc_ref.shape, F32)

    def scores(kc):
        return lax.dot_general(qs_ref[...], k_ref[kc * tk:(kc + 1) * tk, :], (((1,), (1,)), ((), ())),
                               preferred_element_type=F32)

    def softmax_pv(kc, sc):
        probs = []
        for rb in range(2 * nqt):
            mp, a = divmod(rb, nqt)
            rows = slice(rb * LANES, (rb + 1) * LANES)
            tiles = []
            for w in range(nkt):
                d = jnp.clip(kc * nkt + w - (i * nqt + a), -B_FAR_TILES, B_FAR_TILES) + B_FAR_TILES
                tiles.append(sc[rows, w * LANES:(w + 1) * LANES] + bias_ref[mp, d])
            m_old = m_ref[rows, :]
            m_new = jnp.maximum(m_old, jnp.max(functools.reduce(jnp.maximum, tiles), axis=-1, keepdims=True))
            alpha = jnp.exp2(m_old - m_new)
            p = [jnp.exp2(t - m_new) for t in tiles]
            l_ref[rows, :] = alpha * l_ref[rows, :] + functools.reduce(jnp.add, p)
            acc_ref[rows, :] = alpha * acc_ref[rows, :]
            m_ref[rows, :] = m_new
            probs.append(jnp.concatenate(p, axis=1).astype(BF16))
        acc_ref[...] += jnp.dot(jnp.concatenate(probs, axis=0), v_ref[kc * tk:(kc + 1) * tk, :],
                                preferred_element_type=F32)

    for kc in range(n_kc):
        softmax_pv(kc, scores(kc))
    lv = lam_ref[...]
    lam = (jnp.exp(jnp.sum(lv[0:1] * lv[1:2], axis=-1, keepdims=True))
           - jnp.exp(jnp.sum(lv[2:3] * lv[3:4], axis=-1, keepdims=True)) + lambda_init)
    o = acc_ref[...] / jnp.sum(l_ref[...], axis=-1, keepdims=True)
    o = o[:tq] - lam * o[tq:]
    o_ref[...] = (_rms(o, sg_ref[...]) * (1.0 - lambda_init)).astype(BF16)


def _attention_b(qkv, bias, lam_vecs, sub_gain, lambda_init, batch, seq, tq=512, tk=512):
    qkv3 = qkv.reshape(batch, seq, 3 * D_MODEL)
    nb = 2 * B_FAR_TILES + 1
    return pl.pallas_call(
        functools.partial(_attn_b_kernel, tq=tq, tk=tk, n_kc=seq // tk, lambda_init=lambda_init),
        out_shape=jax.ShapeDtypeStruct((batch, seq, D_MODEL), BF16),
        grid=(batch, B_HEADS, seq // tq),
        in_specs=[pl.BlockSpec((4, B_HEAD_DIM), lambda b, h, i: (0, 0)),
                  pl.BlockSpec((1, LANES), lambda b, h, i: (0, 0)),
                  pl.BlockSpec((None, tq, LANES), lambda b, h, i: (b, i, h)),
                  pl.BlockSpec((None, seq, LANES), lambda b, h, i: (b, 0, B_HEADS + h)),
                  pl.BlockSpec((None, seq, LANES), lambda b, h, i: (b, 0, 2 * B_HEADS + h)),
                  pl.BlockSpec((None, 2, nb, LANES, LANES), lambda b, h, i: (h, 0, 0, 0, 0))],
        out_specs=pl.BlockSpec((None, tq, LANES), lambda b, h, i: (b, i, h)),
        scratch_shapes=[pltpu.VMEM((2 * tq, LANES), BF16), pltpu.VMEM((2 * tq, LANES), F32),
                        pltpu.VMEM((2 * tq, LANES), F32), pltpu.VMEM((2 * tq, LANES), F32)],
        compiler_params=_params(("arbitrary", "arbitrary", "arbitrary")),
        name="attn_diff",
    )(lam_vecs, sub_gain.reshape(1, LANES), qkv3, qkv3, qkv3, bias)


def _attn_c_kernel(q_ref, k_ref, v_ref, bias_ref, o_ref, og_ref, lg_ref, *, seq, npos, merge_rows):
    c = pl.program_id(2)

    def rows_of(start, n, dil):
        return pl.ds(start, n) if dil == 1 else pl.ds(start, n, stride=dil)

    def window(g, r, tt):
        dil = C_BRANCHES[g][1]
        t0 = c * (npos // dil) + tt * LANES
        ws = jnp.clip(t0 - C_HALF, 0, seq // dil - 2 * LANES)
        return t0, ws, rows_of(r + dil * LANES * tt, LANES, dil), rows_of(r + dil * ws, 2 * LANES, dil)

    def scores(g, r, tt):
        _, _, qrows, krows = window(g, r, tt)
        return lax.dot_general(q_ref[g, qrows, :].astype(BF16), k_ref[krows, :].astype(BF16),
                               (((1,), (1,)), ((), ())), preferred_element_type=F32)

    def finish(g, r, tt, sc):
        t0, ws, qrows, krows = window(g, r, tt)
        sc = jnp.concatenate(
            [sc[:, w * LANES:(w + 1) * LANES] + bias_ref[g, (ws + w * LANES - t0) // C_HALF + 2]
             for w in range(2)], axis=1)
        mx = jnp.max(sc, axis=-1, keepdims=True)
        p = jnp.exp2(sc - mx)
        den = jnp.sum(p, axis=-1, keepdims=True)
        og_ref[g, qrows, :] = jnp.dot(p.astype(BF16), v_ref[krows, :].astype(BF16), preferred_element_type=F32) / den
        lg_ref[g, qrows, :] = jnp.broadcast_to(mx + jnp.log2(den), (LANES, LANES))

    blocks_todo = [(g, r, tt) for g, (_, dil) in enumerate(C_BRANCHES) for r in range(dil)
                   for tt in range(npos // (dil * LANES))]
    sc = scores(*blocks_todo[0])
    for n, blk in enumerate(blocks_todo):
        sc_next = scores(*blocks_todo[n + 1]) if n + 1 < len(blocks_todo) else None
        finish(*blk, sc)
        sc = sc_next
    for blk in range(npos // merge_rows):
        rows = slice(blk * merge_rows, (blk + 1) * merge_rows)
        lses = [lg_ref[g, rows, :] for g in range(len(C_BRANCHES))]
        mx = functools.reduce(jnp.maximum, lses)
        es = [jnp.exp2(l - mx) for l in lses]
        num = functools.reduce(jnp.add, [e * og_ref[g, rows, :] for g, e in enumerate(es)])
        o_ref[rows, :] = (num / functools.reduce(jnp.add, es)).astype(BF16)


def _attention_c(q, k, v, bias, batch, seq, npos=2048, merge_rows=256):
    n_groups = len(C_BRANCHES)
    m = batch * seq
    chunks = seq // npos
    q4 = q.reshape(n_groups, C_KV_HEADS, m, LANES)
    bias5 = bias.reshape(n_groups, C_KV_HEADS, 5, LANES, LANES)
    return pl.pallas_call(
        functools.partial(_attn_c_kernel, seq=seq, npos=npos, merge_rows=merge_rows),
        out_shape=jax.ShapeDtypeStruct((m, C_KV_HEADS * C_HEAD_DIM), BF16),
        grid=(batch, C_KV_HEADS, chunks),
        in_specs=[pl.BlockSpec((n_groups, None, npos, LANES), lambda b, j, c: (0, j, b * chunks + c, 0)),
                  pl.BlockSpec((None, seq, LANES), lambda b, j, c: (j, b, 0)),
                  pl.BlockSpec((None, seq, LANES), lambda b, j, c: (j, b, 0)),
                  pl.BlockSpec((n_groups, None, 5, LANES, LANES), lambda b, j, c: (0, j, 0, 0, 0))],
        out_specs=pl.BlockSpec((npos, LANES), lambda b, j, c: (b * chunks + c, j)),
        scratch_shapes=[pltpu.VMEM((n_groups, npos, LANES), F32), pltpu.VMEM((n_groups, npos, LANES), F32)],
        compiler_params=_params(("arbitrary", "arbitrary", "arbitrary")),
        name="attn_dilated",
    )(q4, k, v, bias5)


def _ffn_kernel(x_ref, xn_ref, xp_ref, o_ref, on_ref, op_ref, wo_ref, g_ref, wup_ref, cw_ref, cb_ref, wdn_ref,
                y_ref, oe_ref, h_ref, acc_ref, ua_ref, ub_ref, *, tm, tiles_per_seq, n_chunks):
    i = pl.program_id(0)
    pos = i % tiles_per_seq
    kdim = o_ref.shape[1]
    halo_row = lax.broadcasted_iota(jnp.int32, (FFN_HALO, kdim), 0)
    oe_ref[0:tm, :] = o_ref[...]
    oe_ref[tm:tm + FFN_HALO, :] = jnp.where(halo_row < 8, on_ref[...], op_ref[...])
    attn = jnp.dot(oe_ref[...], wo_ref[...], preferred_element_type=F32)
    x1 = x_ref[...] + attn[0:tm]
    x1n = xn_ref[...] + attn[tm:tm + 8]
    x1p = xp_ref[...] + attn[tm + 8:tm + FFN_HALO]
    g = g_ref[...]
    h_ref[0:tm, :] = _rms(x1, g).astype(BF16)
    hn = jnp.where(pos == tiles_per_seq - 1, 0.0, _rms(x1n, g))
    hp = jnp.where(pos == 0, 0.0, _rms(x1p, g))
    h_ref[tm:tm + FFN_HALO, :] = jnp.concatenate([hn, hp], axis=0).astype(BF16)
    acc_ref[...] = x1
    slabs = FFN_CHUNK // LANES

    def chunk_cols(k):
        return pl.ds(pl.multiple_of(k * FFN_CHUNK, FFN_CHUNK), FFN_CHUNK)

    def up_proj(j, u_ref):
        h = h_ref[...]
        for half, k in enumerate((j, j + n_chunks)):
            u = jnp.dot(h, wup_ref[:, chunk_cols(k)], preferred_element_type=F32)
            for s in range(slabs):
                cols = slice(s * LANES, (s + 1) * LANES)
                slab = half * slabs + s
                u_ref[slab, 8:8 + tm, :] = u[0:tm, cols]
                u_ref[slab, 0:8, :] = u[tm + 8:tm + 16, cols]
                u_ref[slab, tm + 8:tm + 16, :] = u[tm:tm + 8, cols]

    def conv(u_ref, slab, k, cols):
        w = cw_ref[:, chunk_cols(k)]
        return (w[0:1, cols] * u_ref[slab, pl.ds(7, tm), :] + w[1:2, cols] * u_ref[slab, pl.ds(8, tm), :]
                + w[2:3, cols] * u_ref[slab, pl.ds(9, tm), :] + cb_ref[:, chunk_cols(k)][:, cols])

    def gate_down(j, u_ref):
        acts = []
        for s in range(slabs):
            cols = slice(s * LANES, (s + 1) * LANES)
            gate = conv(u_ref, s, j, cols)
            val = conv(u_ref, slabs + s, j + n_chunks, cols)
            acts.append((gate / (1.0 + jnp.exp(-gate)) * val).astype(BF16))
        acc_ref[...] += jnp.dot(jnp.concatenate(acts, axis=1), wdn_ref[chunk_cols(j), :],
                                preferred_element_type=F32)

    up_proj(0, ua_ref)

    def pair(c, carry):
        j = 2 * c
        up_proj(j + 1, ub_ref)
        gate_down(j, ua_ref)
        up_proj(j + 2, ua_ref)
        gate_down(j + 1, ub_ref)
        return carry

    assert n_chunks % 2 == 1
    lax.fori_loop(0, n_chunks // 2, pair, 0, unroll=True)
    gate_down(n_chunks - 1, ua_ref)
    y_ref[...] = acc_ref[...]


def _proj_ffn(x2, o2, w_o, gain, w_up, conv_w, conv_b, w_down, seq, tm=512):
    m = x2.shape[0]
    kdim = o2.shape[1]
    n_chunks = D_FF // FFN_CHUNK
    tiles_per_seq = seq // tm
    x_halo = tm // 8
    o_halo = tm // FFN_HALO
    const = lambda i: (0, 0)
    once = pl.Buffered(1)
    nxt = lambda blocks, per_tile: (lambda i: (jnp.minimum((i + 1) * per_tile, blocks - 1), 0))
    prv = lambda per_tile: (lambda i: (jnp.maximum(i * per_tile - 1, 0), 0))
    return pl.pallas_call(
        functools.partial(_ffn_kernel, tm=tm, tiles_per_seq=tiles_per_seq, n_chunks=n_chunks),
        out_shape=jax.ShapeDtypeStruct((m, D_MODEL), F32),
        grid=(m // tm,),
        in_specs=[pl.BlockSpec((tm, D_MODEL), lambda i: (i, 0)),
                  pl.BlockSpec((8, D_MODEL), nxt(m // 8, x_halo)),
                  pl.BlockSpec((8, D_MODEL), prv(x_halo)),
                  pl.BlockSpec((tm, kdim), lambda i: (i, 0)),
                  pl.BlockSpec((FFN_HALO, kdim), nxt(m // FFN_HALO, o_halo)),
                  pl.BlockSpec((FFN_HALO, kdim), prv(o_halo)),
                  pl.BlockSpec((kdim, D_MODEL), const, pipeline_mode=once),
                  pl.BlockSpec((1, D_MODEL), const),
                  pl.BlockSpec((D_MODEL, 2 * D_FF), const, pipeline_mode=once),
                  pl.BlockSpec((3, 2 * D_FF), const),
                  pl.BlockSpec((1, 2 * D_FF), const),
                  pl.BlockSpec((D_FF, D_MODEL), const, pipeline_mode=once)],
        out_specs=pl.BlockSpec((tm, D_MODEL), lambda i: (i, 0)),
        scratch_shapes=[pltpu.VMEM((tm + FFN_HALO, kdim), BF16),
                        pltpu.VMEM((tm + FFN_HALO, D_MODEL), BF16), pltpu.VMEM((tm, D_MODEL), F32),
                        pltpu.VMEM((2 * FFN_CHUNK // LANES, tm + FFN_HALO, LANES), F32),
                        pltpu.VMEM((2 * FFN_CHUNK // LANES, tm + FFN_HALO, LANES), F32)],
        compiler_params=_params(("arbitrary",)),
        name="proj_conv_ffn",
    )(x2, x2, x2, o2, o2, o2, w_o.astype(BF16), gain.reshape(1, D_MODEL), w_up.astype(BF16),
      conv_w.reshape(3, 2 * D_FF), conv_b.reshape(1, 2 * D_FF), w_down.astype(BF16))


def _dup_heads(w, n_heads, dim):
    w = w.reshape(D_MODEL, n_heads, 1, dim)
    return jnp.broadcast_to(w, (D_MODEL, n_heads, 2, dim)).reshape(D_MODEL, 2 * n_heads * dim)


def _mixer_a(x2, norm_gain, w_qkv, q_gain, k_gain, sink, bias, batch, seq):
    nq = A_HEADS * A_HEAD_DIM
    nk = A_KV_HEADS * A_HEAD_DIM
    w = jnp.concatenate([w_qkv[:, :nq], _dup_heads(w_qkv[:, nq:nq + nk], A_KV_HEADS, A_HEAD_DIM),
                         _dup_heads(w_qkv[:, nq + nk:], A_KV_HEADS, A_HEAD_DIM)], axis=1).astype(BF16)
    colscale = jnp.concatenate([jnp.tile(q_gain, A_HEADS) * (LOG2E * A_HEAD_DIM ** -0.5), jnp.tile(k_gain, 2 * A_KV_HEADS),
                                jnp.ones((2 * nk,), F32)])
    (qkv,) = _project(x2, norm_gain, w, colscale, [2 * D_MODEL], nq + 2 * nk, A_HEAD_DIM)
    return _attention_a(qkv, bias, sink, batch, seq).reshape(batch * seq, D_MODEL)


def _mixer_b(x2, norm_gain, w_qkv, q_gain, k_gain, lam_q1, lam_k1, lam_q2, lam_k2, sub_gain, bias,
             lambda_init, batch, seq):
    colscale = jnp.concatenate([jnp.tile(q_gain, 2 * B_HEADS) * (LOG2E * B_HEAD_DIM ** -0.5), jnp.tile(k_gain, 2 * B_HEADS),
                                jnp.ones((D_MODEL,), F32)])
    (qkv,) = _project(x2, norm_gain, w_qkv.astype(BF16), colscale, [3 * D_MODEL], 2 * D_MODEL, B_HEAD_DIM)
    lam_vecs = jnp.stack([lam_q1, lam_k1, lam_q2, lam_k2])
    return _attention_b(qkv, bias, lam_vecs, sub_gain, lambda_init, batch, seq).reshape(batch * seq, D_MODEL)


def _mixer_c(x2, norm_gain, w_qkv, q_gain, k_gain, bias, batch, seq):
    width = C_KV_HEADS * C_HEAD_DIM
    n_groups = len(C_BRANCHES)
    colscale = jnp.concatenate([jnp.tile(q_gain, n_groups * C_KV_HEADS) * (LOG2E * C_HEAD_DIM ** -0.5),
                                jnp.tile(k_gain, C_KV_HEADS), jnp.ones((width,), F32)])
    assert all(window // (2 * dil) == C_HALF for window, dil in C_BRANCHES)
    q, k, v = _project(x2, norm_gain, w_qkv.astype(BF16), colscale, [n_groups * width, width, width],
                       (n_groups + 1) * width, C_HEAD_DIM, slabs=True)
    return _attention_c(q, k, v, bias, batch, seq)


def _lambda_init(layer):
    return 0.8 - 0.6 * math.exp(-0.3 * layer)


def kernel(x, rel_bias, l0_attn_norm, l0_w_qkv, l0_q_gain, l0_k_gain, l0_sink, l0_w_o, l0_ffn_norm, l0_w_up, l0_conv_w, l0_conv_b, l0_w_down, l1_attn_norm, l1_w_qkv, l1_q_gain, l1_k_gain, l1_lambda_q1, l1_lambda_k1, l1_lambda_q2, l1_lambda_k2, l1_sub_gain, l1_w_o, l1_ffn_norm, l1_w_up, l1_conv_w, l1_conv_b, l1_w_down, l2_attn_norm, l2_w_qkv, l2_q_gain, l2_k_gain, l2_w_o, l2_ffn_norm, l2_w_up, l2_conv_w, l2_conv_b, l2_w_down, l3_attn_norm, l3_w_qkv, l3_q_gain, l3_k_gain, l3_sink, l3_w_o, l3_ffn_norm, l3_w_up, l3_conv_w, l3_conv_b, l3_w_down):
    batch, seq, _ = x.shape
    assert seq % (2 * LANES * C_BRANCHES[-1][1]) == 0
    bias_a = _bias_tiles(rel_bias, A_HEADS, [1], [LANES * d for d in range(-2, 3)], A_WINDOW)
    bias_b = _bias_tiles(rel_bias, 2 * B_HEADS, [1], [LANES * d for d in range(-B_FAR_TILES, B_FAR_TILES + 1)], None)
    bias_c = _bias_tiles(rel_bias, len(C_BRANCHES) * C_KV_HEADS, [dil for _, dil in C_BRANCHES],
                         [C_HALF * d for d in range(-2, 3)], C_HALF)
    bias_b = bias_b.reshape(B_HEADS, 2, 2 * B_FAR_TILES + 1, LANES, LANES)

    x2 = x.reshape(batch * seq, D_MODEL)
    o = _mixer_a(x2, l0_attn_norm, l0_w_qkv, l0_q_gain, l0_k_gain, l0_sink, bias_a, batch, seq)
    x2 = _proj_ffn(x2, o, l0_w_o, l0_ffn_norm, l0_w_up, l0_conv_w, l0_conv_b, l0_w_down, seq)
    o = _mixer_b(x2, l1_attn_norm, l1_w_qkv, l1_q_gain, l1_k_gain, l1_lambda_q1, l1_lambda_k1, l1_lambda_q2,
                 l1_lambda_k2, l1_sub_gain, bias_b, _lambda_init(1), batch, seq)
    x2 = _proj_ffn(x2, o, l1_w_o, l1_ffn_norm, l1_w_up, l1_conv_w, l1_conv_b, l1_w_down, seq)
    o = _mixer_c(x2, l2_attn_norm, l2_w_qkv, l2_q_gain, l2_k_gain, bias_c, batch, seq)
    x2 = _proj_ffn(x2, o, l2_w_o, l2_ffn_norm, l2_w_up, l2_conv_w, l2_conv_b, l2_w_down, seq)
    o = _mixer_a(x2, l3_attn_norm, l3_w_qkv, l3_q_gain, l3_k_gain, l3_sink, bias_a, batch, seq)
    x2 = _proj_ffn(x2, o, l3_w_o, l3_ffn_norm, l3_w_up, l3_conv_w, l3_conv_b, l3_w_down, seq)
    return x2.reshape(batch, seq, D_MODEL)
```

```python
import functools
import math

import jax
import jax.numpy as jnp
from jax import lax
from jax.experimental import pallas as pl
from jax.experimental.pallas import tpu as pltpu

F32 = jnp.float32
BF16 = jnp.bfloat16

D_MODEL = 1024
EPS = 1e-6
NEG_INF = -1e30
LOG2E = math.log2(math.e)
LANES = 128
MXU_TILE = 256
VMEM_LIMIT = 56 * 1024 * 1024

NUM_BUCKETS = 32
MAX_DISTANCE = 1024
N_BIAS_HEADS = 16

A_HEADS, A_KV_HEADS, A_HEAD_DIM, A_WINDOW = 16, 4, 64, 128
B_HEADS, B_HEAD_DIM = 8, 64
B_FAR_TILES = 6
C_BRANCHES = ((128, 1), (512, 4), (2048, 16))
C_KV_HEADS, C_HEAD_DIM = 4, 128
C_HALF = 64
D_FF = 2816
FFN_CHUNK = 256
FFN_HALO = 16


def _params(sem, vmem=VMEM_LIMIT):
    return pltpu.CompilerParams(dimension_semantics=sem, vmem_limit_bytes=vmem)


def _rms(x, gain):
    return x * lax.rsqrt(jnp.mean(x * x, axis=-1, keepdims=True) + EPS) * gain


def _bias_tile_kernel(tab_ref, dil_ref, off_ref, out_ref, *, half, heads_per_group):
    grp = pl.program_id(0)
    t = pl.program_id(1)
    dil = dil_ref[grp]
    off = off_ref[t]
    q = lax.broadcasted_iota(jnp.int32, (LANES, LANES), 0)
    c = lax.broadcasted_iota(jnp.int32, (LANES, LANES), 1)
    x = c - q + off
    rel = x * dil
    nb = NUM_BUCKETS // 2
    max_exact = nb // 2
    n = jnp.abs(rel)
    nf = jnp.maximum(n, 1).astype(F32)
    large = max_exact + (jnp.log(nf * (1.0 / max_exact)) / math.log(MAX_DISTANCE / max_exact)
                         * (nb - max_exact)).astype(jnp.int32)
    large = jnp.minimum(large, nb - 1)
    bucket = jnp.where(rel > 0, nb, 0) + jnp.where(n < max_exact, n, large)
    masks = [(bucket & (1 << bit)) != 0 for bit in range(5)]
    inside = None if half is None else jnp.abs(x) <= half
    for hh in range(heads_per_group):
        col = grp * heads_per_group + hh
        level = [tab_ref[b, col] for b in range(NUM_BUCKETS)]
        for mask in masks:
            level = [jnp.where(mask, level[2 * k + 1], level[2 * k]) for k in range(len(level) // 2)]
        bias = level[0] * LOG2E
        if inside is not None:
            bias = jnp.where(inside, bias, NEG_INF)
        out_ref[hh] = bias


def _bias_tiles(rel_bias, n_heads, group_dils, offs, half):
    n_groups, nt = len(group_dils), len(offs)
    hpg = n_heads // n_groups
    smem = pl.BlockSpec(memory_space=pltpu.SMEM)
    return pl.pallas_call(
        functools.partial(_bias_tile_kernel, half=half, heads_per_group=hpg),
        out_shape=jax.ShapeDtypeStruct((n_heads, nt, LANES, LANES), F32),
        grid=(n_groups, nt),
        in_specs=[smem, smem, smem],
        out_specs=pl.BlockSpec((hpg, None, LANES, LANES), lambda g, t: (g, t, 0, 0)),
        compiler_params=_params(("arbitrary", "arbitrary")),
        name="bias_tiles",
    )(rel_bias, jnp.asarray(group_dils, jnp.int32), jnp.asarray(offs, jnp.int32))


def _proj_kernel(x_ref, g_ref, w_ref, cs_ref, p_ref, *out_refs, widths, n_norm_cols, group, slabs):
    h = _rms(x_ref[...], g_ref[...]).astype(BF16)
    group_w = 2 * MXU_TILE
    tiles = [(o_ref, off) for o_ref, width in zip(out_refs, widths) for off in range(0, width, group_w)]

    def project(t):
        return jnp.dot(h, w_ref[:, t * group_w:(t + 1) * group_w], preferred_element_type=F32)

    def finish(t, y):
        o_ref, off = tiles[t]
        for s in range(group_w // MXU_TILE):
            col = t * group_w + s * MXU_TILE
            ys = y[:, s * MXU_TILE:(s + 1) * MXU_TILE]
            if col < n_norm_cols:
                ss = jnp.dot((ys * ys).astype(BF16), p_ref[...], preferred_element_type=F32)
                ys = ys * lax.rsqrt(ss * (1.0 / group) + EPS) * cs_ref[:, col:col + MXU_TILE]
            lo = off + s * MXU_TILE
            if slabs:
                for half in range(MXU_TILE // LANES):
                    o_ref[lo // LANES + half] = ys[:, half * LANES:(half + 1) * LANES]
            else:
                o_ref[:, lo:lo + MXU_TILE] = ys.astype(BF16)

    y = project(0)
    for t in range(len(tiles)):
        y_next = project(t + 1) if t + 1 < len(tiles) else None
        finish(t, y)
        y = y_next


def _project(x2, gain, w, colscale, widths, n_norm_cols, group, slabs=False, tm=512):
    m = x2.shape[0]
    n = w.shape[1]
    assert sum(widths) == n and m % tm == 0 and n_norm_cols % MXU_TILE == 0
    assert all(wd % (2 * MXU_TILE) == 0 for wd in widths)
    r = jnp.arange(MXU_TILE) // group
    ones_bd = (r[:, None] == r[None, :]).astype(BF16)
    const = lambda i: (0, 0)
    if slabs:
        out_shape = [jax.ShapeDtypeStruct((wd // LANES, m, LANES), F32) for wd in widths]
        out_specs = [pl.BlockSpec((wd // LANES, tm, LANES), lambda i: (0, i, 0)) for wd in widths]
    else:
        out_shape = [jax.ShapeDtypeStruct((m, wd), BF16) for wd in widths]
        out_specs = [pl.BlockSpec((tm, wd), lambda i: (i, 0)) for wd in widths]
    return pl.pallas_call(
        functools.partial(_proj_kernel, widths=tuple(widths), n_norm_cols=n_norm_cols, group=group, slabs=slabs),
        out_shape=out_shape,
        grid=(m // tm,),
        in_specs=[pl.BlockSpec((tm, D_MODEL), lambda i: (i, 0)),
                  pl.BlockSpec((1, D_MODEL), const),
                  pl.BlockSpec((D_MODEL, n), const),
                  pl.BlockSpec((1, n), const),
                  pl.BlockSpec((MXU_TILE, MXU_TILE), const)],
        out_specs=out_specs,
        compiler_params=_params(("arbitrary",)),
        name="qkv_proj",
    )(x2, gain.reshape(1, D_MODEL), w, colscale.reshape(1, n), ones_bd)


def _attn_a_kernel(sink_ref, q_ref, k_ref, v_ref, bias_ref, o_ref, *, nq, n_tiles):
    i = pl.program_id(1)
    lane = lax.broadcasted_iota(jnp.int32, (LANES, LANES), 1)
    lo = lane < A_HEAD_DIM
    group = A_HEADS // A_KV_HEADS

    def window(s):
        t = i * nq + s
        base = jnp.clip(t - 1, 0, n_tiles - 3)
        return t, base, pl.ds(pl.multiple_of(base * LANES, LANES), 3 * LANES)

    def scores(s, hk):
        rows = slice(s * LANES, (s + 1) * LANES)
        parts = []
        for c in (2 * hk, 2 * hk + 1):
            qc = q_ref[rows, c * LANES:(c + 1) * LANES]
            parts += [jnp.where(lo, qc, jnp.zeros_like(qc)), jnp.where(lo, jnp.zeros_like(qc), qc)]
        qs = jnp.concatenate(parts, axis=0)
        kw = k_ref[window(s)[2], hk * LANES:(hk + 1) * LANES]
        return lax.dot_general(qs, kw, (((1,), (1,)), ((), ())), preferred_element_type=F32)

    def finish(s, hk, sc):
        t, base, krows = window(s)
        rows = slice(s * LANES, (s + 1) * LANES)
        blocks = []
        for g in range(group):
            hq = hk * group + g
            blocks.append(jnp.concatenate(
                [sc[g * LANES:(g + 1) * LANES, w * LANES:(w + 1) * LANES] + bias_ref[hq, base + w - t + 2]
                 for w in range(3)], axis=1))
        sc = jnp.concatenate(blocks, axis=0)
        mx = jnp.max(sc, axis=-1, keepdims=True)
        p = jnp.exp2(sc - mx)
        den = jnp.sum(p, axis=-1, keepdims=True)
        pv = jnp.dot(p.astype(BF16), v_ref[krows, hk * LANES:(hk + 1) * LANES],
                     preferred_element_type=F32)
        outs = []
        for g in range(group):
            hq = hk * group + g
            gr = slice(g * LANES, (g + 1) * LANES)
            outs.append(pv[gr] / (den[gr] + jnp.exp2(sink_ref[hq] * LOG2E - mx[gr])))
        o_ref[rows, (2 * hk) * LANES:(2 * hk + 1) * LANES] = jnp.where(lo, outs[0], outs[1]).astype(BF16)
        o_ref[rows, (2 * hk + 1) * LANES:(2 * hk + 2) * LANES] = jnp.where(lo, outs[2], outs[3]).astype(BF16)

    blocks_todo = [(s, hk) for s in range(nq) for hk in range(A_KV_HEADS)]
    sc = scores(*blocks_todo[0])
    for n, blk in enumerate(blocks_todo):
        sc_next = scores(*blocks_todo[n + 1]) if n + 1 < len(blocks_todo) else None
        finish(*blk, sc)
        sc = sc_next


def _attention_a(qkv, bias, sink, batch, seq, tq=256):
    nq = tq // LANES
    n_tiles = seq // LANES
    qkv3 = qkv.reshape(batch, seq, 2 * D_MODEL)
    return pl.pallas_call(
        functools.partial(_attn_a_kernel, nq=nq, n_tiles=n_tiles),
        out_shape=jax.ShapeDtypeStruct((batch, seq, D_MODEL), BF16),
        grid=(batch, seq // tq),
        in_specs=[pl.BlockSpec(memory_space=pltpu.SMEM),
                  pl.BlockSpec((None, tq, D_MODEL), lambda b, i: (b, i, 0)),
                  pl.BlockSpec((None, seq, 512), lambda b, i: (b, 0, 2)),
                  pl.BlockSpec((None, seq, 512), lambda b, i: (b, 0, 3)),
                  pl.BlockSpec((A_HEADS, 5, LANES, LANES), lambda b, i: (0, 0, 0, 0))],
        out_specs=pl.BlockSpec((None, tq, D_MODEL), lambda b, i: (b, i, 0)),
        compiler_params=_params(("arbitrary", "arbitrary")),
        name="attn_window",
    )(sink, qkv3, qkv3, qkv3, bias)


def _attn_b_kernel(lam_ref, sg_ref, q_ref, k_ref, v_ref, bias_ref, o_ref, qs_ref, v1_ref, m_ref, acc_ref,
                   *, tq, tk, n_kc, lambda_init):
    i = pl.program_id(2)
    nqt = tq // LANES
    nkt = tk // LANES

    @pl.when(i == 0)
    def _():
        v1_ref[:, 0:LANES] = v_ref[...]
        v1_ref[:, LANES:2 * LANES] = jnp.ones(v_ref.shape, BF16)

    lo = lax.broadcasted_iota(jnp.int32, (tq, LANES), 1) < B_HEAD_DIM
    q = q_ref[...]
    zero = jnp.zeros_like(q)
    qs_ref[0:tq, :] = jnp.where(lo, q, zero)
    qs_ref[tq:2 * tq, :] = jnp.where(lo, zero, q)
    m_ref[...] = jnp.full(m_ref.shape, -jnp.inf, F32)
    acc_ref[...] = jnp.zeros(acc_ref.shape, F32)

    def map_chunk(mp, kc):
        mrows = slice(mp * tq, (mp + 1) * tq)
        sc = lax.dot_general(qs_ref[mrows, :], k_ref[kc * tk:(kc + 1) * tk, :], (((1,), (1,)), ((), ())),
                             preferred_element_type=F32)
        probs = []
        for a in range(nqt):
            rows = slice(mp * tq + a * LANES, mp * tq + (a + 1) * LANES)
            tiles = []
            for w in range(nkt):
                d = jnp.clip(kc * nkt + w - (i * nqt + a), -B_FAR_TILES, B_FAR_TILES) + B_FAR_TILES
                tiles.append(sc[a * LANES:(a + 1) * LANES, w * LANES:(w + 1) * LANES] + bias_ref[mp, d])
            m_old = m_ref[rows, :]
            m_new = jnp.maximum(m_old, jnp.max(functools.reduce(jnp.maximum, tiles), axis=-1, keepdims=True))
            alpha = jnp.exp2(m_old - m_new)
            p = [jnp.exp2((t - m_new).astype(BF16)) for t in tiles]
            acc_ref[rows, :] = jnp.concatenate([alpha, alpha], axis=1) * acc_ref[rows, :]
            m_ref[rows, :] = m_new
            probs.append(jnp.concatenate(p, axis=1))
        acc_ref[mrows, :] += jnp.dot(jnp.concatenate(probs, axis=0), v1_ref[kc * tk:(kc + 1) * tk, :],
                                     preferred_element_type=F32)

    for kc in range(n_kc):
        for mp in range(2):
            map_chunk(mp, kc)
    lv = lam_ref[...]
    lam = (jnp.exp(jnp.sum(lv[0:1] * lv[1:2], axis=-1, keepdims=True))
           - jnp.exp(jnp.sum(lv[2:3] * lv[3:4], axis=-1, keepdims=True)) + lambda_init)
    o = acc_ref[:, 0:LANES] / acc_ref[:, LANES:2 * LANES]
    o = o[:tq] - lam * o[tq:]
    o_ref[...] = (_rms(o, sg_ref[...]) * (1.0 - lambda_init)).astype(BF16)


def _attention_b(qkv, bias, lam_vecs, sub_gain, lambda_init, batch, seq, tq=512, tk=512):
    qkv3 = qkv.reshape(batch, seq, 3 * D_MODEL)
    nb = 2 * B_FAR_TILES + 1
    return pl.pallas_call(
        functools.partial(_attn_b_kernel, tq=tq, tk=tk, n_kc=seq // tk, lambda_init=lambda_init),
        out_shape=jax.ShapeDtypeStruct((batch, seq, D_MODEL), BF16),
        grid=(batch, B_HEADS, seq // tq),
        in_specs=[pl.BlockSpec((4, B_HEAD_DIM), lambda b, h, i: (0, 0)),
                  pl.BlockSpec((1, LANES), lambda b, h, i: (0, 0)),
                  pl.BlockSpec((None, tq, LANES), lambda b, h, i: (b, i, h)),
                  pl.BlockSpec((None, seq, LANES), lambda b, h, i: (b, 0, B_HEADS + h)),
                  pl.BlockSpec((None, seq, LANES), lambda b, h, i: (b, 0, 2 * B_HEADS + h)),
                  pl.BlockSpec((None, 2, nb, LANES, LANES), lambda b, h, i: (h, 0, 0, 0, 0))],
        out_specs=pl.BlockSpec((None, tq, LANES), lambda b, h, i: (b, i, h)),
        scratch_shapes=[pltpu.VMEM((2 * tq, LANES), BF16), pltpu.VMEM((seq, 2 * LANES), BF16),
                        pltpu.VMEM((2 * tq, LANES), F32), pltpu.VMEM((2 * tq, 2 * LANES), F32)],
        compiler_params=_params(("arbitrary", "arbitrary", "arbitrary")),
        name="attn_diff",
    )(lam_vecs, sub_gain.reshape(1, LANES), qkv3, qkv3, qkv3, bias)


def _attn_c_kernel(q_ref, k_ref, v_ref, bias_ref, o_ref, og_ref, lg_ref, *, seq, npos, merge_rows):
    c = pl.program_id(2)

    def rows_of(start, n, dil):
        return pl.ds(start, n) if dil == 1 else pl.ds(start, n, stride=dil)

    def window(g, r, tt):
        dil = C_BRANCHES[g][1]
        t0 = c * (npos // dil) + tt * LANES
        ws = jnp.clip(t0 - C_HALF, 0, seq // dil - 2 * LANES)
        return t0, ws, rows_of(r + dil * LANES * tt, LANES, dil), rows_of(r + dil * ws, 2 * LANES, dil)

    def scores(g, r, tt):
        _, _, qrows, krows = window(g, r, tt)
        return lax.dot_general(q_ref[g, qrows, :].astype(BF16), k_ref[krows, :].astype(BF16),
                               (((1,), (1,)), ((), ())), preferred_element_type=F32)

    def finish(g, r, tt, sc):
        t0, ws, qrows, krows = window(g, r, tt)
        sc = jnp.concatenate(
            [sc[:, w * LANES:(w + 1) * LANES] + bias_ref[g, (ws + w * LANES - t0) // C_HALF + 2]
             for w in range(2)], axis=1)
        mx = jnp.max(sc, axis=-1, keepdims=True)
        p = jnp.exp2(sc - mx)
        den = jnp.sum(p, axis=-1, keepdims=True)
        og_ref[g, qrows, :] = jnp.dot(p.astype(BF16), v_ref[krows, :].astype(BF16), preferred_element_type=F32) / den
        lg_ref[g, qrows, :] = jnp.broadcast_to(mx + jnp.log2(den), (LANES, LANES))

    blocks_todo = [(g, r, tt) for g, (_, dil) in enumerate(C_BRANCHES) for r in range(dil)
                   for tt in range(npos // (dil * LANES))]
    sc = scores(*blocks_todo[0])
    for n, blk in enumerate(blocks_todo):
        sc_next = scores(*blocks_todo[n + 1]) if n + 1 < len(blocks_todo) else None
        finish(*blk, sc)
        sc = sc_next
    for blk in range(npos // merge_rows):
        rows = slice(blk * merge_rows, (blk + 1) * merge_rows)
        lses = [lg_ref[g, rows, :] for g in range(len(C_BRANCHES))]
        mx = functools.reduce(jnp.maximum, lses)
        es = [jnp.exp2(l - mx) for l in lses]
        num = functools.reduce(jnp.add, [e * og_ref[g, rows, :] for g, e in enumerate(es)])
        o_ref[rows, :] = (num / functools.reduce(jnp.add, es)).astype(BF16)


def _attention_c(q, k, v, bias, batch, seq, npos=2048, merge_rows=256):
    n_groups = len(C_BRANCHES)
    m = batch * seq
    chunks = seq // npos
    q4 = q.reshape(n_groups, C_KV_HEADS, m, LANES)
    bias5 = bias.reshape(n_groups, C_KV_HEADS, 5, LANES, LANES)
    return pl.pallas_call(
        functools.partial(_attn_c_kernel, seq=seq, npos=npos, merge_rows=merge_rows),
        out_shape=jax.ShapeDtypeStruct((m, C_KV_HEADS * C_HEAD_DIM), BF16),
        grid=(batch, C_KV_HEADS, chunks),
        in_specs=[pl.BlockSpec((n_groups, None, npos, LANES), lambda b, j, c: (0, j, b * chunks + c, 0)),
                  pl.BlockSpec((None, seq, LANES), lambda b, j, c: (j, b, 0)),
                  pl.BlockSpec((None, seq, LANES), lambda b, j, c: (j, b, 0)),
                  pl.BlockSpec((n_groups, None, 5, LANES, LANES), lambda b, j, c: (0, j, 0, 0, 0))],
        out_specs=pl.BlockSpec((npos, LANES), lambda b, j, c: (b * chunks + c, j)),
        scratch_shapes=[pltpu.VMEM((n_groups, npos, LANES), F32), pltpu.VMEM((n_groups, npos, LANES), F32)],
        compiler_params=_params(("arbitrary", "arbitrary", "arbitrary")),
        name="attn_dilated",
    )(q4, k, v, bias5)


def _ffn_kernel(x_ref, xn_ref, xp_ref, o_ref, on_ref, op_ref, wo_ref, g_ref, wup_ref, cw_ref, cb_ref, wdn_ref,
                y_ref, oe_ref, h_ref, acc_ref, ua_ref, ub_ref, *, tm, tiles_per_seq, n_chunks):
    i = pl.program_id(0)
    pos = i % tiles_per_seq
    kdim = o_ref.shape[1]
    halo_row = lax.broadcasted_iota(jnp.int32, (FFN_HALO, kdim), 0)
    oe_ref[0:tm, :] = o_ref[...]
    oe_ref[tm:tm + FFN_HALO, :] = jnp.where(halo_row < 8, on_ref[...], op_ref[...])
    attn = jnp.dot(oe_ref[...], wo_ref[...], preferred_element_type=F32)
    x1 = x_ref[...] + attn[0:tm]
    x1n = xn_ref[...] + attn[tm:tm + 8]
    x1p = xp_ref[...] + attn[tm + 8:tm + FFN_HALO]
    g = g_ref[...]
    h_ref[0:tm, :] = _rms(x1, g).astype(BF16)
    hn = jnp.where(pos == tiles_per_seq - 1, 0.0, _rms(x1n, g))
    hp = jnp.where(pos == 0, 0.0, _rms(x1p, g))
    h_ref[tm:tm + FFN_HALO, :] = jnp.concatenate([hn, hp], axis=0).astype(BF16)
    acc_ref[...] = x1
    slabs = FFN_CHUNK // LANES

    def chunk_cols(k):
        return pl.ds(pl.multiple_of(k * FFN_CHUNK, FFN_CHUNK), FFN_CHUNK)

    def up_proj(j, u_ref):
        h = h_ref[...]
        for half, k in enumerate((j, j + n_chunks)):
            u = jnp.dot(h, wup_ref[:, chunk_cols(k)], preferred_element_type=F32)
            for s in range(slabs):
                cols = slice(s * LANES, (s + 1) * LANES)
                slab = half * slabs + s
                u_ref[slab, 8:8 + tm, :] = u[0:tm, cols]
                u_ref[slab, 0:8, :] = u[tm + 8:tm + 16, cols]
                u_ref[slab, tm + 8:tm + 16, :] = u[tm:tm + 8, cols]

    def conv(u_ref, slab, k, cols):
        w = cw_ref[:, chunk_cols(k)]
        return (w[0:1, cols] * u_ref[slab, pl.ds(7, tm), :] + w[1:2, cols] * u_ref[slab, pl.ds(8, tm), :]
                + w[2:3, cols] * u_ref[slab, pl.ds(9, tm), :] + cb_ref[:, chunk_cols(k)][:, cols])

    def gate_down(j, u_ref):
        acts = []
        for s in range(slabs):
            cols = slice(s * LANES, (s + 1) * LANES)
            gate = conv(u_ref, s, j, cols)
            val = conv(u_ref, slabs + s, j + n_chunks, cols)
            acts.append((gate / (1.0 + jnp.exp(-gate)) * val).astype(BF16))
        acc_ref[...] += jnp.dot(jnp.concatenate(acts, axis=1), wdn_ref[chunk_cols(j), :],
                                preferred_element_type=F32)

    up_proj(0, ua_ref)

    def pair(c, carry):
        j = 2 * c
        up_proj(j + 1, ub_ref)
        gate_down(j, ua_ref)
        up_proj(j + 2, ua_ref)
        gate_down(j + 1, ub_ref)
        return carry

    assert n_chunks % 2 == 1
    lax.fori_loop(0, n_chunks // 2, pair, 0, unroll=True)
    gate_down(n_chunks - 1, ua_ref)
    y_ref[...] = acc_ref[...]


def _proj_ffn(x2, o2, w_o, gain, w_up, conv_w, conv_b, w_down, seq, tm=512):
    m = x2.shape[0]
    kdim = o2.shape[1]
    n_chunks = D_FF // FFN_CHUNK
    tiles_per_seq = seq // tm
    x_halo = tm // 8
    o_halo = tm // FFN_HALO
    const = lambda i: (0, 0)
    once = pl.Buffered(1)
    nxt = lambda blocks, per_tile: (lambda i: (jnp.minimum((i + 1) * per_tile, blocks - 1), 0))
    prv = lambda per_tile: (lambda i: (jnp.maximum(i * per_tile - 1, 0), 0))
    return pl.pallas_call(
        functools.partial(_ffn_kernel, tm=tm, tiles_per_seq=tiles_per_seq, n_chunks=n_chunks),
        out_shape=jax.ShapeDtypeStruct((m, D_MODEL), F32),
        grid=(m // tm,),
        in_specs=[pl.BlockSpec((tm, D_MODEL), lambda i: (i, 0)),
                  pl.BlockSpec((8, D_MODEL), nxt(m // 8, x_halo)),
                  pl.BlockSpec((8, D_MODEL), prv(x_halo)),
                  pl.BlockSpec((tm, kdim), lambda i: (i, 0)),
                  pl.BlockSpec((FFN_HALO, kdim), nxt(m // FFN_HALO, o_halo)),
                  pl.BlockSpec((FFN_HALO, kdim), prv(o_halo)),
                  pl.BlockSpec((kdim, D_MODEL), const, pipeline_mode=once),
                  pl.BlockSpec((1, D_MODEL), const),
                  pl.BlockSpec((D_MODEL, 2 * D_FF), const, pipeline_mode=once),
                  pl.BlockSpec((3, 2 * D_FF), const),
                  pl.BlockSpec((1, 2 * D_FF), const),
                  pl.BlockSpec((D_FF, D_MODEL), const, pipeline_mode=once)],
        out_specs=pl.BlockSpec((tm, D_MODEL), lambda i: (i, 0)),
        scratch_shapes=[pltpu.VMEM((tm + FFN_HALO, kdim), BF16),
                        pltpu.VMEM((tm + FFN_HALO, D_MODEL), BF16), pltpu.VMEM((tm, D_MODEL), F32),
                        pltpu.VMEM((2 * FFN_CHUNK // LANES, tm + FFN_HALO, LANES), F32),
                        pltpu.VMEM((2 * FFN_CHUNK // LANES, tm + FFN_HALO, LANES), F32)],
        compiler_params=_params(("arbitrary",)),
        name="proj_conv_ffn",
    )(x2, x2, x2, o2, o2, o2, w_o.astype(BF16), gain.reshape(1, D_MODEL), w_up.astype(BF16),
      conv_w.reshape(3, 2 * D_FF), conv_b.reshape(1, 2 * D_FF), w_down.astype(BF16))


def _dup_heads(w, n_heads, dim):
    w = w.reshape(D_MODEL, n_heads, 1, dim)
    return jnp.broadcast_to(w, (D_MODEL, n_heads, 2, dim)).reshape(D_MODEL, 2 * n_heads * dim)


def _mixer_a(x2, norm_gain, w_qkv, q_gain, k_gain, sink, bias, batch, seq):
    nq = A_HEADS * A_HEAD_DIM
    nk = A_KV_HEADS * A_HEAD_DIM
    w = jnp.concatenate([w_qkv[:, :nq], _dup_heads(w_qkv[:, nq:nq + nk], A_KV_HEADS, A_HEAD_DIM),
                         _dup_heads(w_qkv[:, nq + nk:], A_KV_HEADS, A_HEAD_DIM)], axis=1).astype(BF16)
    colscale = jnp.concatenate([jnp.tile(q_gain, A_HEADS) * (LOG2E * A_HEAD_DIM ** -0.5), jnp.tile(k_gain, 2 * A_KV_HEADS),
                                jnp.ones((2 * nk,), F32)])
    (qkv,) = _project(x2, norm_gain, w, colscale, [2 * D_MODEL], nq + 2 * nk, A_HEAD_DIM)
    return _attention_a(qkv, bias, sink, batch, seq).reshape(batch * seq, D_MODEL)


def _mixer_b(x2, norm_gain, w_qkv, q_gain, k_gain, lam_q1, lam_k1, lam_q2, lam_k2, sub_gain, bias,
             lambda_init, batch, seq):
    colscale = jnp.concatenate([jnp.tile(q_gain, 2 * B_HEADS) * (LOG2E * B_HEAD_DIM ** -0.5), jnp.tile(k_gain, 2 * B_HEADS),
                                jnp.ones((D_MODEL,), F32)])
    (qkv,) = _project(x2, norm_gain, w_qkv.astype(BF16), colscale, [3 * D_MODEL], 2 * D_MODEL, B_HEAD_DIM)
    lam_vecs = jnp.stack([lam_q1, lam_k1, lam_q2, lam_k2])
    return _attention_b(qkv, bias, lam_vecs, sub_gain, lambda_init, batch, seq).reshape(batch * seq, D_MODEL)


def _mixer_c(x2, norm_gain, w_qkv, q_gain, k_gain, bias, batch, seq):
    width = C_KV_HEADS * C_HEAD_DIM
    n_groups = len(C_BRANCHES)
    colscale = jnp.concatenate([jnp.tile(q_gain, n_groups * C_KV_HEADS) * (LOG2E * C_HEAD_DIM ** -0.5),
                                jnp.tile(k_gain, C_KV_HEADS), jnp.ones((width,), F32)])
    assert all(window // (2 * dil) == C_HALF for window, dil in C_BRANCHES)
    q, k, v = _project(x2, norm_gain, w_qkv.astype(BF16), colscale, [n_groups * width, width, width],
                       (n_groups + 1) * width, C_HEAD_DIM, slabs=True)
    return _attention_c(q, k, v, bias, batch, seq)


def _lambda_init(layer):
    return 0.8 - 0.6 * math.exp(-0.3 * layer)


def kernel(x, rel_bias, l0_attn_norm, l0_w_qkv, l0_q_gain, l0_k_gain, l0_sink, l0_w_o, l0_ffn_norm, l0_w_up, l0_conv_w, l0_conv_b, l0_w_down, l1_attn_norm, l1_w_qkv, l1_q_gain, l1_k_gain, l1_lambda_q1, l1_lambda_k1, l1_lambda_q2, l1_lambda_k2, l1_sub_gain, l1_w_o, l1_ffn_norm, l1_w_up, l1_conv_w, l1_conv_b, l1_w_down, l2_attn_norm, l2_w_qkv, l2_q_gain, l2_k_gain, l2_w_o, l2_ffn_norm, l2_w_up, l2_conv_w, l2_conv_b, l2_w_down, l3_attn_norm, l3_w_qkv, l3_q_gain, l3_k_gain, l3_sink, l3_w_o, l3_ffn_norm, l3_w_up, l3_conv_w, l3_conv_b, l3_w_down):
    batch, seq, _ = x.shape
    assert seq % (2 * LANES * C_BRANCHES[-1][1]) == 0
    bias_a = _bias_tiles(rel_bias, A_HEADS, [1], [LANES * d for d in range(-2, 3)], A_WINDOW)
    bias_b = _bias_tiles(rel_bias, 2 * B_HEADS, [1], [LANES * d for d in range(-B_FAR_TILES, B_FAR_TILES + 1)], None)
    bias_c = _bias_tiles(rel_bias, len(C_BRANCHES) * C_KV_HEADS, [dil for _, dil in C_BRANCHES],
                         [C_HALF * d for d in range(-2, 3)], C_HALF)
    bias_b = bias_b.reshape(B_HEADS, 2, 2 * B_FAR_TILES + 1, LANES, LANES)

    x2 = x.reshape(batch * seq, D_MODEL)
    o = _mixer_a(x2, l0_attn_norm, l0_w_qkv, l0_q_gain, l0_k_gain, l0_sink, bias_a, batch, seq)
    x2 = _proj_ffn(x2, o, l0_w_o, l0_ffn_norm, l0_w_up, l0_conv_w, l0_conv_b, l0_w_down, seq)
    o = _mixer_b(x2, l1_attn_norm, l1_w_qkv, l1_q_gain, l1_k_gain, l1_lambda_q1, l1_lambda_k1, l1_lambda_q2,
                 l1_lambda_k2, l1_sub_gain, bias_b, _lambda_init(1), batch, seq)
    x2 = _proj_ffn(x2, o, l1_w_o, l1_ffn_norm, l1_w_up, l1_conv_w, l1_conv_b, l1_w_down, seq)
    o = _mixer_c(x2, l2_attn_norm, l2_w_qkv, l2_q_gain, l2_k_gain, bias_c, batch, seq)
    x2 = _proj_ffn(x2, o, l2_w_o, l2_ffn_norm, l2_w_up, l2_conv_w, l2_conv_b, l2_w_down, seq)
    o = _mixer_a(x2, l3_attn_norm, l3_w_qkv, l3_q_gain, l3_k_gain, l3_sink, bias_a, batch, seq)
    x2 = _proj_ffn(x2, o, l3_w_o, l3_ffn_norm, l3_w_up, l3_conv_w, l3_conv_b, l3_w_down, seq)
    return x2.reshape(batch, seq, D_MODEL)
```

```python
import functools
import math

import jax
import jax.numpy as jnp
from jax import lax
from jax.experimental import pallas as pl
from jax.experimental.pallas import tpu as pltpu

F32 = jnp.float32
BF16 = jnp.bfloat16

D_MODEL = 1024
EPS = 1e-6
NEG_INF = -1e30
LOG2E = math.log2(math.e)
LANES = 128
MXU_TILE = 256
VMEM_LIMIT = 56 * 1024 * 1024

NUM_BUCKETS = 32
MAX_DISTANCE = 1024
N_BIAS_HEADS = 16

A_HEADS, A_KV_HEADS, A_HEAD_DIM, A_WINDOW = 16, 4, 64, 128
B_HEADS, B_HEAD_DIM = 8, 64
B_FAR_TILES = 6
C_BRANCHES = ((128, 1), (512, 4), (2048, 16))
C_KV_HEADS, C_HEAD_DIM = 4, 128
C_HALF = 64
D_FF = 2816
FFN_CHUNK = 256
FFN_HALO = 16


def _params(sem, vmem=VMEM_LIMIT):
    return pltpu.CompilerParams(dimension_semantics=sem, vmem_limit_bytes=vmem)


def _rms(x, gain):
    return x * lax.rsqrt(jnp.mean(x * x, axis=-1, keepdims=True) + EPS) * gain


def _bias_tile_kernel(tab_ref, dil_ref, off_ref, out_ref, *, half, heads_per_group):
    grp = pl.program_id(0)
    t = pl.program_id(1)
    dil = dil_ref[grp]
    off = off_ref[t]
    q = lax.broadcasted_iota(jnp.int32, (LANES, LANES), 0)
    c = lax.broadcasted_iota(jnp.int32, (LANES, LANES), 1)
    x = c - q + off
    rel = x * dil
    nb = NUM_BUCKETS // 2
    max_exact = nb // 2
    n = jnp.abs(rel)
    nf = jnp.maximum(n, 1).astype(F32)
    large = max_exact + (jnp.log(nf * (1.0 / max_exact)) / math.log(MAX_DISTANCE / max_exact)
                         * (nb - max_exact)).astype(jnp.int32)
    large = jnp.minimum(large, nb - 1)
    bucket = jnp.where(rel > 0, nb, 0) + jnp.where(n < max_exact, n, large)
    masks = [(bucket & (1 << bit)) != 0 for bit in range(5)]
    inside = None if half is None else jnp.abs(x) <= half
    for hh in range(heads_per_group):
        col = grp * heads_per_group + hh
        level = [tab_ref[b, col] for b in range(NUM_BUCKETS)]
        for mask in masks:
            level = [jnp.where(mask, level[2 * k + 1], level[2 * k]) for k in range(len(level) // 2)]
        bias = level[0] * LOG2E
        if inside is not None:
            bias = jnp.where(inside, bias, NEG_INF)
        out_ref[hh] = bias


def _bias_tiles(rel_bias, n_heads, group_dils, offs, half):
    n_groups, nt = len(group_dils), len(offs)
    hpg = n_heads // n_groups
    smem = pl.BlockSpec(memory_space=pltpu.SMEM)
    return pl.pallas_call(
        functools.partial(_bias_tile_kernel, half=half, heads_per_group=hpg),
        out_shape=jax.ShapeDtypeStruct((n_heads, nt, LANES, LANES), F32),
        grid=(n_groups, nt),
        in_specs=[smem, smem, smem],
        out_specs=pl.BlockSpec((hpg, None, LANES, LANES), lambda g, t: (g, t, 0, 0)),
        compiler_params=_params(("arbitrary", "arbitrary")),
        name="bias_tiles",
    )(rel_bias, jnp.asarray(group_dils, jnp.int32), jnp.asarray(offs, jnp.int32))


def _proj_kernel(x_ref, g_ref, w_ref, cs_ref, p_ref, *out_refs, widths, n_norm_cols, group, slabs):
    h = _rms(x_ref[...], g_ref[...]).astype(BF16)
    group_w = 2 * MXU_TILE
    tiles = [(o_ref, off) for o_ref, width in zip(out_refs, widths) for off in range(0, width, group_w)]

    def project(t):
        return jnp.dot(h, w_ref[:, t * group_w:(t + 1) * group_w], preferred_element_type=F32)

    def finish(t, y):
        o_ref, off = tiles[t]
        for s in range(group_w // MXU_TILE):
            col = t * group_w + s * MXU_TILE
            ys = y[:, s * MXU_TILE:(s + 1) * MXU_TILE]
            if col < n_norm_cols:
                ss = jnp.dot((ys * ys).astype(BF16), p_ref[...], preferred_element_type=F32)
                ys = ys * lax.rsqrt(ss * (1.0 / group) + EPS) * cs_ref[:, col:col + MXU_TILE]
            lo = off + s * MXU_TILE
            if slabs:
                for half in range(MXU_TILE // LANES):
                    o_ref[lo // LANES + half] = ys[:, half * LANES:(half + 1) * LANES]
            else:
                o_ref[:, lo:lo + MXU_TILE] = ys.astype(BF16)

    y = project(0)
    for t in range(len(tiles)):
        y_next = project(t + 1) if t + 1 < len(tiles) else None
        finish(t, y)
        y = y_next


def _project(x2, gain, w, colscale, widths, n_norm_cols, group, slabs=False, tm=512):
    m = x2.shape[0]
    n = w.shape[1]
    assert sum(widths) == n and m % tm == 0 and n_norm_cols % MXU_TILE == 0
    assert all(wd % (2 * MXU_TILE) == 0 for wd in widths)
    r = jnp.arange(MXU_TILE) // group
    ones_bd = (r[:, None] == r[None, :]).astype(BF16)
    const = lambda i: (0, 0)
    if slabs:
        out_shape = [jax.ShapeDtypeStruct((wd // LANES, m, LANES), F32) for wd in widths]
        out_specs = [pl.BlockSpec((wd // LANES, tm, LANES), lambda i: (0, i, 0)) for wd in widths]
    else:
        out_shape = [jax.ShapeDtypeStruct((m, wd), BF16) for wd in widths]
        out_specs = [pl.BlockSpec((tm, wd), lambda i: (i, 0)) for wd in widths]
    return pl.pallas_call(
        functools.partial(_proj_kernel, widths=tuple(widths), n_norm_cols=n_norm_cols, group=group, slabs=slabs),
        out_shape=out_shape,
        grid=(m // tm,),
        in_specs=[pl.BlockSpec((tm, D_MODEL), lambda i: (i, 0)),
                  pl.BlockSpec((1, D_MODEL), const),
                  pl.BlockSpec((D_MODEL, n), const),
                  pl.BlockSpec((1, n), const),
                  pl.BlockSpec((MXU_TILE, MXU_TILE), const)],
        out_specs=out_specs,
        compiler_params=_params(("arbitrary",)),
        name="qkv_proj",
    )(x2, gain.reshape(1, D_MODEL), w, colscale.reshape(1, n), ones_bd)


def _attn_a_kernel(sink_ref, q_ref, k_ref, v_ref, bias_ref, o_ref, *, nq, n_tiles):
    i = pl.program_id(1)
    lane = lax.broadcasted_iota(jnp.int32, (LANES, LANES), 1)
    lo = lane < A_HEAD_DIM
    group = A_HEADS // A_KV_HEADS

    def window(s):
        t = i * nq + s
        base = jnp.clip(t - 1, 0, n_tiles - 3)
        return t, base, pl.ds(pl.multiple_of(base * LANES, LANES), 3 * LANES)

    def scores(s, hk):
        rows = slice(s * LANES, (s + 1) * LANES)
        parts = []
        for c in (2 * hk, 2 * hk + 1):
            qc = q_ref[rows, c * LANES:(c + 1) * LANES]
            parts += [jnp.where(lo, qc, jnp.zeros_like(qc)), jnp.where(lo, jnp.zeros_like(qc), qc)]
        qs = jnp.concatenate(parts, axis=0)
        kw = k_ref[window(s)[2], hk * LANES:(hk + 1) * LANES]
        return lax.dot_general(qs, kw, (((1,), (1,)), ((), ())), preferred_element_type=F32)

    def finish(s, hk, sc):
        t, base, krows = window(s)
        rows = slice(s * LANES, (s + 1) * LANES)
        blocks = []
        for g in range(group):
            hq = hk * group + g
            blocks.append(jnp.concatenate(
                [sc[g * LANES:(g + 1) * LANES, w * LANES:(w + 1) * LANES] + bias_ref[hq, base + w - t + 2]
                 for w in range(3)], axis=1))
        sc = jnp.concatenate(blocks, axis=0)
        mx = jnp.max(sc, axis=-1, keepdims=True)
        p = jnp.exp2(sc - mx)
        den = jnp.sum(p, axis=-1, keepdims=True)
        pv = jnp.dot(p.astype(BF16), v_ref[krows, hk * LANES:(hk + 1) * LANES],
                     preferred_element_type=F32)
        outs = []
        for g in range(group):
            hq = hk * group + g
            gr = slice(g * LANES, (g + 1) * LANES)
            outs.append(pv[gr] / (den[gr] + jnp.exp2(sink_ref[hq] * LOG2E - mx[gr])))
        o_ref[rows, (2 * hk) * LANES:(2 * hk + 1) * LANES] = jnp.where(lo, outs[0], outs[1]).astype(BF16)
        o_ref[rows, (2 * hk + 1) * LANES:(2 * hk + 2) * LANES] = jnp.where(lo, outs[2], outs[3]).astype(BF16)

    blocks_todo = [(s, hk) for s in range(nq) for hk in range(A_KV_HEADS)]
    sc = scores(*blocks_todo[0])
    for n, blk in enumerate(blocks_todo):
        sc_next = scores(*blocks_todo[n + 1]) if n + 1 < len(blocks_todo) else None
        finish(*blk, sc)
        sc = sc_next


def _attention_a(qkv, bias, sink, batch, seq, tq=256):
    nq = tq // LANES
    n_tiles = seq // LANES
    qkv3 = qkv.reshape(batch, seq, 2 * D_MODEL)
    return pl.pallas_call(
        functools.partial(_attn_a_kernel, nq=nq, n_tiles=n_tiles),
        out_shape=jax.ShapeDtypeStruct((batch, seq, D_MODEL), BF16),
        grid=(batch, seq // tq),
        in_specs=[pl.BlockSpec(memory_space=pltpu.SMEM),
                  pl.BlockSpec((None, tq, D_MODEL), lambda b, i: (b, i, 0)),
                  pl.BlockSpec((None, seq, 512), lambda b, i: (b, 0, 2)),
                  pl.BlockSpec((None, seq, 512), lambda b, i: (b, 0, 3)),
                  pl.BlockSpec((A_HEADS, 5, LANES, LANES), lambda b, i: (0, 0, 0, 0))],
        out_specs=pl.BlockSpec((None, tq, D_MODEL), lambda b, i: (b, i, 0)),
        compiler_params=_params(("arbitrary", "arbitrary")),
        name="attn_window",
    )(sink, qkv3, qkv3, qkv3, bias)


def _attn_b_kernel(lam_ref, sg_ref, q_ref, k_ref, v_ref, bias_ref, o_ref, qs_ref, m_ref, acc_ref,
                   *, tq, tk, n_kc, lambda_init):
    i = pl.program_id(2)
    nqt = tq // LANES
    nkt = tk // LANES
    lo = lax.broadcasted_iota(jnp.int32, (tq, LANES), 1) < B_HEAD_DIM
    q = q_ref[...]
    zero = jnp.zeros_like(q)
    qs_ref[0:tq, :] = jnp.where(lo, q, zero)
    qs_ref[tq:2 * tq, :] = jnp.where(lo, zero, q)
    m_ref[...] = jnp.full(m_ref.shape, -jnp.inf, F32)
    acc_ref[...] = jnp.zeros(acc_ref.shape, F32)

    def map_chunk(mp, kc):
        mrows = slice(mp * tq, (mp + 1) * tq)
        keys = slice(kc * tk, (kc + 1) * tk)
        sc = lax.dot_general(qs_ref[mrows, :], k_ref[keys, :], (((1,), (1,)), ((), ())),
                             preferred_element_type=F32)
        probs = []
        for a in range(nqt):
            rows = slice(mp * tq + a * LANES, mp * tq + (a + 1) * LANES)
            tiles = []
            for w in range(nkt):
                d = jnp.clip(kc * nkt + w - (i * nqt + a), -B_FAR_TILES, B_FAR_TILES) + B_FAR_TILES
                tiles.append(sc[a * LANES:(a + 1) * LANES, w * LANES:(w + 1) * LANES] + bias_ref[mp, d])
            m_old = m_ref[rows, :]
            m_new = jnp.maximum(m_old, jnp.max(functools.reduce(jnp.maximum, tiles), axis=-1, keepdims=True))
            alpha = jnp.exp2(m_old - m_new)
            p = [jnp.exp2((t - m_new).astype(BF16)) for t in tiles]
            acc_ref[rows, :] = jnp.concatenate([alpha, alpha], axis=1) * acc_ref[rows, :]
            m_ref[rows, :] = m_new
            probs.append(jnp.concatenate(p, axis=1))
        vw = v_ref[keys, :]
        acc_ref[mrows, :] += jnp.dot(jnp.concatenate(probs, axis=0), jnp.concatenate([vw, jnp.ones_like(vw)], axis=1),
                                     preferred_element_type=F32)

    for kc in range(n_kc):
        for mp in range(2):
            map_chunk(mp, kc)
    lv = lam_ref[...]
    lam = (jnp.exp(jnp.sum(lv[0:1] * lv[1:2], axis=-1, keepdims=True))
           - jnp.exp(jnp.sum(lv[2:3] * lv[3:4], axis=-1, keepdims=True)) + lambda_init)
    o = acc_ref[:, 0:LANES] / acc_ref[:, LANES:2 * LANES]
    o = o[:tq] - lam * o[tq:]
    o_ref[...] = (_rms(o, sg_ref[...]) * (1.0 - lambda_init)).astype(BF16)


def _attention_b(qkv, bias, lam_vecs, sub_gain, lambda_init, batch, seq, tq=512, tk=512):
    qkv3 = qkv.reshape(batch, seq, 3 * D_MODEL)
    nb = 2 * B_FAR_TILES + 1
    return pl.pallas_call(
        functools.partial(_attn_b_kernel, tq=tq, tk=tk, n_kc=seq // tk, lambda_init=lambda_init),
        out_shape=jax.ShapeDtypeStruct((batch, seq, D_MODEL), BF16),
        grid=(batch, B_HEADS, seq // tq),
        in_specs=[pl.BlockSpec((4, B_HEAD_DIM), lambda b, h, i: (0, 0)),
                  pl.BlockSpec((1, LANES), lambda b, h, i: (0, 0)),
                  pl.BlockSpec((None, tq, LANES), lambda b, h, i: (b, i, h)),
                  pl.BlockSpec((None, seq, LANES), lambda b, h, i: (b, 0, B_HEADS + h)),
                  pl.BlockSpec((None, seq, LANES), lambda b, h, i: (b, 0, 2 * B_HEADS + h)),
                  pl.BlockSpec((None, 2, nb, LANES, LANES), lambda b, h, i: (h, 0, 0, 0, 0))],
        out_specs=pl.BlockSpec((None, tq, LANES), lambda b, h, i: (b, i, h)),
        scratch_shapes=[pltpu.VMEM((2 * tq, LANES), BF16), pltpu.VMEM((2 * tq, LANES), F32),
                        pltpu.VMEM((2 * tq, 2 * LANES), F32)],
        compiler_params=_params(("arbitrary", "arbitrary", "arbitrary")),
        name="attn_diff",
    )(lam_vecs, sub_gain.reshape(1, LANES), qkv3, qkv3, qkv3, bias)


def _attn_c_kernel(q_ref, k_ref, v_ref, bias_ref, o_ref, og_ref, lg_ref, *, seq, npos, merge_rows):
    c = pl.program_id(2)

    def rows_of(start, n, dil):
        return pl.ds(start, n) if dil == 1 else pl.ds(start, n, stride=dil)

    def window(g, r, tt):
        dil = C_BRANCHES[g][1]
        t0 = c * (npos // dil) + tt * LANES
        ws = jnp.clip(t0 - C_HALF, 0, seq // dil - 2 * LANES)
        return t0, ws, rows_of(r + dil * LANES * tt, LANES, dil), rows_of(r + dil * ws, 2 * LANES, dil)

    def scores(g, r, tt):
        _, _, qrows, krows = window(g, r, tt)
        return lax.dot_general(q_ref[g, qrows, :].astype(BF16), k_ref[krows, :].astype(BF16),
                               (((1,), (1,)), ((), ())), preferred_element_type=F32)

    def finish(g, r, tt, sc):
        t0, ws, qrows, krows = window(g, r, tt)
        sc = jnp.concatenate(
            [sc[:, w * LANES:(w + 1) * LANES] + bias_ref[g, (ws + w * LANES - t0) // C_HALF + 2]
             for w in range(2)], axis=1)
        mx = jnp.max(sc, axis=-1, keepdims=True)
        p = jnp.exp2((sc - mx).astype(BF16))
        vw = v_ref[krows, :].astype(BF16)
        pv = jnp.dot(p, jnp.concatenate([vw, jnp.ones_like(vw)], axis=1), preferred_element_type=F32)
        den = pv[:, LANES:2 * LANES]
        og_ref[g, qrows, :] = pv[:, 0:LANES] / den
        lg_ref[g, qrows, :] = mx + jnp.log2(den)

    blocks_todo = [(g, r, tt) for g, (_, dil) in enumerate(C_BRANCHES) for r in range(dil)
                   for tt in range(npos // (dil * LANES))]
    sc = scores(*blocks_todo[0])
    for n, blk in enumerate(blocks_todo):
        sc_next = scores(*blocks_todo[n + 1]) if n + 1 < len(blocks_todo) else None
        finish(*blk, sc)
        sc = sc_next
    for blk in range(npos // merge_rows):
        rows = slice(blk * merge_rows, (blk + 1) * merge_rows)
        lses = [lg_ref[g, rows, :] for g in range(len(C_BRANCHES))]
        mx = functools.reduce(jnp.maximum, lses)
        es = [jnp.exp2(l - mx) for l in lses]
        num = functools.reduce(jnp.add, [e * og_ref[g, rows, :] for g, e in enumerate(es)])
        o_ref[rows, :] = (num / functools.reduce(jnp.add, es)).astype(BF16)


def _attention_c(q, k, v, bias, batch, seq, npos=2048, merge_rows=256):
    n_groups = len(C_BRANCHES)
    m = batch * seq
    chunks = seq // npos
    q4 = q.reshape(n_groups, C_KV_HEADS, m, LANES)
    bias5 = bias.reshape(n_groups, C_KV_HEADS, 5, LANES, LANES)
    return pl.pallas_call(
        functools.partial(_attn_c_kernel, seq=seq, npos=npos, merge_rows=merge_rows),
        out_shape=jax.ShapeDtypeStruct((m, C_KV_HEADS * C_HEAD_DIM), BF16),
        grid=(batch, C_KV_HEADS, chunks),
        in_specs=[pl.BlockSpec((n_groups, None, npos, LANES), lambda b, j, c: (0, j, b * chunks + c, 0)),
                  pl.BlockSpec((None, seq, LANES), lambda b, j, c: (j, b, 0)),
                  pl.BlockSpec((None, seq, LANES), lambda b, j, c: (j, b, 0)),
                  pl.BlockSpec((n_groups, None, 5, LANES, LANES), lambda b, j, c: (0, j, 0, 0, 0))],
        out_specs=pl.BlockSpec((npos, LANES), lambda b, j, c: (b * chunks + c, j)),
        scratch_shapes=[pltpu.VMEM((n_groups, npos, LANES), F32), pltpu.VMEM((n_groups, npos, LANES), F32)],
        compiler_params=_params(("arbitrary", "arbitrary", "arbitrary")),
        name="attn_dilated",
    )(q4, k, v, bias5)


def _ffn_kernel(x_ref, xn_ref, xp_ref, o_ref, on_ref, op_ref, wo_ref, g_ref, wup_ref, cw_ref, cb_ref, wdn_ref,
                y_ref, oe_ref, h_ref, acc_ref, ua_ref, ub_ref, *, tm, tiles_per_seq, n_chunks):
    i = pl.program_id(0)
    pos = i % tiles_per_seq
    kdim = o_ref.shape[1]
    halo_row = lax.broadcasted_iota(jnp.int32, (FFN_HALO, kdim), 0)
    oe_ref[0:tm, :] = o_ref[...]
    oe_ref[tm:tm + FFN_HALO, :] = jnp.where(halo_row < 8, on_ref[...], op_ref[...])
    attn = jnp.dot(oe_ref[...], wo_ref[...], preferred_element_type=F32)
    x1 = x_ref[...] + attn[0:tm]
    x1n = xn_ref[...] + attn[tm:tm + 8]
    x1p = xp_ref[...] + attn[tm + 8:tm + FFN_HALO]
    g = g_ref[...]
    h_ref[0:tm, :] = _rms(x1, g).astype(BF16)
    hn = jnp.where(pos == tiles_per_seq - 1, 0.0, _rms(x1n, g))
    hp = jnp.where(pos == 0, 0.0, _rms(x1p, g))
    h_ref[tm:tm + FFN_HALO, :] = jnp.concatenate([hn, hp], axis=0).astype(BF16)
    acc_ref[...] = x1
    slabs = FFN_CHUNK // LANES

    def chunk_cols(k):
        return pl.ds(pl.multiple_of(k * FFN_CHUNK, FFN_CHUNK), FFN_CHUNK)

    def up_proj(j, u_ref):
        h = h_ref[...]
        for half, k in enumerate((j, j + n_chunks)):
            u = jnp.dot(h, wup_ref[:, chunk_cols(k)], preferred_element_type=F32)
            for s in range(slabs):
                cols = slice(s * LANES, (s + 1) * LANES)
                slab = half * slabs + s
                u_ref[slab, 8:8 + tm, :] = u[0:tm, cols]
                u_ref[slab, 0:8, :] = u[tm + 8:tm + 16, cols]
                u_ref[slab, tm + 8:tm + 16, :] = u[tm:tm + 8, cols]

    def conv(u_ref, slab, k, cols):
        w = cw_ref[:, chunk_cols(k)]
        return (w[0:1, cols] * u_ref[slab, pl.ds(7, tm), :] + w[1:2, cols] * u_ref[slab, pl.ds(8, tm), :]
                + w[2:3, cols] * u_ref[slab, pl.ds(9, tm), :] + cb_ref[:, chunk_cols(k)][:, cols])

    def gate_down(j, u_ref):
        acts = []
        for s in range(slabs):
            cols = slice(s * LANES, (s + 1) * LANES)
            gate = conv(u_ref, s, j, cols)
            val = conv(u_ref, slabs + s, j + n_chunks, cols)
            acts.append((gate / (1.0 + jnp.exp(-gate)) * val).astype(BF16))
        acc_ref[...] += jnp.dot(jnp.concatenate(acts, axis=1), wdn_ref[chunk_cols(j), :],
                                preferred_element_type=F32)

    up_proj(0, ua_ref)

    def pair(c, carry):
        j = 2 * c
        up_proj(j + 1, ub_ref)
        gate_down(j, ua_ref)
        up_proj(j + 2, ua_ref)
        gate_down(j + 1, ub_ref)
        return carry

    assert n_chunks % 2 == 1
    lax.fori_loop(0, n_chunks // 2, pair, 0, unroll=True)
    gate_down(n_chunks - 1, ua_ref)
    y_ref[...] = acc_ref[...]


def _proj_ffn(x2, o2, w_o, gain, w_up, conv_w, conv_b, w_down, seq, tm=512):
    m = x2.shape[0]
    kdim = o2.shape[1]
    n_chunks = D_FF // FFN_CHUNK
    tiles_per_seq = seq // tm
    x_halo = tm // 8
    o_halo = tm // FFN_HALO
    const = lambda i: (0, 0)
    once = pl.Buffered(1)
    nxt = lambda blocks, per_tile: (lambda i: (jnp.minimum((i + 1) * per_tile, blocks - 1), 0))
    prv = lambda per_tile: (lambda i: (jnp.maximum(i * per_tile - 1, 0), 0))
    return pl.pallas_call(
        functools.partial(_ffn_kernel, tm=tm, tiles_per_seq=tiles_per_seq, n_chunks=n_chunks),
        out_shape=jax.ShapeDtypeStruct((m, D_MODEL), F32),
        grid=(m // tm,),
        in_specs=[pl.BlockSpec((tm, D_MODEL), lambda i: (i, 0)),
                  pl.BlockSpec((8, D_MODEL), nxt(m // 8, x_halo)),
                  pl.BlockSpec((8, D_MODEL), prv(x_halo)),
                  pl.BlockSpec((tm, kdim), lambda i: (i, 0)),
                  pl.BlockSpec((FFN_HALO, kdim), nxt(m // FFN_HALO, o_halo)),
                  pl.BlockSpec((FFN_HALO, kdim), prv(o_halo)),
                  pl.BlockSpec((kdim, D_MODEL), const, pipeline_mode=once),
                  pl.BlockSpec((1, D_MODEL), const),
                  pl.BlockSpec((D_MODEL, 2 * D_FF), const, pipeline_mode=once),
                  pl.BlockSpec((3, 2 * D_FF), const),
                  pl.BlockSpec((1, 2 * D_FF), const),
                  pl.BlockSpec((D_FF, D_MODEL), const, pipeline_mode=once)],
        out_specs=pl.BlockSpec((tm, D_MODEL), lambda i: (i, 0)),
        scratch_shapes=[pltpu.VMEM((tm + FFN_HALO, kdim), BF16),
                        pltpu.VMEM((tm + FFN_HALO, D_MODEL), BF16), pltpu.VMEM((tm, D_MODEL), F32),
                        pltpu.VMEM((2 * FFN_CHUNK // LANES, tm + FFN_HALO, LANES), F32),
                        pltpu.VMEM((2 * FFN_CHUNK // LANES, tm + FFN_HALO, LANES), F32)],
        compiler_params=_params(("arbitrary",)),
        name="proj_conv_ffn",
    )(x2, x2, x2, o2, o2, o2, w_o.astype(BF16), gain.reshape(1, D_MODEL), w_up.astype(BF16),
      conv_w.reshape(3, 2 * D_FF), conv_b.reshape(1, 2 * D_FF), w_down.astype(BF16))


def _dup_heads(w, n_heads, dim):
    w = w.reshape(D_MODEL, n_heads, 1, dim)
    return jnp.broadcast_to(w, (D_MODEL, n_heads, 2, dim)).reshape(D_MODEL, 2 * n_heads * dim)


def _mixer_a(x2, norm_gain, w_qkv, q_gain, k_gain, sink, bias, batch, seq):
    nq = A_HEADS * A_HEAD_DIM
    nk = A_KV_HEADS * A_HEAD_DIM
    w = jnp.concatenate([w_qkv[:, :nq], _dup_heads(w_qkv[:, nq:nq + nk], A_KV_HEADS, A_HEAD_DIM),
                         _dup_heads(w_qkv[:, nq + nk:], A_KV_HEADS, A_HEAD_DIM)], axis=1).astype(BF16)
    colscale = jnp.concatenate([jnp.tile(q_gain, A_HEADS) * (LOG2E * A_HEAD_DIM ** -0.5), jnp.tile(k_gain, 2 * A_KV_HEADS),
                                jnp.ones((2 * nk,), F32)])
    (qkv,) = _project(x2, norm_gain, w, colscale, [2 * D_MODEL], nq + 2 * nk, A_HEAD_DIM)
    return _attention_a(qkv, bias, sink, batch, seq).reshape(batch * seq, D_MODEL)


def _mixer_b(x2, norm_gain, w_qkv, q_gain, k_gain, lam_q1, lam_k1, lam_q2, lam_k2, sub_gain, bias,
             lambda_init, batch, seq):
    colscale = jnp.concatenate([jnp.tile(q_gain, 2 * B_HEADS) * (LOG2E * B_HEAD_DIM ** -0.5), jnp.tile(k_gain, 2 * B_HEADS),
                                jnp.ones((D_MODEL,), F32)])
    (qkv,) = _project(x2, norm_gain, w_qkv.astype(BF16), colscale, [3 * D_MODEL], 2 * D_MODEL, B_HEAD_DIM)
    lam_vecs = jnp.stack([lam_q1, lam_k1, lam_q2, lam_k2])
    return _attention_b(qkv, bias, lam_vecs, sub_gain, lambda_init, batch, seq).reshape(batch * seq, D_MODEL)


def _mixer_c(x2, norm_gain, w_qkv, q_gain, k_gain, bias, batch, seq):
    width = C_KV_HEADS * C_HEAD_DIM
    n_groups = len(C_BRANCHES)
    colscale = jnp.concatenate([jnp.tile(q_gain, n_groups * C_KV_HEADS) * (LOG2E * C_HEAD_DIM ** -0.5),
                                jnp.tile(k_gain, C_KV_HEADS), jnp.ones((width,), F32)])
    assert all(window // (2 * dil) == C_HALF for window, dil in C_BRANCHES)
    q, k, v = _project(x2, norm_gain, w_qkv.astype(BF16), colscale, [n_groups * width, width, width],
                       (n_groups + 1) * width, C_HEAD_DIM, slabs=True)
    return _attention_c(q, k, v, bias, batch, seq)


def _lambda_init(layer):
    return 0.8 - 0.6 * math.exp(-0.3 * layer)


def kernel(x, rel_bias, l0_attn_norm, l0_w_qkv, l0_q_gain, l0_k_gain, l0_sink, l0_w_o, l0_ffn_norm, l0_w_up, l0_conv_w, l0_conv_b, l0_w_down, l1_attn_norm, l1_w_qkv, l1_q_gain, l1_k_gain, l1_lambda_q1, l1_lambda_k1, l1_lambda_q2, l1_lambda_k2, l1_sub_gain, l1_w_o, l1_ffn_norm, l1_w_up, l1_conv_w, l1_conv_b, l1_w_down, l2_attn_norm, l2_w_qkv, l2_q_gain, l2_k_gain, l2_w_o, l2_ffn_norm, l2_w_up, l2_conv_w, l2_conv_b, l2_w_down, l3_attn_norm, l3_w_qkv, l3_q_gain, l3_k_gain, l3_sink, l3_w_o, l3_ffn_norm, l3_w_up, l3_conv_w, l3_conv_b, l3_w_down):
    batch, seq, _ = x.shape
    assert seq % (2 * LANES * C_BRANCHES[-1][1]) == 0
    bias_a = _bias_tiles(rel_bias, A_HEADS, [1], [LANES * d for d in range(-2, 3)], A_WINDOW)
    bias_b = _bias_tiles(rel_bias, 2 * B_HEADS, [1], [LANES * d for d in range(-B_FAR_TILES, B_FAR_TILES + 1)], None)
    bias_c = _bias_tiles(rel_bias, len(C_BRANCHES) * C_KV_HEADS, [dil for _, dil in C_BRANCHES],
                         [C_HALF * d for d in range(-2, 3)], C_HALF)
    bias_b = bias_b.reshape(B_HEADS, 2, 2 * B_FAR_TILES + 1, LANES, LANES)

    x2 = x.reshape(batch * seq, D_MODEL)
    o = _mixer_a(x2, l0_attn_norm, l0_w_qkv, l0_q_gain, l0_k_gain, l0_sink, bias_a, batch, seq)
    x2 = _proj_ffn(x2, o, l0_w_o, l0_ffn_norm, l0_w_up, l0_conv_w, l0_conv_b, l0_w_down, seq)
    o = _mixer_b(x2, l1_attn_norm, l1_w_qkv, l1_q_gain, l1_k_gain, l1_lambda_q1, l1_lambda_k1, l1_lambda_q2,
                 l1_lambda_k2, l1_sub_gain, bias_b, _lambda_init(1), batch, seq)
    x2 = _proj_ffn(x2, o, l1_w_o, l1_ffn_norm, l1_w_up, l1_conv_w, l1_conv_b, l1_w_down, seq)
    o = _mixer_c(x2, l2_attn_norm, l2_w_qkv, l2_q_gain, l2_k_gain, bias_c, batch, seq)
    x2 = _proj_ffn(x2, o, l2_w_o, l2_ffn_norm, l2_w_up, l2_conv_w, l2_conv_b, l2_w_down, seq)
    o = _mixer_a(x2, l3_attn_norm, l3_w_qkv, l3_q_gain, l3_k_gain, l3_sink, bias_a, batch, seq)
    x2 = _proj_ffn(x2, o, l3_w_o, l3_ffn_norm, l3_w_up, l3_conv_w, l3_conv_b, l3_w_down, seq)
    return x2.reshape(batch, seq, D_MODEL)
```

```python
import functools
import math

import jax
import jax.numpy as jnp
from jax import lax
from jax.experimental import pallas as pl
from jax.experimental.pallas import tpu as pltpu

F32 = jnp.float32
BF16 = jnp.bfloat16

D_MODEL = 1024
EPS = 1e-6
NEG_INF = -1e30
LOG2E = math.log2(math.e)
LANES = 128
MXU_TILE = 256
VMEM_LIMIT = 56 * 1024 * 1024

NUM_BUCKETS = 32
MAX_DISTANCE = 1024
N_BIAS_HEADS = 16

A_HEADS, A_KV_HEADS, A_HEAD_DIM, A_WINDOW = 16, 4, 64, 128
B_HEADS, B_HEAD_DIM = 8, 64
B_FAR_TILES = 6
C_BRANCHES = ((128, 1), (512, 4), (2048, 16))
C_KV_HEADS, C_HEAD_DIM = 4, 128
C_HALF = 64
D_FF = 2816
FFN_CHUNK = 256
FFN_HALO = 16


def _params(sem, vmem=VMEM_LIMIT):
    return pltpu.CompilerParams(dimension_semantics=sem, vmem_limit_bytes=vmem)


def _rms(x, gain):
    return x * lax.rsqrt(jnp.mean(x * x, axis=-1, keepdims=True) + EPS) * gain


def _bias_tile_kernel(tab_ref, dil_ref, off_ref, out_ref, *, half, heads_per_group):
    grp = pl.program_id(0)
    t = pl.program_id(1)
    dil = dil_ref[grp]
    off = off_ref[t]
    q = lax.broadcasted_iota(jnp.int32, (LANES, LANES), 0)
    c = lax.broadcasted_iota(jnp.int32, (LANES, LANES), 1)
    x = c - q + off
    rel = x * dil
    nb = NUM_BUCKETS // 2
    max_exact = nb // 2
    n = jnp.abs(rel)
    nf = jnp.maximum(n, 1).astype(F32)
    large = max_exact + (jnp.log(nf * (1.0 / max_exact)) / math.log(MAX_DISTANCE / max_exact)
                         * (nb - max_exact)).astype(jnp.int32)
    large = jnp.minimum(large, nb - 1)
    bucket = jnp.where(rel > 0, nb, 0) + jnp.where(n < max_exact, n, large)
    masks = [(bucket & (1 << bit)) != 0 for bit in range(5)]
    inside = None if half is None else jnp.abs(x) <= half
    for hh in range(heads_per_group):
        col = grp * heads_per_group + hh
        level = [tab_ref[b, col] for b in range(NUM_BUCKETS)]
        for mask in masks:
            level = [jnp.where(mask, level[2 * k + 1], level[2 * k]) for k in range(len(level) // 2)]
        bias = level[0] * LOG2E
        if inside is not None:
            bias = jnp.where(inside, bias, NEG_INF)
        out_ref[hh] = bias


def _bias_tiles(rel_bias, n_heads, group_dils, offs, half):
    n_groups, nt = len(group_dils), len(offs)
    hpg = n_heads // n_groups
    smem = pl.BlockSpec(memory_space=pltpu.SMEM)
    return pl.pallas_call(
        functools.partial(_bias_tile_kernel, half=half, heads_per_group=hpg),
        out_shape=jax.ShapeDtypeStruct((n_heads, nt, LANES, LANES), F32),
        grid=(n_groups, nt),
        in_specs=[smem, smem, smem],
        out_specs=pl.BlockSpec((hpg, None, LANES, LANES), lambda g, t: (g, t, 0, 0)),
        compiler_params=_params(("arbitrary", "arbitrary")),
        name="bias_tiles",
    )(rel_bias, jnp.asarray(group_dils, jnp.int32), jnp.asarray(offs, jnp.int32))


def _proj_kernel(x_ref, g_ref, w_ref, cs_ref, p_ref, *out_refs, widths, n_norm_cols, group, slabs):
    h = _rms(x_ref[...], g_ref[...]).astype(BF16)
    group_w = 2 * MXU_TILE
    tiles = [(o_ref, off) for o_ref, width in zip(out_refs, widths) for off in range(0, width, group_w)]

    def project(t):
        return jnp.dot(h, w_ref[:, t * group_w:(t + 1) * group_w], preferred_element_type=F32)

    def finish(t, y):
        o_ref, off = tiles[t]
        for s in range(group_w // MXU_TILE):
            col = t * group_w + s * MXU_TILE
            ys = y[:, s * MXU_TILE:(s + 1) * MXU_TILE]
            if col < n_norm_cols:
                ss = jnp.dot((ys * ys).astype(BF16), p_ref[...], preferred_element_type=F32)
                ys = ys * lax.rsqrt(ss * (1.0 / group) + EPS) * cs_ref[:, col:col + MXU_TILE]
            lo = off + s * MXU_TILE
            if slabs:
                for half in range(MXU_TILE // LANES):
                    o_ref[lo // LANES + half] = ys[:, half * LANES:(half + 1) * LANES]
            else:
                o_ref[:, lo:lo + MXU_TILE] = ys.astype(BF16)

    y = project(0)
    for t in range(len(tiles)):
        y_next = project(t + 1) if t + 1 < len(tiles) else None
        finish(t, y)
        y = y_next


def _project(x2, gain, w, colscale, widths, n_norm_cols, group, slabs=False, tm=512):
    m = x2.shape[0]
    n = w.shape[1]
    assert sum(widths) == n and m % tm == 0 and n_norm_cols % MXU_TILE == 0
    assert all(wd % (2 * MXU_TILE) == 0 for wd in widths)
    r = jnp.arange(MXU_TILE) // group
    ones_bd = (r[:, None] == r[None, :]).astype(BF16)
    const = lambda i: (0, 0)
    if slabs:
        out_shape = [jax.ShapeDtypeStruct((wd // LANES, m, LANES), F32) for wd in widths]
        out_specs = [pl.BlockSpec((wd // LANES, tm, LANES), lambda i: (0, i, 0)) for wd in widths]
    else:
        out_shape = [jax.ShapeDtypeStruct((m, wd), BF16) for wd in widths]
        out_specs = [pl.BlockSpec((tm, wd), lambda i: (i, 0)) for wd in widths]
    return pl.pallas_call(
        functools.partial(_proj_kernel, widths=tuple(widths), n_norm_cols=n_norm_cols, group=group, slabs=slabs),
        out_shape=out_shape,
        grid=(m // tm,),
        in_specs=[pl.BlockSpec((tm, D_MODEL), lambda i: (i, 0)),
                  pl.BlockSpec((1, D_MODEL), const),
                  pl.BlockSpec((D_MODEL, n), const),
                  pl.BlockSpec((1, n), const),
                  pl.BlockSpec((MXU_TILE, MXU_TILE), const)],
        out_specs=out_specs,
        compiler_params=_params(("arbitrary",)),
        name="qkv_proj",
    )(x2, gain.reshape(1, D_MODEL), w, colscale.reshape(1, n), ones_bd)


def _attn_a_kernel(sink_ref, q_ref, k_ref, v_ref, bias_ref, o_ref, *, nq, n_tiles):
    i = pl.program_id(1)
    lane = lax.broadcasted_iota(jnp.int32, (LANES, LANES), 1)
    lo = lane < A_HEAD_DIM
    group = A_HEADS // A_KV_HEADS

    def window(s):
        t = i * nq + s
        base = jnp.clip(t - 1, 0, n_tiles - 3)
        return t, base, pl.ds(pl.multiple_of(base * LANES, LANES), 3 * LANES)

    def scores(s, hk):
        rows = slice(s * LANES, (s + 1) * LANES)
        parts = []
        for c in (2 * hk, 2 * hk + 1):
            qc = q_ref[rows, c * LANES:(c + 1) * LANES]
            parts += [jnp.where(lo, qc, jnp.zeros_like(qc)), jnp.where(lo, jnp.zeros_like(qc), qc)]
        qs = jnp.concatenate(parts, axis=0)
        kw = k_ref[window(s)[2], hk * LANES:(hk + 1) * LANES]
        return lax.dot_general(qs, kw, (((1,), (1,)), ((), ())), preferred_element_type=F32)

    def finish(s, hk, sc):
        t, base, krows = window(s)
        rows = slice(s * LANES, (s + 1) * LANES)
        blocks = []
        for g in range(group):
            hq = hk * group + g
            blocks.append(jnp.concatenate(
                [sc[g * LANES:(g + 1) * LANES, w * LANES:(w + 1) * LANES] + bias_ref[hq, base + w - t + 2]
                 for w in range(3)], axis=1))
        sc = jnp.concatenate(blocks, axis=0)
        mx = jnp.max(sc, axis=-1, keepdims=True)
        p = jnp.exp2(sc - mx)
        den = jnp.sum(p, axis=-1, keepdims=True)
        pv = jnp.dot(p.astype(BF16), v_ref[krows, hk * LANES:(hk + 1) * LANES],
                     preferred_element_type=F32)
        outs = []
        for g in range(group):
            hq = hk * group + g
            gr = slice(g * LANES, (g + 1) * LANES)
            outs.append(pv[gr] / (den[gr] + jnp.exp2(sink_ref[hq] * LOG2E - mx[gr])))
        o_ref[rows, (2 * hk) * LANES:(2 * hk + 1) * LANES] = jnp.where(lo, outs[0], outs[1]).astype(BF16)
        o_ref[rows, (2 * hk + 1) * LANES:(2 * hk + 2) * LANES] = jnp.where(lo, outs[2], outs[3]).astype(BF16)

    blocks_todo = [(s, hk) for s in range(nq) for hk in range(A_KV_HEADS)]
    sc = scores(*blocks_todo[0])
    for n, blk in enumerate(blocks_todo):
        sc_next = scores(*blocks_todo[n + 1]) if n + 1 < len(blocks_todo) else None
        finish(*blk, sc)
        sc = sc_next


def _attention_a(qkv, bias, sink, batch, seq, tq=256):
    nq = tq // LANES
    n_tiles = seq // LANES
    qkv3 = qkv.reshape(batch, seq, 2 * D_MODEL)
    return pl.pallas_call(
        functools.partial(_attn_a_kernel, nq=nq, n_tiles=n_tiles),
        out_shape=jax.ShapeDtypeStruct((batch, seq, D_MODEL), BF16),
        grid=(batch, seq // tq),
        in_specs=[pl.BlockSpec(memory_space=pltpu.SMEM),
                  pl.BlockSpec((None, tq, D_MODEL), lambda b, i: (b, i, 0)),
                  pl.BlockSpec((None, seq, 512), lambda b, i: (b, 0, 2)),
                  pl.BlockSpec((None, seq, 512), lambda b, i: (b, 0, 3)),
                  pl.BlockSpec((A_HEADS, 5, LANES, LANES), lambda b, i: (0, 0, 0, 0))],
        out_specs=pl.BlockSpec((None, tq, D_MODEL), lambda b, i: (b, i, 0)),
        compiler_params=_params(("arbitrary", "arbitrary")),
        name="attn_window",
    )(sink, qkv3, qkv3, qkv3, bias)


def _attn_b_kernel(lam_ref, sg_ref, q_ref, k_ref, v_ref, bias_ref, o_ref, qs_ref, m_ref, acc_ref,
                   *, tq, tk, n_kc, lambda_init):
    i = pl.program_id(2)
    nqt = tq // LANES
    nkt = tk // LANES
    lo = lax.broadcasted_iota(jnp.int32, (tq, LANES), 1) < B_HEAD_DIM
    q = q_ref[...]
    zero = jnp.zeros_like(q)
    qs_ref[0:tq, :] = jnp.where(lo, q, zero)
    qs_ref[tq:2 * tq, :] = jnp.where(lo, zero, q)
    m_ref[...] = jnp.full(m_ref.shape, -jnp.inf, F32)
    acc_ref[...] = jnp.zeros(acc_ref.shape, F32)

    def map_chunk(mp, kc):
        mrows = slice(mp * tq, (mp + 1) * tq)
        keys = slice(kc * tk, (kc + 1) * tk)
        sc = lax.dot_general(qs_ref[mrows, :], k_ref[keys, :], (((1,), (1,)), ((), ())),
                             preferred_element_type=F32)
        probs = []
        for a in range(nqt):
            rows = slice(mp * tq + a * LANES, mp * tq + (a + 1) * LANES)
            tiles = []
            for w in range(nkt):
                d = jnp.clip(kc * nkt + w - (i * nqt + a), -B_FAR_TILES, B_FAR_TILES) + B_FAR_TILES
                tiles.append(sc[a * LANES:(a + 1) * LANES, w * LANES:(w + 1) * LANES] + bias_ref[mp, d])
            m_old = m_ref[rows, :]
            m_new = jnp.maximum(m_old, jnp.max(functools.reduce(jnp.maximum, tiles), axis=-1, keepdims=True))
            alpha = jnp.exp2(m_old - m_new)
            p = [jnp.exp2((t - m_new).astype(BF16)) for t in tiles]
            acc_ref[rows, :] = jnp.concatenate([alpha, alpha], axis=1) * acc_ref[rows, :]
            m_ref[rows, :] = m_new
            probs.append(jnp.concatenate(p, axis=1))
        vw = v_ref[keys, :]
        acc_ref[mrows, :] += jnp.dot(jnp.concatenate(probs, axis=0), jnp.concatenate([vw, jnp.ones_like(vw)], axis=1),
                                     preferred_element_type=F32)

    for kc in range(n_kc):
        for mp in range(2):
            map_chunk(mp, kc)
    lv = lam_ref[...]
    lam = (jnp.exp(jnp.sum(lv[0:1] * lv[1:2], axis=-1, keepdims=True))
           - jnp.exp(jnp.sum(lv[2:3] * lv[3:4], axis=-1, keepdims=True)) + lambda_init)
    o = acc_ref[:, 0:LANES] / acc_ref[:, LANES:2 * LANES]
    o = o[:tq] - lam * o[tq:]
    o_ref[...] = (_rms(o, sg_ref[...]) * (1.0 - lambda_init)).astype(BF16)


def _attention_b(qkv, bias, lam_vecs, sub_gain, lambda_init, batch, seq, tq=512, tk=512):
    qkv3 = qkv.reshape(batch, seq, 3 * D_MODEL)
    nb = 2 * B_FAR_TILES + 1
    return pl.pallas_call(
        functools.partial(_attn_b_kernel, tq=tq, tk=tk, n_kc=seq // tk, lambda_init=lambda_init),
        out_shape=jax.ShapeDtypeStruct((batch, seq, D_MODEL), BF16),
        grid=(batch, B_HEADS, seq // tq),
        in_specs=[pl.BlockSpec((4, B_HEAD_DIM), lambda b, h, i: (0, 0)),
                  pl.BlockSpec((1, LANES), lambda b, h, i: (0, 0)),
                  pl.BlockSpec((None, tq, LANES), lambda b, h, i: (b, i, h)),
                  pl.BlockSpec((None, seq, LANES), lambda b, h, i: (b, 0, B_HEADS + h)),
                  pl.BlockSpec((None, seq, LANES), lambda b, h, i: (b, 0, 2 * B_HEADS + h)),
                  pl.BlockSpec((None, 2, nb, LANES, LANES), lambda b, h, i: (h, 0, 0, 0, 0))],
        out_specs=pl.BlockSpec((None, tq, LANES), lambda b, h, i: (b, i, h)),
        scratch_shapes=[pltpu.VMEM((2 * tq, LANES), BF16), pltpu.VMEM((2 * tq, LANES), F32),
                        pltpu.VMEM((2 * tq, 2 * LANES), F32)],
        compiler_params=_params(("arbitrary", "arbitrary", "arbitrary")),
        name="attn_diff",
    )(lam_vecs, sub_gain.reshape(1, LANES), qkv3, qkv3, qkv3, bias)


def _attn_c_kernel(q_ref, k_ref, v_ref, bias_ref, o_ref, og_ref, lg_ref, *, seq, npos, merge_rows):
    c = pl.program_id(2)

    def rows_of(start, n, dil):
        return pl.ds(start, n) if dil == 1 else pl.ds(start, n, stride=dil)

    def window(g, r, tt):
        dil = C_BRANCHES[g][1]
        t0 = c * (npos // dil) + tt * LANES
        ws = jnp.clip(t0 - C_HALF, 0, seq // dil - 2 * LANES)
        return t0, ws, rows_of(r + dil * LANES * tt, LANES, dil), rows_of(r + dil * ws, 2 * LANES, dil)

    def scores(g, r, tt):
        _, _, qrows, krows = window(g, r, tt)
        return lax.dot_general(q_ref[g, qrows, :].astype(BF16), k_ref[krows, :].astype(BF16),
                               (((1,), (1,)), ((), ())), preferred_element_type=F32)

    def finish(g, r, tt, sc):
        t0, ws, qrows, krows = window(g, r, tt)
        sc = jnp.concatenate(
            [sc[:, w * LANES:(w + 1) * LANES] + bias_ref[g, (ws + w * LANES - t0) // C_HALF + 2]
             for w in range(2)], axis=1)
        mx = jnp.max(sc, axis=-1, keepdims=True)
        p = jnp.exp2((sc - mx).astype(BF16))
        vw = v_ref[krows, :].astype(BF16)
        pv = jnp.dot(p, jnp.concatenate([vw, jnp.ones_like(vw)], axis=1), preferred_element_type=F32)
        den = pv[:, LANES:2 * LANES]
        og_ref[g, qrows, :] = pv[:, 0:LANES] / den
        lg_ref[g, qrows, :] = mx + jnp.log2(den)

    blocks_todo = [(g, r, tt) for g, (_, dil) in enumerate(C_BRANCHES) for r in range(dil)
                   for tt in range(npos // (dil * LANES))]
    sc = scores(*blocks_todo[0])
    for n, blk in enumerate(blocks_todo):
        sc_next = scores(*blocks_todo[n + 1]) if n + 1 < len(blocks_todo) else None
        finish(*blk, sc)
        sc = sc_next
    for blk in range(npos // merge_rows):
        rows = slice(blk * merge_rows, (blk + 1) * merge_rows)
        lses = [lg_ref[g, rows, :] for g in range(len(C_BRANCHES))]
        mx = functools.reduce(jnp.maximum, lses)
        es = [jnp.exp2(l - mx) for l in lses]
        num = functools.reduce(jnp.add, [e * og_ref[g, rows, :] for g, e in enumerate(es)])
        o_ref[rows, :] = (num / functools.reduce(jnp.add, es)).astype(BF16)


def _attention_c(q, k, v, bias, batch, seq, npos=2048, merge_rows=256):
    n_groups = len(C_BRANCHES)
    m = batch * seq
    chunks = seq // npos
    q4 = q.reshape(n_groups, C_KV_HEADS, m, LANES)
    bias5 = bias.reshape(n_groups, C_KV_HEADS, 5, LANES, LANES)
    return pl.pallas_call(
        functools.partial(_attn_c_kernel, seq=seq, npos=npos, merge_rows=merge_rows),
        out_shape=jax.ShapeDtypeStruct((m, C_KV_HEADS * C_HEAD_DIM), BF16),
        grid=(batch, C_KV_HEADS, chunks),
        in_specs=[pl.BlockSpec((n_groups, None, npos, LANES), lambda b, j, c: (0, j, b * chunks + c, 0)),
                  pl.BlockSpec((None, seq, LANES), lambda b, j, c: (j, b, 0)),
                  pl.BlockSpec((None, seq, LANES), lambda b, j, c: (j, b, 0)),
                  pl.BlockSpec((n_groups, None, 5, LANES, LANES), lambda b, j, c: (0, j, 0, 0, 0))],
        out_specs=pl.BlockSpec((npos, LANES), lambda b, j, c: (b * chunks + c, j)),
        scratch_shapes=[pltpu.VMEM((n_groups, npos, LANES), F32), pltpu.VMEM((n_groups, npos, LANES), F32)],
        compiler_params=_params(("arbitrary", "arbitrary", "arbitrary")),
        name="attn_dilated",
    )(q4, k, v, bias5)


def _ffn_kernel(x_ref, xn_ref, xp_ref, o_ref, on_ref, op_ref, wo_ref, g_ref, wup_ref, cw_ref, cb_ref, wdn_ref,
                *rest, tm, tiles_per_seq, n_chunks, cast_blocks):
    n_cast = len(cast_blocks)
    cast_in, y_ref, cast_out = rest[:n_cast], rest[n_cast], rest[n_cast + 1:2 * n_cast + 1]
    oe_ref, h_ref, acc_ref, ua_ref, ub_ref = rest[2 * n_cast + 1:]
    i = pl.program_id(0)
    for src, dst, blocks in zip(cast_in, cast_out, cast_blocks):
        @pl.when(i < blocks)
        def _():
            dst[...] = src[...].astype(BF16)
    pos = i % tiles_per_seq
    kdim = o_ref.shape[1]
    halo_row = lax.broadcasted_iota(jnp.int32, (FFN_HALO, kdim), 0)
    oe_ref[0:tm, :] = o_ref[...]
    oe_ref[tm:tm + FFN_HALO, :] = jnp.where(halo_row < 8, on_ref[...], op_ref[...])
    attn = jnp.dot(oe_ref[...], wo_ref[...], preferred_element_type=F32)
    x1 = x_ref[...] + attn[0:tm]
    x1n = xn_ref[...] + attn[tm:tm + 8]
    x1p = xp_ref[...] + attn[tm + 8:tm + FFN_HALO]
    g = g_ref[...]
    h_ref[0:tm, :] = _rms(x1, g).astype(BF16)
    hn = jnp.where(pos == tiles_per_seq - 1, 0.0, _rms(x1n, g))
    hp = jnp.where(pos == 0, 0.0, _rms(x1p, g))
    h_ref[tm:tm + FFN_HALO, :] = jnp.concatenate([hn, hp], axis=0).astype(BF16)
    acc_ref[...] = x1
    slabs = FFN_CHUNK // LANES

    def chunk_cols(k):
        return pl.ds(pl.multiple_of(k * FFN_CHUNK, FFN_CHUNK), FFN_CHUNK)

    def up_proj(j, u_ref):
        h = h_ref[...]
        for half, k in enumerate((j, j + n_chunks)):
            u = jnp.dot(h, wup_ref[:, chunk_cols(k)], preferred_element_type=F32)
            for s in range(slabs):
                cols = slice(s * LANES, (s + 1) * LANES)
                slab = half * slabs + s
                u_ref[slab, 8:8 + tm, :] = u[0:tm, cols]
                u_ref[slab, 0:8, :] = u[tm + 8:tm + 16, cols]
                u_ref[slab, tm + 8:tm + 16, :] = u[tm:tm + 8, cols]

    def conv(u_ref, slab, k, cols):
        w = cw_ref[:, chunk_cols(k)]
        return (w[0:1, cols] * u_ref[slab, pl.ds(7, tm), :] + w[1:2, cols] * u_ref[slab, pl.ds(8, tm), :]
                + w[2:3, cols] * u_ref[slab, pl.ds(9, tm), :] + cb_ref[:, chunk_cols(k)][:, cols])

    def gate_down(j, u_ref):
        acts = []
        for s in range(slabs):
            cols = slice(s * LANES, (s + 1) * LANES)
            gate = conv(u_ref, s, j, cols)
            val = conv(u_ref, slabs + s, j + n_chunks, cols)
            acts.append((gate / (1.0 + jnp.exp(-gate)) * val).astype(BF16))
        acc_ref[...] += jnp.dot(jnp.concatenate(acts, axis=1), wdn_ref[chunk_cols(j), :],
                                preferred_element_type=F32)

    up_proj(0, ua_ref)

    def pair(c, carry):
        j = 2 * c
        up_proj(j + 1, ub_ref)
        gate_down(j, ua_ref)
        up_proj(j + 2, ua_ref)
        gate_down(j + 1, ub_ref)
        return carry

    assert n_chunks % 2 == 1
    lax.fori_loop(0, n_chunks // 2, pair, 0, unroll=True)
    gate_down(n_chunks - 1, ua_ref)
    y_ref[...] = acc_ref[...]


def _cast_block_rows(rows, n_steps):
    return next(rb for rb in range(16, rows + 1, 16) if rows % rb == 0 and rows // rb <= n_steps)


def _proj_ffn(x2, o2, w_o, gain, w_up, conv_w, conv_b, w_down, next_weights, seq, tm=512):
    m = x2.shape[0]
    kdim = o2.shape[1]
    n_chunks = D_FF // FFN_CHUNK
    tiles_per_seq = seq // tm
    n_steps = m // tm
    x_halo = tm // 8
    o_halo = tm // FFN_HALO
    const = lambda i: (0, 0)
    once = pl.Buffered(1)
    nxt = lambda blocks, per_tile: (lambda i: (jnp.minimum((i + 1) * per_tile, blocks - 1), 0))
    prv = lambda per_tile: (lambda i: (jnp.maximum(i * per_tile - 1, 0), 0))
    cast_rows = [_cast_block_rows(w.shape[0], n_steps) for w in next_weights]
    cast_blocks = tuple(w.shape[0] // rb for w, rb in zip(next_weights, cast_rows))
    cast_specs = [pl.BlockSpec((rb, w.shape[1]), (lambda i, nb=nb: (jnp.minimum(i, nb - 1), 0)))
                  for w, rb, nb in zip(next_weights, cast_rows, cast_blocks)]
    y, *cast = pl.pallas_call(
        functools.partial(_ffn_kernel, tm=tm, tiles_per_seq=tiles_per_seq, n_chunks=n_chunks,
                          cast_blocks=cast_blocks),
        out_shape=[jax.ShapeDtypeStruct((m, D_MODEL), F32)]
                  + [jax.ShapeDtypeStruct(w.shape, BF16) for w in next_weights],
        grid=(n_steps,),
        in_specs=[pl.BlockSpec((tm, D_MODEL), lambda i: (i, 0)),
                  pl.BlockSpec((8, D_MODEL), nxt(m // 8, x_halo)),
                  pl.BlockSpec((8, D_MODEL), prv(x_halo)),
                  pl.BlockSpec((tm, kdim), lambda i: (i, 0)),
                  pl.BlockSpec((FFN_HALO, kdim), nxt(m // FFN_HALO, o_halo)),
                  pl.BlockSpec((FFN_HALO, kdim), prv(o_halo)),
                  pl.BlockSpec((kdim, D_MODEL), const, pipeline_mode=once),
                  pl.BlockSpec((1, D_MODEL), const),
                  pl.BlockSpec((D_MODEL, 2 * D_FF), const, pipeline_mode=once),
                  pl.BlockSpec((3, 2 * D_FF), const),
                  pl.BlockSpec((1, 2 * D_FF), const),
                  pl.BlockSpec((D_FF, D_MODEL), const, pipeline_mode=once)] + cast_specs,
        out_specs=[pl.BlockSpec((tm, D_MODEL), lambda i: (i, 0))] + cast_specs,
        scratch_shapes=[pltpu.VMEM((tm + FFN_HALO, kdim), BF16),
                        pltpu.VMEM((tm + FFN_HALO, D_MODEL), BF16), pltpu.VMEM((tm, D_MODEL), F32),
                        pltpu.VMEM((2 * FFN_CHUNK // LANES, tm + FFN_HALO, LANES), F32),
                        pltpu.VMEM((2 * FFN_CHUNK // LANES, tm + FFN_HALO, LANES), F32)],
        compiler_params=_params(("arbitrary",)),
        name="proj_conv_ffn",
    )(x2, x2, x2, o2, o2, o2, w_o, gain.reshape(1, D_MODEL), w_up, conv_w.reshape(3, 2 * D_FF),
      conv_b.reshape(1, 2 * D_FF), w_down, *next_weights)
    return y, cast


def _dup_heads(w, n_heads, dim):
    w = w.reshape(D_MODEL, n_heads, 1, dim)
    return jnp.broadcast_to(w, (D_MODEL, n_heads, 2, dim)).reshape(D_MODEL, 2 * n_heads * dim)


def _a_weight_layout(w_qkv):
    nq = A_HEADS * A_HEAD_DIM
    nk = A_KV_HEADS * A_HEAD_DIM
    return jnp.concatenate([w_qkv[:, :nq], _dup_heads(w_qkv[:, nq:nq + nk], A_KV_HEADS, A_HEAD_DIM),
                            _dup_heads(w_qkv[:, nq + nk:], A_KV_HEADS, A_HEAD_DIM)], axis=1)


def _mixer_a(x2, norm_gain, w, q_gain, k_gain, sink, bias, batch, seq):
    nq = A_HEADS * A_HEAD_DIM
    nk = A_KV_HEADS * A_HEAD_DIM
    colscale = jnp.concatenate([jnp.tile(q_gain, A_HEADS) * (LOG2E * A_HEAD_DIM ** -0.5), jnp.tile(k_gain, 2 * A_KV_HEADS),
                                jnp.ones((2 * nk,), F32)])
    (qkv,) = _project(x2, norm_gain, w, colscale, [2 * D_MODEL], nq + 2 * nk, A_HEAD_DIM)
    return _attention_a(qkv, bias, sink, batch, seq).reshape(batch * seq, D_MODEL)


def _mixer_b(x2, norm_gain, w, q_gain, k_gain, lam_q1, lam_k1, lam_q2, lam_k2, sub_gain, bias,
             lambda_init, batch, seq):
    colscale = jnp.concatenate([jnp.tile(q_gain, 2 * B_HEADS) * (LOG2E * B_HEAD_DIM ** -0.5), jnp.tile(k_gain, 2 * B_HEADS),
                                jnp.ones((D_MODEL,), F32)])
    (qkv,) = _project(x2, norm_gain, w, colscale, [3 * D_MODEL], 2 * D_MODEL, B_HEAD_DIM)
    lam_vecs = jnp.stack([lam_q1, lam_k1, lam_q2, lam_k2])
    return _attention_b(qkv, bias, lam_vecs, sub_gain, lambda_init, batch, seq).reshape(batch * seq, D_MODEL)


def _mixer_c(x2, norm_gain, w, q_gain, k_gain, bias, batch, seq):
    width = C_KV_HEADS * C_HEAD_DIM
    n_groups = len(C_BRANCHES)
    colscale = jnp.concatenate([jnp.tile(q_gain, n_groups * C_KV_HEADS) * (LOG2E * C_HEAD_DIM ** -0.5),
                                jnp.tile(k_gain, C_KV_HEADS), jnp.ones((width,), F32)])
    assert all(window // (2 * dil) == C_HALF for window, dil in C_BRANCHES)
    q, k, v = _project(x2, norm_gain, w, colscale, [n_groups * width, width, width],
                       (n_groups + 1) * width, C_HEAD_DIM, slabs=True)
    return _attention_c(q, k, v, bias, batch, seq)


def _lambda_init(layer):
    return 0.8 - 0.6 * math.exp(-0.3 * layer)


def kernel(x, rel_bias, l0_attn_norm, l0_w_qkv, l0_q_gain, l0_k_gain, l0_sink, l0_w_o, l0_ffn_norm, l0_w_up, l0_conv_w, l0_conv_b, l0_w_down, l1_attn_norm, l1_w_qkv, l1_q_gain, l1_k_gain, l1_lambda_q1, l1_lambda_k1, l1_lambda_q2, l1_lambda_k2, l1_sub_gain, l1_w_o, l1_ffn_norm, l1_w_up, l1_conv_w, l1_conv_b, l1_w_down, l2_attn_norm, l2_w_qkv, l2_q_gain, l2_k_gain, l2_w_o, l2_ffn_norm, l2_w_up, l2_conv_w, l2_conv_b, l2_w_down, l3_attn_norm, l3_w_qkv, l3_q_gain, l3_k_gain, l3_sink, l3_w_o, l3_ffn_norm, l3_w_up, l3_conv_w, l3_conv_b, l3_w_down):
    batch, seq, _ = x.shape
    assert seq % (2 * LANES * C_BRANCHES[-1][1]) == 0
    bias_a = _bias_tiles(rel_bias, A_HEADS, [1], [LANES * d for d in range(-2, 3)], A_WINDOW)
    bias_b = _bias_tiles(rel_bias, 2 * B_HEADS, [1], [LANES * d for d in range(-B_FAR_TILES, B_FAR_TILES + 1)], None)
    bias_c = _bias_tiles(rel_bias, len(C_BRANCHES) * C_KV_HEADS, [dil for _, dil in C_BRANCHES],
                         [C_HALF * d for d in range(-2, 3)], C_HALF)
    bias_b = bias_b.reshape(B_HEADS, 2, 2 * B_FAR_TILES + 1, LANES, LANES)

    w_qkv, w_o, w_up, w_down = (w.astype(BF16) for w in (_a_weight_layout(l0_w_qkv), l0_w_o, l0_w_up, l0_w_down))
    x2 = x.reshape(batch * seq, D_MODEL)
    o = _mixer_a(x2, l0_attn_norm, w_qkv, l0_q_gain, l0_k_gain, l0_sink, bias_a, batch, seq)
    x2, (w_qkv, w_o, w_up, w_down) = _proj_ffn(x2, o, w_o, l0_ffn_norm, w_up, l0_conv_w, l0_conv_b, w_down,
                                               [l1_w_qkv, l1_w_o, l1_w_up, l1_w_down], seq)
    o = _mixer_b(x2, l1_attn_norm, w_qkv, l1_q_gain, l1_k_gain, l1_lambda_q1, l1_lambda_k1, l1_lambda_q2,
                 l1_lambda_k2, l1_sub_gain, bias_b, _lambda_init(1), batch, seq)
    x2, (w_qkv, w_o, w_up, w_down) = _proj_ffn(x2, o, w_o, l1_ffn_norm, w_up, l1_conv_w, l1_conv_b, w_down,
                                               [l2_w_qkv, l2_w_o, l2_w_up, l2_w_down], seq)
    o = _mixer_c(x2, l2_attn_norm, w_qkv, l2_q_gain, l2_k_gain, bias_c, batch, seq)
    x2, (w_qkv, w_o, w_up, w_down) = _proj_ffn(x2, o, w_o, l2_ffn_norm, w_up, l2_conv_w, l2_conv_b, w_down,
                                               [_a_weight_layout(l3_w_qkv), l3_w_o, l3_w_up, l3_w_down], seq)
    o = _mixer_a(x2, l3_attn_norm, w_qkv, l3_q_gain, l3_k_gain, l3_sink, bias_a, batch, seq)
    x2, _ = _proj_ffn(x2, o, w_o, l3_ffn_norm, w_up, l3_conv_w, l3_conv_b, w_down, [], seq)
    return x2.reshape(batch, seq, D_MODEL)
```

```python
import functools
import math

import jax
import jax.numpy as jnp
from jax import lax
from jax.experimental import pallas as pl
from jax.experimental.pallas import tpu as pltpu

F32 = jnp.float32
BF16 = jnp.bfloat16

D_MODEL = 1024
EPS = 1e-6
NEG_INF = -1e30
LOG2E = math.log2(math.e)
LANES = 128
MXU_TILE = 256
VMEM_LIMIT = 56 * 1024 * 1024

NUM_BUCKETS = 32
MAX_DISTANCE = 1024
N_BIAS_HEADS = 16

A_HEADS, A_KV_HEADS, A_HEAD_DIM, A_WINDOW = 16, 4, 64, 128
B_HEADS, B_HEAD_DIM = 8, 64
B_FAR_TILES = 6
C_BRANCHES = ((128, 1), (512, 4), (2048, 16))
C_KV_HEADS, C_HEAD_DIM = 4, 128
C_HALF = 64
D_FF = 2816
FFN_CHUNK = 256
FFN_HALO = 16


def _params(sem, vmem=VMEM_LIMIT):
    return pltpu.CompilerParams(dimension_semantics=sem, vmem_limit_bytes=vmem)


def _rms(x, gain):
    return x * lax.rsqrt(jnp.mean(x * x, axis=-1, keepdims=True) + EPS) * gain


def _bias_tile_kernel(tab_ref, dil_ref, off_ref, out_ref, *, half, heads_per_group):
    grp = pl.program_id(0)
    t = pl.program_id(1)
    dil = dil_ref[grp]
    off = off_ref[t]
    q = lax.broadcasted_iota(jnp.int32, (LANES, LANES), 0)
    c = lax.broadcasted_iota(jnp.int32, (LANES, LANES), 1)
    x = c - q + off
    rel = x * dil
    nb = NUM_BUCKETS // 2
    max_exact = nb // 2
    n = jnp.abs(rel)
    nf = jnp.maximum(n, 1).astype(F32)
    large = max_exact + (jnp.log(nf * (1.0 / max_exact)) / math.log(MAX_DISTANCE / max_exact)
                         * (nb - max_exact)).astype(jnp.int32)
    large = jnp.minimum(large, nb - 1)
    bucket = jnp.where(rel > 0, nb, 0) + jnp.where(n < max_exact, n, large)
    masks = [(bucket & (1 << bit)) != 0 for bit in range(5)]
    inside = None if half is None else jnp.abs(x) <= half
    for hh in range(heads_per_group):
        col = grp * heads_per_group + hh
        level = [tab_ref[b, col] for b in range(NUM_BUCKETS)]
        for mask in masks:
            level = [jnp.where(mask, level[2 * k + 1], level[2 * k]) for k in range(len(level) // 2)]
        bias = level[0] * LOG2E
        if inside is not None:
            bias = jnp.where(inside, bias, NEG_INF)
        out_ref[hh] = bias


def _bias_tiles(rel_bias, n_heads, group_dils, offs, half):
    n_groups, nt = len(group_dils), len(offs)
    hpg = n_heads // n_groups
    smem = pl.BlockSpec(memory_space=pltpu.SMEM)
    return pl.pallas_call(
        functools.partial(_bias_tile_kernel, half=half, heads_per_group=hpg),
        out_shape=jax.ShapeDtypeStruct((n_heads, nt, LANES, LANES), F32),
        grid=(n_groups, nt),
        in_specs=[smem, smem, smem],
        out_specs=pl.BlockSpec((hpg, None, LANES, LANES), lambda g, t: (g, t, 0, 0)),
        compiler_params=_params(("arbitrary", "arbitrary")),
        name="bias_tiles",
    )(rel_bias, jnp.asarray(group_dils, jnp.int32), jnp.asarray(offs, jnp.int32))


def _proj_kernel(x_ref, g_ref, w_ref, cs_ref, p_ref, *out_refs, widths, n_norm_cols, group, slabs):
    h = _rms(x_ref[...], g_ref[...]).astype(BF16)
    group_w = 2 * MXU_TILE
    tiles = [(o_ref, off) for o_ref, width in zip(out_refs, widths) for off in range(0, width, group_w)]

    def project(t):
        return jnp.dot(h, w_ref[:, t * group_w:(t + 1) * group_w], preferred_element_type=F32)

    def finish(t, y):
        o_ref, off = tiles[t]
        for s in range(group_w // MXU_TILE):
            col = t * group_w + s * MXU_TILE
            ys = y[:, s * MXU_TILE:(s + 1) * MXU_TILE]
            if col < n_norm_cols:
                ss = jnp.dot((ys * ys).astype(BF16), p_ref[...], preferred_element_type=F32)
                ys = ys * lax.rsqrt(ss * (1.0 / group) + EPS) * cs_ref[:, col:col + MXU_TILE]
            lo = off + s * MXU_TILE
            if slabs:
                for half in range(MXU_TILE // LANES):
                    o_ref[lo // LANES + half] = ys[:, half * LANES:(half + 1) * LANES]
            else:
                o_ref[:, lo:lo + MXU_TILE] = ys.astype(BF16)

    y = project(0)
    for t in range(len(tiles)):
        y_next = project(t + 1) if t + 1 < len(tiles) else None
        finish(t, y)
        y = y_next


def _project(x2, gain, w, colscale, widths, n_norm_cols, group, slabs=False, tm=512):
    m = x2.shape[0]
    n = w.shape[1]
    assert sum(widths) == n and m % tm == 0 and n_norm_cols % MXU_TILE == 0
    assert all(wd % (2 * MXU_TILE) == 0 for wd in widths)
    r = jnp.arange(MXU_TILE) // group
    ones_bd = (r[:, None] == r[None, :]).astype(BF16)
    const = lambda i: (0, 0)
    if slabs:
        out_shape = [jax.ShapeDtypeStruct((wd // LANES, m, LANES), F32) for wd in widths]
        out_specs = [pl.BlockSpec((wd // LANES, tm, LANES), lambda i: (0, i, 0)) for wd in widths]
    else:
        out_shape = [jax.ShapeDtypeStruct((m, wd), BF16) for wd in widths]
        out_specs = [pl.BlockSpec((tm, wd), lambda i: (i, 0)) for wd in widths]
    return pl.pallas_call(
        functools.partial(_proj_kernel, widths=tuple(widths), n_norm_cols=n_norm_cols, group=group, slabs=slabs),
        out_shape=out_shape,
        grid=(m // tm,),
        in_specs=[pl.BlockSpec((tm, D_MODEL), lambda i: (i, 0)),
                  pl.BlockSpec((1, D_MODEL), const),
                  pl.BlockSpec((D_MODEL, n), const),
                  pl.BlockSpec((1, n), const),
                  pl.BlockSpec((MXU_TILE, MXU_TILE), const)],
        out_specs=out_specs,
        compiler_params=_params(("arbitrary",)),
        name="qkv_proj",
    )(x2, gain.reshape(1, D_MODEL), w, colscale.reshape(1, n), ones_bd)


def _attn_a_kernel(sink_ref, q_ref, k_ref, v_ref, bias_ref, o_ref, *, nq, n_tiles):
    i = pl.program_id(1)
    lane = lax.broadcasted_iota(jnp.int32, (LANES, LANES), 1)
    lo = lane < A_HEAD_DIM
    group = A_HEADS // A_KV_HEADS

    def window(s):
        t = i * nq + s
        base = jnp.clip(t - 1, 0, n_tiles - 3)
        return t, base, pl.ds(pl.multiple_of(base * LANES, LANES), 3 * LANES)

    def scores(s, hk):
        rows = slice(s * LANES, (s + 1) * LANES)
        parts = []
        for c in (2 * hk, 2 * hk + 1):
            qc = q_ref[rows, c * LANES:(c + 1) * LANES]
            parts += [jnp.where(lo, qc, jnp.zeros_like(qc)), jnp.where(lo, jnp.zeros_like(qc), qc)]
        qs = jnp.concatenate(parts, axis=0)
        kw = k_ref[window(s)[2], hk * LANES:(hk + 1) * LANES]
        return lax.dot_general(qs, kw, (((1,), (1,)), ((), ())), preferred_element_type=F32)

    def finish(s, hk, sc):
        t, base, krows = window(s)
        rows = slice(s * LANES, (s + 1) * LANES)
        blocks = []
        for g in range(group):
            hq = hk * group + g
            blocks.append(jnp.concatenate(
                [sc[g * LANES:(g + 1) * LANES, w * LANES:(w + 1) * LANES] + bias_ref[hq, base + w - t + 2]
                 for w in range(3)], axis=1))
        sc = jnp.concatenate(blocks, axis=0)
        mx = jnp.max(sc, axis=-1, keepdims=True)
        p = jnp.exp2(sc - mx)
        den = jnp.sum(p, axis=-1, keepdims=True)
        pv = jnp.dot(p.astype(BF16), v_ref[krows, hk * LANES:(hk + 1) * LANES],
                     preferred_element_type=F32)
        outs = []
        for g in range(group):
            hq = hk * group + g
            gr = slice(g * LANES, (g + 1) * LANES)
            outs.append(pv[gr] / (den[gr] + jnp.exp2(sink_ref[hq] * LOG2E - mx[gr])))
        o_ref[rows, (2 * hk) * LANES:(2 * hk + 1) * LANES] = jnp.where(lo, outs[0], outs[1]).astype(BF16)
        o_ref[rows, (2 * hk + 1) * LANES:(2 * hk + 2) * LANES] = jnp.where(lo, outs[2], outs[3]).astype(BF16)

    blocks_todo = [(s, hk) for s in range(nq) for hk in range(A_KV_HEADS)]
    sc = scores(*blocks_todo[0])
    for n, blk in enumerate(blocks_todo):
        sc_next = scores(*blocks_todo[n + 1]) if n + 1 < len(blocks_todo) else None
        finish(*blk, sc)
        sc = sc_next


def _attention_a(qkv, bias, sink, batch, seq, tq=256):
    nq = tq // LANES
    n_tiles = seq // LANES
    qkv3 = qkv.reshape(batch, seq, 2 * D_MODEL)
    return pl.pallas_call(
        functools.partial(_attn_a_kernel, nq=nq, n_tiles=n_tiles),
        out_shape=jax.ShapeDtypeStruct((batch, seq, D_MODEL), BF16),
        grid=(batch, seq // tq),
        in_specs=[pl.BlockSpec(memory_space=pltpu.SMEM),
                  pl.BlockSpec((None, tq, D_MODEL), lambda b, i: (b, i, 0)),
                  pl.BlockSpec((None, seq, 512), lambda b, i: (b, 0, 2)),
                  pl.BlockSpec((None, seq, 512), lambda b, i: (b, 0, 3)),
                  pl.BlockSpec((A_HEADS, 5, LANES, LANES), lambda b, i: (0, 0, 0, 0))],
        out_specs=pl.BlockSpec((None, tq, D_MODEL), lambda b, i: (b, i, 0)),
        compiler_params=_params(("arbitrary", "arbitrary")),
        name="attn_window",
    )(sink, qkv3, qkv3, qkv3, bias)


def _attn_b_kernel(lam_ref, sg_ref, q_ref, k_ref, v_ref, bias_ref, o_ref, qs_ref, m_ref, acc_ref,
                   *, tq, tk, n_kc, lambda_init):
    i = pl.program_id(2)
    nqt = tq // LANES
    nkt = tk // LANES
    lo = lax.broadcasted_iota(jnp.int32, (tq, LANES), 1) < B_HEAD_DIM
    q = q_ref[...]
    zero = jnp.zeros_like(q)
    qs_ref[0:tq, :] = jnp.where(lo, q, zero)
    qs_ref[tq:2 * tq, :] = jnp.where(lo, zero, q)
    m_ref[...] = jnp.full(m_ref.shape, -jnp.inf, F32)
    acc_ref[...] = jnp.zeros(acc_ref.shape, F32)

    def map_chunk(mp, kc):
        mrows = slice(mp * tq, (mp + 1) * tq)
        keys = slice(kc * tk, (kc + 1) * tk)
        sc = lax.dot_general(qs_ref[mrows, :], k_ref[keys, :], (((1,), (1,)), ((), ())),
                             preferred_element_type=F32)
        probs = []
        for a in range(nqt):
            rows = slice(mp * tq + a * LANES, mp * tq + (a + 1) * LANES)
            tiles = []
            for w in range(nkt):
                d = jnp.clip(kc * nkt + w - (i * nqt + a), -B_FAR_TILES, B_FAR_TILES) + B_FAR_TILES
                tiles.append(sc[a * LANES:(a + 1) * LANES, w * LANES:(w + 1) * LANES] + bias_ref[mp, d])
            m_old = m_ref[rows, :]
            m_new = jnp.maximum(m_old, jnp.max(functools.reduce(jnp.maximum, tiles), axis=-1, keepdims=True))
            alpha = jnp.exp2(m_old - m_new)
            p = [jnp.exp2((t - m_new).astype(BF16)) for t in tiles]
            acc_ref[rows, :] = jnp.concatenate([alpha, alpha], axis=1) * acc_ref[rows, :]
            m_ref[rows, :] = m_new
            probs.append(jnp.concatenate(p, axis=1))
        vw = v_ref[keys, :]
        acc_ref[mrows, :] += jnp.dot(jnp.concatenate(probs, axis=0), jnp.concatenate([vw, jnp.ones_like(vw)], axis=1),
                                     preferred_element_type=F32)

    for kc in range(n_kc):
        for mp in range(2):
            map_chunk(mp, kc)
    lv = lam_ref[...]
    lam = (jnp.exp(jnp.sum(lv[0:1] * lv[1:2], axis=-1, keepdims=True))
           - jnp.exp(jnp.sum(lv[2:3] * lv[3:4], axis=-1, keepdims=True)) + lambda_init)
    o = acc_ref[:, 0:LANES] / acc_ref[:, LANES:2 * LANES]
    o = o[:tq] - lam * o[tq:]
    o_ref[...] = (_rms(o, sg_ref[...]) * (1.0 - lambda_init)).astype(BF16)


def _attention_b(qkv, bias, lam_vecs, sub_gain, lambda_init, batch, seq, tq=1024, tk=256):
    qkv3 = qkv.reshape(batch, seq, 3 * D_MODEL)
    nb = 2 * B_FAR_TILES + 1
    return pl.pallas_call(
        functools.partial(_attn_b_kernel, tq=tq, tk=tk, n_kc=seq // tk, lambda_init=lambda_init),
        out_shape=jax.ShapeDtypeStruct((batch, seq, D_MODEL), BF16),
        grid=(batch, B_HEADS, seq // tq),
        in_specs=[pl.BlockSpec((4, B_HEAD_DIM), lambda b, h, i: (0, 0)),
                  pl.BlockSpec((1, LANES), lambda b, h, i: (0, 0)),
                  pl.BlockSpec((None, tq, LANES), lambda b, h, i: (b, i, h)),
                  pl.BlockSpec((None, seq, LANES), lambda b, h, i: (b, 0, B_HEADS + h)),
                  pl.BlockSpec((None, seq, LANES), lambda b, h, i: (b, 0, 2 * B_HEADS + h)),
                  pl.BlockSpec((None, 2, nb, LANES, LANES), lambda b, h, i: (h, 0, 0, 0, 0))],
        out_specs=pl.BlockSpec((None, tq, LANES), lambda b, h, i: (b, i, h)),
        scratch_shapes=[pltpu.VMEM((2 * tq, LANES), BF16), pltpu.VMEM((2 * tq, LANES), F32),
                        pltpu.VMEM((2 * tq, 2 * LANES), F32)],
        compiler_params=_params(("arbitrary", "arbitrary", "arbitrary")),
        name="attn_diff",
    )(lam_vecs, sub_gain.reshape(1, LANES), qkv3, qkv3, qkv3, bias)


def _attn_c_kernel(q_ref, k_ref, v_ref, bias_ref, o_ref, og_ref, lg_ref, *, seq, npos, merge_rows):
    c = pl.program_id(2)

    def rows_of(start, n, dil):
        return pl.ds(start, n) if dil == 1 else pl.ds(start, n, stride=dil)

    def window(g, r, tt):
        dil = C_BRANCHES[g][1]
        t0 = c * (npos // dil) + tt * LANES
        ws = jnp.clip(t0 - C_HALF, 0, seq // dil - 2 * LANES)
        return t0, ws, rows_of(r + dil * LANES * tt, LANES, dil), rows_of(r + dil * ws, 2 * LANES, dil)

    def scores(g, r, tt):
        _, _, qrows, krows = window(g, r, tt)
        return lax.dot_general(q_ref[g, qrows, :].astype(BF16), k_ref[krows, :].astype(BF16),
                               (((1,), (1,)), ((), ())), preferred_element_type=F32)

    def finish(g, r, tt, sc):
        t0, ws, qrows, krows = window(g, r, tt)
        sc = jnp.concatenate(
            [sc[:, w * LANES:(w + 1) * LANES] + bias_ref[g, (ws + w * LANES - t0) // C_HALF + 2]
             for w in range(2)], axis=1)
        mx = jnp.max(sc, axis=-1, keepdims=True)
        p = jnp.exp2((sc - mx).astype(BF16))
        vw = v_ref[krows, :].astype(BF16)
        pv = jnp.dot(p, jnp.concatenate([vw, jnp.ones_like(vw)], axis=1), preferred_element_type=F32)
        den = pv[:, LANES:2 * LANES]
        og_ref[g, qrows, :] = pv[:, 0:LANES] / den
        lg_ref[g, qrows, :] = mx + jnp.log2(den)

    blocks_todo = [(g, r, tt) for g, (_, dil) in enumerate(C_BRANCHES) for r in range(dil)
                   for tt in range(npos // (dil * LANES))]
    sc = scores(*blocks_todo[0])
    for n, blk in enumerate(blocks_todo):
        sc_next = scores(*blocks_todo[n + 1]) if n + 1 < len(blocks_todo) else None
        finish(*blk, sc)
        sc = sc_next
    for blk in range(npos // merge_rows):
        rows = slice(blk * merge_rows, (blk + 1) * merge_rows)
        lses = [lg_ref[g, rows, :] for g in range(len(C_BRANCHES))]
        mx = functools.reduce(jnp.maximum, lses)
        es = [jnp.exp2(l - mx) for l in lses]
        num = functools.reduce(jnp.add, [e * og_ref[g, rows, :] for g, e in enumerate(es)])
        o_ref[rows, :] = (num / functools.reduce(jnp.add, es)).astype(BF16)


def _attention_c(q, k, v, bias, batch, seq, npos=2048, merge_rows=256):
    n_groups = len(C_BRANCHES)
    m = batch * seq
    chunks = seq // npos
    q4 = q.reshape(n_groups, C_KV_HEADS, m, LANES)
    bias5 = bias.reshape(n_groups, C_KV_HEADS, 5, LANES, LANES)
    return pl.pallas_call(
        functools.partial(_attn_c_kernel, seq=seq, npos=npos, merge_rows=merge_rows),
        out_shape=jax.ShapeDtypeStruct((m, C_KV_HEADS * C_HEAD_DIM), BF16),
        grid=(batch, C_KV_HEADS, chunks),
        in_specs=[pl.BlockSpec((n_groups, None, npos, LANES), lambda b, j, c: (0, j, b * chunks + c, 0)),
                  pl.BlockSpec((None, seq, LANES), lambda b, j, c: (j, b, 0)),
                  pl.BlockSpec((None, seq, LANES), lambda b, j, c: (j, b, 0)),
                  pl.BlockSpec((n_groups, None, 5, LANES, LANES), lambda b, j, c: (0, j, 0, 0, 0))],
        out_specs=pl.BlockSpec((npos, LANES), lambda b, j, c: (b * chunks + c, j)),
        scratch_shapes=[pltpu.VMEM((n_groups, npos, LANES), F32), pltpu.VMEM((n_groups, npos, LANES), F32)],
        compiler_params=_params(("arbitrary", "arbitrary", "arbitrary")),
        name="attn_dilated",
    )(q4, k, v, bias5)


def _ffn_kernel(x_ref, xn_ref, xp_ref, o_ref, on_ref, op_ref, wo_ref, g_ref, wup_ref, cw_ref, cb_ref, wdn_ref,
                *rest, tm, tiles_per_seq, n_chunks, cast_blocks):
    n_cast = len(cast_blocks)
    cast_in, y_ref, cast_out = rest[:n_cast], rest[n_cast], rest[n_cast + 1:2 * n_cast + 1]
    oe_ref, h_ref, acc_ref, ua_ref, ub_ref = rest[2 * n_cast + 1:]
    i = pl.program_id(0)
    for src, dst, blocks in zip(cast_in, cast_out, cast_blocks):
        @pl.when(i < blocks)
        def _():
            dst[...] = src[...].astype(BF16)
    pos = i % tiles_per_seq
    kdim = o_ref.shape[1]
    halo_row = lax.broadcasted_iota(jnp.int32, (FFN_HALO, kdim), 0)
    oe_ref[0:tm, :] = o_ref[...]
    oe_ref[tm:tm + FFN_HALO, :] = jnp.where(halo_row < 8, on_ref[...], op_ref[...])
    attn = jnp.dot(oe_ref[...], wo_ref[...], preferred_element_type=F32)
    x1 = x_ref[...] + attn[0:tm]
    x1n = xn_ref[...] + attn[tm:tm + 8]
    x1p = xp_ref[...] + attn[tm + 8:tm + FFN_HALO]
    g = g_ref[...]
    h_ref[0:tm, :] = _rms(x1, g).astype(BF16)
    hn = jnp.where(pos == tiles_per_seq - 1, 0.0, _rms(x1n, g))
    hp = jnp.where(pos == 0, 0.0, _rms(x1p, g))
    h_ref[tm:tm + FFN_HALO, :] = jnp.concatenate([hn, hp], axis=0).astype(BF16)
    acc_ref[...] = x1
    slabs = FFN_CHUNK // LANES

    def chunk_cols(k):
        return pl.ds(pl.multiple_of(k * FFN_CHUNK, FFN_CHUNK), FFN_CHUNK)

    def up_proj(j, u_ref):
        h = h_ref[...]
        for half, k in enumerate((j, j + n_chunks)):
            u = jnp.dot(h, wup_ref[:, chunk_cols(k)], preferred_element_type=F32)
            for s in range(slabs):
                cols = slice(s * LANES, (s + 1) * LANES)
                slab = half * slabs + s
                u_ref[slab, 8:8 + tm, :] = u[0:tm, cols]
                u_ref[slab, 0:8, :] = u[tm + 8:tm + 16, cols]
                u_ref[slab, tm + 8:tm + 16, :] = u[tm:tm + 8, cols]

    def conv(u_ref, slab, k, cols):
        w = cw_ref[:, chunk_cols(k)]
        return (w[0:1, cols] * u_ref[slab, pl.ds(7, tm), :] + w[1:2, cols] * u_ref[slab, pl.ds(8, tm), :]
                + w[2:3, cols] * u_ref[slab, pl.ds(9, tm), :] + cb_ref[:, chunk_cols(k)][:, cols])

    def gate_down(j, u_ref):
        acts = []
        for s in range(slabs):
            cols = slice(s * LANES, (s + 1) * LANES)
            gate = conv(u_ref, s, j, cols)
            val = conv(u_ref, slabs + s, j + n_chunks, cols)
            acts.append((gate / (1.0 + jnp.exp(-gate)) * val).astype(BF16))
        acc_ref[...] += jnp.dot(jnp.concatenate(acts, axis=1), wdn_ref[chunk_cols(j), :],
                                preferred_element_type=F32)

    up_proj(0, ua_ref)

    def pair(c, carry):
        j = 2 * c
        up_proj(j + 1, ub_ref)
        gate_down(j, ua_ref)
        up_proj(j + 2, ua_ref)
        gate_down(j + 1, ub_ref)
        return carry

    assert n_chunks % 2 == 1
    lax.fori_loop(0, n_chunks // 2, pair, 0, unroll=True)
    gate_down(n_chunks - 1, ua_ref)
    y_ref[...] = acc_ref[...]


def _cast_block_rows(rows, n_steps):
    return next(rb for rb in range(16, rows + 1, 16) if rows % rb == 0 and rows // rb <= n_steps)


def _proj_ffn(x2, o2, w_o, gain, w_up, conv_w, conv_b, w_down, next_weights, seq, tm=512):
    m = x2.shape[0]
    kdim = o2.shape[1]
    n_chunks = D_FF // FFN_CHUNK
    tiles_per_seq = seq // tm
    n_steps = m // tm
    x_halo = tm // 8
    o_halo = tm // FFN_HALO
    const = lambda i: (0, 0)
    once = pl.Buffered(1)
    nxt = lambda blocks, per_tile: (lambda i: (jnp.minimum((i + 1) * per_tile, blocks - 1), 0))
    prv = lambda per_tile: (lambda i: (jnp.maximum(i * per_tile - 1, 0), 0))
    cast_rows = [_cast_block_rows(w.shape[0], n_steps) for w in next_weights]
    cast_blocks = tuple(w.shape[0] // rb for w, rb in zip(next_weights, cast_rows))
    cast_specs = [pl.BlockSpec((rb, w.shape[1]), (lambda i, nb=nb: (jnp.minimum(i, nb - 1), 0)))
                  for w, rb, nb in zip(next_weights, cast_rows, cast_blocks)]
    y, *cast = pl.pallas_call(
        functools.partial(_ffn_kernel, tm=tm, tiles_per_seq=tiles_per_seq, n_chunks=n_chunks,
                          cast_blocks=cast_blocks),
        out_shape=[jax.ShapeDtypeStruct((m, D_MODEL), F32)]
                  + [jax.ShapeDtypeStruct(w.shape, BF16) for w in next_weights],
        grid=(n_steps,),
        in_specs=[pl.BlockSpec((tm, D_MODEL), lambda i: (i, 0)),
                  pl.BlockSpec((8, D_MODEL), nxt(m // 8, x_halo)),
                  pl.BlockSpec((8, D_MODEL), prv(x_halo)),
                  pl.BlockSpec((tm, kdim), lambda i: (i, 0)),
                  pl.BlockSpec((FFN_HALO, kdim), nxt(m // FFN_HALO, o_halo)),
                  pl.BlockSpec((FFN_HALO, kdim), prv(o_halo)),
                  pl.BlockSpec((kdim, D_MODEL), const, pipeline_mode=once),
                  pl.BlockSpec((1, D_MODEL), const),
                  pl.BlockSpec((D_MODEL, 2 * D_FF), const, pipeline_mode=once),
                  pl.BlockSpec((3, 2 * D_FF), const),
                  pl.BlockSpec((1, 2 * D_FF), const),
                  pl.BlockSpec((D_FF, D_MODEL), const, pipeline_mode=once)] + cast_specs,
        out_specs=[pl.BlockSpec((tm, D_MODEL), lambda i: (i, 0))] + cast_specs,
        scratch_shapes=[pltpu.VMEM((tm + FFN_HALO, kdim), BF16),
                        pltpu.VMEM((tm + FFN_HALO, D_MODEL), BF16), pltpu.VMEM((tm, D_MODEL), F32),
                        pltpu.VMEM((2 * FFN_CHUNK // LANES, tm + FFN_HALO, LANES), F32),
                        pltpu.VMEM((2 * FFN_CHUNK // LANES, tm + FFN_HALO, LANES), F32)],
        compiler_params=_params(("arbitrary",)),
        name="proj_conv_ffn",
    )(x2, x2, x2, o2, o2, o2, w_o, gain.reshape(1, D_MODEL), w_up, conv_w.reshape(3, 2 * D_FF),
      conv_b.reshape(1, 2 * D_FF), w_down, *next_weights)
    return y, cast


def _dup_heads(w, n_heads, dim):
    w = w.reshape(D_MODEL, n_heads, 1, dim)
    return jnp.broadcast_to(w, (D_MODEL, n_heads, 2, dim)).reshape(D_MODEL, 2 * n_heads * dim)


def _a_weight_layout(w_qkv):
    nq = A_HEADS * A_HEAD_DIM
    nk = A_KV_HEADS * A_HEAD_DIM
    return jnp.concatenate([w_qkv[:, :nq], _dup_heads(w_qkv[:, nq:nq + nk], A_KV_HEADS, A_HEAD_DIM),
                            _dup_heads(w_qkv[:, nq + nk:], A_KV_HEADS, A_HEAD_DIM)], axis=1)


def _mixer_a(x2, norm_gain, w, q_gain, k_gain, sink, bias, batch, seq):
    nq = A_HEADS * A_HEAD_DIM
    nk = A_KV_HEADS * A_HEAD_DIM
    colscale = jnp.concatenate([jnp.tile(q_gain, A_HEADS) * (LOG2E * A_HEAD_DIM ** -0.5), jnp.tile(k_gain, 2 * A_KV_HEADS),
                                jnp.ones((2 * nk,), F32)])
    (qkv,) = _project(x2, norm_gain, w, colscale, [2 * D_MODEL], nq + 2 * nk, A_HEAD_DIM)
    return _attention_a(qkv, bias, sink, batch, seq).reshape(batch * seq, D_MODEL)


def _mixer_b(x2, norm_gain, w, q_gain, k_gain, lam_q1, lam_k1, lam_q2, lam_k2, sub_gain, bias,
             lambda_init, batch, seq):
    colscale = jnp.concatenate([jnp.tile(q_gain, 2 * B_HEADS) * (LOG2E * B_HEAD_DIM ** -0.5), jnp.tile(k_gain, 2 * B_HEADS),
                                jnp.ones((D_MODEL,), F32)])
    (qkv,) = _project(x2, norm_gain, w, colscale, [3 * D_MODEL], 2 * D_MODEL, B_HEAD_DIM)
    lam_vecs = jnp.stack([lam_q1, lam_k1, lam_q2, lam_k2])
    return _attention_b(qkv, bias, lam_vecs, sub_gain, lambda_init, batch, seq).reshape(batch * seq, D_MODEL)


def _mixer_c(x2, norm_gain, w, q_gain, k_gain, bias, batch, seq):
    width = C_KV_HEADS * C_HEAD_DIM
    n_groups = len(C_BRANCHES)
    colscale = jnp.concatenate([jnp.tile(q_gain, n_groups * C_KV_HEADS) * (LOG2E * C_HEAD_DIM ** -0.5),
                                jnp.tile(k_gain, C_KV_HEADS), jnp.ones((width,), F32)])
    assert all(window // (2 * dil) == C_HALF for window, dil in C_BRANCHES)
    q, k, v = _project(x2, norm_gain, w, colscale, [n_groups * width, width, width],
                       (n_groups + 1) * width, C_HEAD_DIM, slabs=True)
    return _attention_c(q, k, v, bias, batch, seq)


def _lambda_init(layer):
    return 0.8 - 0.6 * math.exp(-0.3 * layer)


def kernel(x, rel_bias, l0_attn_norm, l0_w_qkv, l0_q_gain, l0_k_gain, l0_sink, l0_w_o, l0_ffn_norm, l0_w_up, l0_conv_w, l0_conv_b, l0_w_down, l1_attn_norm, l1_w_qkv, l1_q_gain, l1_k_gain, l1_lambda_q1, l1_lambda_k1, l1_lambda_q2, l1_lambda_k2, l1_sub_gain, l1_w_o, l1_ffn_norm, l1_w_up, l1_conv_w, l1_conv_b, l1_w_down, l2_attn_norm, l2_w_qkv, l2_q_gain, l2_k_gain, l2_w_o, l2_ffn_norm, l2_w_up, l2_conv_w, l2_conv_b, l2_w_down, l3_attn_norm, l3_w_qkv, l3_q_gain, l3_k_gain, l3_sink, l3_w_o, l3_ffn_norm, l3_w_up, l3_conv_w, l3_conv_b, l3_w_down):
    batch, seq, _ = x.shape
    assert seq % (2 * LANES * C_BRANCHES[-1][1]) == 0
    bias_a = _bias_tiles(rel_bias, A_HEADS, [1], [LANES * d for d in range(-2, 3)], A_WINDOW)
    bias_b = _bias_tiles(rel_bias, 2 * B_HEADS, [1], [LANES * d for d in range(-B_FAR_TILES, B_FAR_TILES + 1)], None)
    bias_c = _bias_tiles(rel_bias, len(C_BRANCHES) * C_KV_HEADS, [dil for _, dil in C_BRANCHES],
                         [C_HALF * d for d in range(-2, 3)], C_HALF)
    bias_b = bias_b.reshape(B_HEADS, 2, 2 * B_FAR_TILES + 1, LANES, LANES)

    w_qkv, w_o, w_up, w_down = (w.astype(BF16) for w in (_a_weight_layout(l0_w_qkv), l0_w_o, l0_w_up, l0_w_down))
    x2 = x.reshape(batch * seq, D_MODEL)
    o = _mixer_a(x2, l0_attn_norm, w_qkv, l0_q_gain, l0_k_gain, l0_sink, bias_a, batch, seq)
    x2, (w_qkv, w_o, w_up, w_down) = _proj_ffn(x2, o, w_o, l0_ffn_norm, w_up, l0_conv_w, l0_conv_b, w_down,
                                               [l1_w_qkv, l1_w_o, l1_w_up, l1_w_down], seq)
    o = _mixer_b(x2, l1_attn_norm, w_qkv, l1_q_gain, l1_k_gain, l1_lambda_q1, l1_lambda_k1, l1_lambda_q2,
                 l1_lambda_k2, l1_sub_gain, bias_b, _lambda_init(1), batch, seq)
    x2, (w_qkv, w_o, w_up, w_down) = _proj_ffn(x2, o, w_o, l1_ffn_norm, w_up, l1_conv_w, l1_conv_b, w_down,
                                               [l2_w_qkv, l2_w_o, l2_w_up, l2_w_down], seq)
    o = _mixer_c(x2, l2_attn_norm, w_qkv, l2_q_gain, l2_k_gain, bias_c, batch, seq)
    x2, (w_qkv, w_o, w_up, w_down) = _proj_ffn(x2, o, w_o, l2_ffn_norm, w_up, l2_conv_w, l2_conv_b, w_down,
                                               [_a_weight_layout(l3_w_qkv), l3_w_o, l3_w_up, l3_w_down], seq)
    o = _mixer_a(x2, l3_attn_norm, w_qkv, l3_q_gain, l3_k_gain, l3_sink, bias_a, batch, seq)
    x2, _ = _proj_ffn(x2, o, w_o, l3_ffn_norm, w_up, l3_conv_w, l3_conv_b, w_down, [], seq)
    return x2.reshape(batch, seq, D_MODEL)
```

```python
import functools
import math

import jax
import jax.numpy as jnp
from jax import lax
from jax.experimental import pallas as pl
from jax.experimental.pallas import tpu as pltpu

F32 = jnp.float32
BF16 = jnp.bfloat16

D_MODEL = 1024
EPS = 1e-6
NEG_INF = -1e30
LOG2E = math.log2(math.e)
LANES = 128
MXU_TILE = 256
VMEM_LIMIT = 56 * 1024 * 1024

NUM_BUCKETS = 32
MAX_DISTANCE = 1024
N_BIAS_HEADS = 16

A_HEADS, A_KV_HEADS, A_HEAD_DIM, A_WINDOW = 16, 4, 64, 128
B_HEADS, B_HEAD_DIM = 8, 64
B_FAR_TILES = 6
C_BRANCHES = ((128, 1), (512, 4), (2048, 16))
C_KV_HEADS, C_HEAD_DIM = 4, 128
C_HALF = 64
D_FF = 2816
FFN_CHUNK = 256
FFN_HALO = 16


def _params(sem, vmem=VMEM_LIMIT):
    return pltpu.CompilerParams(dimension_semantics=sem, vmem_limit_bytes=vmem)


def _rms(x, gain):
    return x * lax.rsqrt(jnp.mean(x * x, axis=-1, keepdims=True) + EPS) * gain


def _bias_tile_kernel(tab_ref, dil_ref, off_ref, out_ref, *, half, heads_per_group):
    grp = pl.program_id(0)
    t = pl.program_id(1)
    dil = dil_ref[grp]
    off = off_ref[t]
    q = lax.broadcasted_iota(jnp.int32, (LANES, LANES), 0)
    c = lax.broadcasted_iota(jnp.int32, (LANES, LANES), 1)
    x = c - q + off
    rel = x * dil
    nb = NUM_BUCKETS // 2
    max_exact = nb // 2
    n = jnp.abs(rel)
    nf = jnp.maximum(n, 1).astype(F32)
    large = max_exact + (jnp.log(nf * (1.0 / max_exact)) / math.log(MAX_DISTANCE / max_exact)
                         * (nb - max_exact)).astype(jnp.int32)
    large = jnp.minimum(large, nb - 1)
    bucket = jnp.where(rel > 0, nb, 0) + jnp.where(n < max_exact, n, large)
    masks = [(bucket & (1 << bit)) != 0 for bit in range(5)]
    inside = None if half is None else jnp.abs(x) <= half
    for hh in range(heads_per_group):
        col = grp * heads_per_group + hh
        level = [tab_ref[b, col] for b in range(NUM_BUCKETS)]
        for mask in masks:
            level = [jnp.where(mask, level[2 * k + 1], level[2 * k]) for k in range(len(level) // 2)]
        bias = level[0] * LOG2E
        if inside is not None:
            bias = jnp.where(inside, bias, NEG_INF)
        out_ref[hh] = bias


def _bias_tiles(rel_bias, n_heads, group_dils, offs, half):
    n_groups, nt = len(group_dils), len(offs)
    hpg = n_heads // n_groups
    smem = pl.BlockSpec(memory_space=pltpu.SMEM)
    return pl.pallas_call(
        functools.partial(_bias_tile_kernel, half=half, heads_per_group=hpg),
        out_shape=jax.ShapeDtypeStruct((n_heads, nt, LANES, LANES), F32),
        grid=(n_groups, nt),
        in_specs=[smem, smem, smem],
        out_specs=pl.BlockSpec((hpg, None, LANES, LANES), lambda g, t: (g, t, 0, 0)),
        compiler_params=_params(("arbitrary", "arbitrary")),
        name="bias_tiles",
    )(rel_bias, jnp.asarray(group_dils, jnp.int32), jnp.asarray(offs, jnp.int32))


def _proj_kernel(x_ref, g_ref, w_ref, cs_ref, p_ref, *out_refs, widths, n_norm_cols, group, slabs):
    h = _rms(x_ref[...], g_ref[...]).astype(BF16)
    group_w = 2 * MXU_TILE
    tiles = [(o_ref, off) for o_ref, width in zip(out_refs, widths) for off in range(0, width, group_w)]

    def project(t):
        return jnp.dot(h, w_ref[:, t * group_w:(t + 1) * group_w], preferred_element_type=F32)

    def finish(t, y):
        o_ref, off = tiles[t]
        for s in range(group_w // MXU_TILE):
            col = t * group_w + s * MXU_TILE
            ys = y[:, s * MXU_TILE:(s + 1) * MXU_TILE]
            if col < n_norm_cols:
                ss = jnp.dot((ys * ys).astype(BF16), p_ref[...], preferred_element_type=F32)
                ys = ys * lax.rsqrt(ss * (1.0 / group) + EPS) * cs_ref[:, col:col + MXU_TILE]
            lo = off + s * MXU_TILE
            if slabs:
                for half in range(MXU_TILE // LANES):
                    o_ref[lo // LANES + half] = ys[:, half * LANES:(half + 1) * LANES]
            else:
                o_ref[:, lo:lo + MXU_TILE] = ys.astype(BF16)

    y = project(0)
    for t in range(len(tiles)):
        y_next = project(t + 1) if t + 1 < len(tiles) else None
        finish(t, y)
        y = y_next


def _project(x2, gain, w, colscale, widths, n_norm_cols, group, slabs=False, tm=1024):
    m = x2.shape[0]
    n = w.shape[1]
    assert sum(widths) == n and m % tm == 0 and n_norm_cols % MXU_TILE == 0
    assert all(wd % (2 * MXU_TILE) == 0 for wd in widths)
    r = jnp.arange(MXU_TILE) // group
    ones_bd = (r[:, None] == r[None, :]).astype(BF16)
    const = lambda i: (0, 0)
    if slabs:
        out_shape = [jax.ShapeDtypeStruct((wd // LANES, m, LANES), F32) for wd in widths]
        out_specs = [pl.BlockSpec((wd // LANES, tm, LANES), lambda i: (0, i, 0)) for wd in widths]
    else:
        out_shape = [jax.ShapeDtypeStruct((m, wd), BF16) for wd in widths]
        out_specs = [pl.BlockSpec((tm, wd), lambda i: (i, 0)) for wd in widths]
    return pl.pallas_call(
        functools.partial(_proj_kernel, widths=tuple(widths), n_norm_cols=n_norm_cols, group=group, slabs=slabs),
        out_shape=out_shape,
        grid=(m // tm,),
        in_specs=[pl.BlockSpec((tm, D_MODEL), lambda i: (i, 0)),
                  pl.BlockSpec((1, D_MODEL), const),
                  pl.BlockSpec((D_MODEL, n), const),
                  pl.BlockSpec((1, n), const),
                  pl.BlockSpec((MXU_TILE, MXU_TILE), const)],
        out_specs=out_specs,
        compiler_params=_params(("arbitrary",)),
        name="qkv_proj",
    )(x2, gain.reshape(1, D_MODEL), w, colscale.reshape(1, n), ones_bd)


def _attn_a_kernel(sink_ref, q_ref, k_ref, v_ref, bias_ref, o_ref, *, nq, n_tiles):
    i = pl.program_id(1)
    lane = lax.broadcasted_iota(jnp.int32, (LANES, LANES), 1)
    lo = lane < A_HEAD_DIM
    group = A_HEADS // A_KV_HEADS

    def window(s):
        t = i * nq + s
        base = jnp.clip(t - 1, 0, n_tiles - 3)
        return t, base, pl.ds(pl.multiple_of(base * LANES, LANES), 3 * LANES)

    def scores(s, hk):
        rows = slice(s * LANES, (s + 1) * LANES)
        parts = []
        for c in (2 * hk, 2 * hk + 1):
            qc = q_ref[rows, c * LANES:(c + 1) * LANES]
            parts += [jnp.where(lo, qc, jnp.zeros_like(qc)), jnp.where(lo, jnp.zeros_like(qc), qc)]
        qs = jnp.concatenate(parts, axis=0)
        kw = k_ref[window(s)[2], hk * LANES:(hk + 1) * LANES]
        return lax.dot_general(qs, kw, (((1,), (1,)), ((), ())), preferred_element_type=F32)

    def finish(s, hk, sc):
        t, base, krows = window(s)
        rows = slice(s * LANES, (s + 1) * LANES)
        blocks = []
        for g in range(group):
            hq = hk * group + g
            blocks.append(jnp.concatenate(
                [sc[g * LANES:(g + 1) * LANES, w * LANES:(w + 1) * LANES] + bias_ref[hq, base + w - t + 2]
                 for w in range(3)], axis=1))
        sc = jnp.concatenate(blocks, axis=0)
        mx = jnp.max(sc, axis=-1, keepdims=True)
        p = jnp.exp2(sc - mx)
        den = jnp.sum(p, axis=-1, keepdims=True)
        pv = jnp.dot(p.astype(BF16), v_ref[krows, hk * LANES:(hk + 1) * LANES],
                     preferred_element_type=F32)
        outs = []
        for g in range(group):
            hq = hk * group + g
            gr = slice(g * LANES, (g + 1) * LANES)
            outs.append(pv[gr] / (den[gr] + jnp.exp2(sink_ref[hq] * LOG2E - mx[gr])))
        o_ref[rows, (2 * hk) * LANES:(2 * hk + 1) * LANES] = jnp.where(lo, outs[0], outs[1]).astype(BF16)
        o_ref[rows, (2 * hk + 1) * LANES:(2 * hk + 2) * LANES] = jnp.where(lo, outs[2], outs[3]).astype(BF16)

    blocks_todo = [(s, hk) for s in range(nq) for hk in range(A_KV_HEADS)]
    sc = scores(*blocks_todo[0])
    for n, blk in enumerate(blocks_todo):
        sc_next = scores(*blocks_todo[n + 1]) if n + 1 < len(blocks_todo) else None
        finish(*blk, sc)
        sc = sc_next


def _attention_a(qkv, bias, sink, batch, seq, tq=512):
    nq = tq // LANES
    n_tiles = seq // LANES
    qkv3 = qkv.reshape(batch, seq, 2 * D_MODEL)
    return pl.pallas_call(
        functools.partial(_attn_a_kernel, nq=nq, n_tiles=n_tiles),
        out_shape=jax.ShapeDtypeStruct((batch, seq, D_MODEL), BF16),
        grid=(batch, seq // tq),
        in_specs=[pl.BlockSpec(memory_space=pltpu.SMEM),
                  pl.BlockSpec((None, tq, D_MODEL), lambda b, i: (b, i, 0)),
                  pl.BlockSpec((None, seq, 512), lambda b, i: (b, 0, 2)),
                  pl.BlockSpec((None, seq, 512), lambda b, i: (b, 0, 3)),
                  pl.BlockSpec((A_HEADS, 5, LANES, LANES), lambda b, i: (0, 0, 0, 0))],
        out_specs=pl.BlockSpec((None, tq, D_MODEL), lambda b, i: (b, i, 0)),
        compiler_params=_params(("arbitrary", "arbitrary")),
        name="attn_window",
    )(sink, qkv3, qkv3, qkv3, bias)


def _attn_b_kernel(lam_ref, sg_ref, q_ref, k_ref, v_ref, bias_ref, o_ref, qs_ref, m_ref, acc_ref,
                   *, tq, tk, n_kc, lambda_init):
    i = pl.program_id(2)
    nqt = tq // LANES
    nkt = tk // LANES
    lo = lax.broadcasted_iota(jnp.int32, (tq, LANES), 1) < B_HEAD_DIM
    q = q_ref[...]
    zero = jnp.zeros_like(q)
    qs_ref[0:tq, :] = jnp.where(lo, q, zero)
    qs_ref[tq:2 * tq, :] = jnp.where(lo, zero, q)
    m_ref[...] = jnp.full(m_ref.shape, -jnp.inf, F32)
    acc_ref[...] = jnp.zeros(acc_ref.shape, F32)

    def map_chunk(mp, kc):
        mrows = slice(mp * tq, (mp + 1) * tq)
        keys = slice(kc * tk, (kc + 1) * tk)
        sc = lax.dot_general(qs_ref[mrows, :], k_ref[keys, :], (((1,), (1,)), ((), ())),
                             preferred_element_type=F32)
        probs = []
        for a in range(nqt):
            rows = slice(mp * tq + a * LANES, mp * tq + (a + 1) * LANES)
            tiles = []
            for w in range(nkt):
                d = jnp.clip(kc * nkt + w - (i * nqt + a), -B_FAR_TILES, B_FAR_TILES) + B_FAR_TILES
                tiles.append(sc[a * LANES:(a + 1) * LANES, w * LANES:(w + 1) * LANES] + bias_ref[mp, d])
            m_old = m_ref[rows, :]
            m_new = jnp.maximum(m_old, jnp.max(functools.reduce(jnp.maximum, tiles), axis=-1, keepdims=True))
            alpha = jnp.exp2(m_old - m_new)
            p = [jnp.exp2((t - m_new).astype(BF16)) for t in tiles]
            acc_ref[rows, :] = jnp.concatenate([alpha, alpha], axis=1) * acc_ref[rows, :]
            m_ref[rows, :] = m_new
            probs.append(jnp.concatenate(p, axis=1))
        vw = v_ref[keys, :]
        acc_ref[mrows, :] += jnp.dot(jnp.concatenate(probs, axis=0), jnp.concatenate([vw, jnp.ones_like(vw)], axis=1),
                                     preferred_element_type=F32)

    for kc in range(n_kc):
        for mp in range(2):
            map_chunk(mp, kc)
    lv = lam_ref[...]
    lam = (jnp.exp(jnp.sum(lv[0:1] * lv[1:2], axis=-1, keepdims=True))
           - jnp.exp(jnp.sum(lv[2:3] * lv[3:4], axis=-1, keepdims=True)) + lambda_init)
    o = acc_ref[:, 0:LANES] / acc_ref[:, LANES:2 * LANES]
    o = o[:tq] - lam * o[tq:]
    o_ref[...] = (_rms(o, sg_ref[...]) * (1.0 - lambda_init)).astype(BF16)


def _attention_b(qkv, bias, lam_vecs, sub_gain, lambda_init, batch, seq, tq=2048, tk=256):
    qkv3 = qkv.reshape(batch, seq, 3 * D_MODEL)
    nb = 2 * B_FAR_TILES + 1
    return pl.pallas_call(
        functools.partial(_attn_b_kernel, tq=tq, tk=tk, n_kc=seq // tk, lambda_init=lambda_init),
        out_shape=jax.ShapeDtypeStruct((batch, seq, D_MODEL), BF16),
        grid=(batch, B_HEADS, seq // tq),
        in_specs=[pl.BlockSpec((4, B_HEAD_DIM), lambda b, h, i: (0, 0)),
                  pl.BlockSpec((1, LANES), lambda b, h, i: (0, 0)),
                  pl.BlockSpec((None, tq, LANES), lambda b, h, i: (b, i, h)),
                  pl.BlockSpec((None, seq, LANES), lambda b, h, i: (b, 0, B_HEADS + h)),
                  pl.BlockSpec((None, seq, LANES), lambda b, h, i: (b, 0, 2 * B_HEADS + h)),
                  pl.BlockSpec((None, 2, nb, LANES, LANES), lambda b, h, i: (h, 0, 0, 0, 0))],
        out_specs=pl.BlockSpec((None, tq, LANES), lambda b, h, i: (b, i, h)),
        scratch_shapes=[pltpu.VMEM((2 * tq, LANES), BF16), pltpu.VMEM((2 * tq, LANES), F32),
                        pltpu.VMEM((2 * tq, 2 * LANES), F32)],
        compiler_params=_params(("arbitrary", "arbitrary", "arbitrary")),
        name="attn_diff",
    )(lam_vecs, sub_gain.reshape(1, LANES), qkv3, qkv3, qkv3, bias)


def _attn_c_kernel(q_ref, k_ref, v_ref, bias_ref, o_ref, og_ref, lg_ref, *, seq, npos, merge_rows):
    c = pl.program_id(2)

    def rows_of(start, n, dil):
        return pl.ds(start, n) if dil == 1 else pl.ds(start, n, stride=dil)

    def window(g, r, tt):
        dil = C_BRANCHES[g][1]
        t0 = c * (npos // dil) + tt * LANES
        ws = jnp.clip(t0 - C_HALF, 0, seq // dil - 2 * LANES)
        return t0, ws, rows_of(r + dil * LANES * tt, LANES, dil), rows_of(r + dil * ws, 2 * LANES, dil)

    def scores(g, r, tt):
        _, _, qrows, krows = window(g, r, tt)
        return lax.dot_general(q_ref[g, qrows, :].astype(BF16), k_ref[krows, :].astype(BF16),
                               (((1,), (1,)), ((), ())), preferred_element_type=F32)

    def finish(g, r, tt, sc):
        t0, ws, qrows, krows = window(g, r, tt)
        sc = jnp.concatenate(
            [sc[:, w * LANES:(w + 1) * LANES] + bias_ref[g, (ws + w * LANES - t0) // C_HALF + 2]
             for w in range(2)], axis=1)
        mx = jnp.max(sc, axis=-1, keepdims=True)
        p = jnp.exp2((sc - mx).astype(BF16))
        vw = v_ref[krows, :].astype(BF16)
        pv = jnp.dot(p, jnp.concatenate([vw, jnp.ones_like(vw)], axis=1), preferred_element_type=F32)
        den = pv[:, LANES:2 * LANES]
        og_ref[g, qrows, :] = pv[:, 0:LANES] / den
        lg_ref[g, qrows, :] = mx + jnp.log2(den)

    blocks_todo = [(g, r, tt) for g, (_, dil) in enumerate(C_BRANCHES) for r in range(dil)
                   for tt in range(npos // (dil * LANES))]
    sc = scores(*blocks_todo[0])
    for n, blk in enumerate(blocks_todo):
        sc_next = scores(*blocks_todo[n + 1]) if n + 1 < len(blocks_todo) else None
        finish(*blk, sc)
        sc = sc_next
    for blk in range(npos // merge_rows):
        rows = slice(blk * merge_rows, (blk + 1) * merge_rows)
        lses = [lg_ref[g, rows, :] for g in range(len(C_BRANCHES))]
        mx = functools.reduce(jnp.maximum, lses)
        es = [jnp.exp2(l - mx) for l in lses]
        num = functools.reduce(jnp.add, [e * og_ref[g, rows, :] for g, e in enumerate(es)])
        o_ref[rows, :] = (num / functools.reduce(jnp.add, es)).astype(BF16)


def _attention_c(q, k, v, bias, batch, seq, npos=4096, merge_rows=256):
    n_groups = len(C_BRANCHES)
    m = batch * seq
    chunks = seq // npos
    q4 = q.reshape(n_groups, C_KV_HEADS, m, LANES)
    bias5 = bias.reshape(n_groups, C_KV_HEADS, 5, LANES, LANES)
    return pl.pallas_call(
        functools.partial(_attn_c_kernel, seq=seq, npos=npos, merge_rows=merge_rows),
        out_shape=jax.ShapeDtypeStruct((m, C_KV_HEADS * C_HEAD_DIM), BF16),
        grid=(batch, C_KV_HEADS, chunks),
        in_specs=[pl.BlockSpec((n_groups, None, npos, LANES), lambda b, j, c: (0, j, b * chunks + c, 0)),
                  pl.BlockSpec((None, seq, LANES), lambda b, j, c: (j, b, 0)),
                  pl.BlockSpec((None, seq, LANES), lambda b, j, c: (j, b, 0)),
                  pl.BlockSpec((n_groups, None, 5, LANES, LANES), lambda b, j, c: (0, j, 0, 0, 0))],
        out_specs=pl.BlockSpec((npos, LANES), lambda b, j, c: (b * chunks + c, j)),
        scratch_shapes=[pltpu.VMEM((n_groups, npos, LANES), F32), pltpu.VMEM((n_groups, npos, LANES), F32)],
        compiler_params=_params(("arbitrary", "arbitrary", "arbitrary")),
        name="attn_dilated",
    )(q4, k, v, bias5)


def _ffn_kernel(x_ref, xn_ref, xp_ref, o_ref, on_ref, op_ref, wo_ref, g_ref, wup_ref, cw_ref, cb_ref, wdn_ref,
                *rest, tm, tiles_per_seq, n_chunks, cast_blocks):
    n_cast = len(cast_blocks)
    cast_in, y_ref, cast_out = rest[:n_cast], rest[n_cast], rest[n_cast + 1:2 * n_cast + 1]
    oe_ref, h_ref, acc_ref, ua_ref, ub_ref = rest[2 * n_cast + 1:]
    i = pl.program_id(0)
    for src, dst, blocks in zip(cast_in, cast_out, cast_blocks):
        @pl.when(i < blocks)
        def _():
            dst[...] = src[...].astype(BF16)
    pos = i % tiles_per_seq
    kdim = o_ref.shape[1]
    halo_row = lax.broadcasted_iota(jnp.int32, (FFN_HALO, kdim), 0)
    oe_ref[0:tm, :] = o_ref[...]
    oe_ref[tm:tm + FFN_HALO, :] = jnp.where(halo_row < 8, on_ref[...], op_ref[...])
    attn = jnp.dot(oe_ref[...], wo_ref[...], preferred_element_type=F32)
    x1 = x_ref[...] + attn[0:tm]
    x1n = xn_ref[...] + attn[tm:tm + 8]
    x1p = xp_ref[...] + attn[tm + 8:tm + FFN_HALO]
    g = g_ref[...]
    h_ref[0:tm, :] = _rms(x1, g).astype(BF16)
    hn = jnp.where(pos == tiles_per_seq - 1, 0.0, _rms(x1n, g))
    hp = jnp.where(pos == 0, 0.0, _rms(x1p, g))
    h_ref[tm:tm + FFN_HALO, :] = jnp.concatenate([hn, hp], axis=0).astype(BF16)
    acc_ref[...] = x1
    slabs = FFN_CHUNK // LANES

    def chunk_cols(k):
        return pl.ds(pl.multiple_of(k * FFN_CHUNK, FFN_CHUNK), FFN_CHUNK)

    def up_proj(j, u_ref):
        h = h_ref[...]
        for half, k in enumerate((j, j + n_chunks)):
            u = jnp.dot(h, wup_ref[:, chunk_cols(k)], preferred_element_type=F32)
            for s in range(slabs):
                cols = slice(s * LANES, (s + 1) * LANES)
                slab = half * slabs + s
                u_ref[slab, 8:8 + tm, :] = u[0:tm, cols]
                u_ref[slab, 0:8, :] = u[tm + 8:tm + 16, cols]
                u_ref[slab, tm + 8:tm + 16, :] = u[tm:tm + 8, cols]

    def conv(u_ref, slab, k, cols):
        w = cw_ref[:, chunk_cols(k)]
        return (w[0:1, cols] * u_ref[slab, pl.ds(7, tm), :] + w[1:2, cols] * u_ref[slab, pl.ds(8, tm), :]
                + w[2:3, cols] * u_ref[slab, pl.ds(9, tm), :] + cb_ref[:, chunk_cols(k)][:, cols])

    def gate_down(j, u_ref):
        acts = []
        for s in range(slabs):
            cols = slice(s * LANES, (s + 1) * LANES)
            gate = conv(u_ref, s, j, cols)
            val = conv(u_ref, slabs + s, j + n_chunks, cols)
            acts.append((gate / (1.0 + jnp.exp(-gate)) * val).astype(BF16))
        acc_ref[...] += jnp.dot(jnp.concatenate(acts, axis=1), wdn_ref[chunk_cols(j), :],
                                preferred_element_type=F32)

    up_proj(0, ua_ref)

    def pair(c, carry):
        j = 2 * c
        up_proj(j + 1, ub_ref)
        gate_down(j, ua_ref)
        up_proj(j + 2, ua_ref)
        gate_down(j + 1, ub_ref)
        return carry

    assert n_chunks % 2 == 1
    lax.fori_loop(0, n_chunks // 2, pair, 0, unroll=True)
    gate_down(n_chunks - 1, ua_ref)
    y_ref[...] = acc_ref[...]


def _cast_block_rows(rows, n_steps):
    return next(rb for rb in range(16, rows + 1, 16) if rows % rb == 0 and rows // rb <= n_steps)


def _proj_ffn(x2, o2, w_o, gain, w_up, conv_w, conv_b, w_down, next_weights, seq, tm=512):
    m = x2.shape[0]
    kdim = o2.shape[1]
    n_chunks = D_FF // FFN_CHUNK
    tiles_per_seq = seq // tm
    n_steps = m // tm
    x_halo = tm // 8
    o_halo = tm // FFN_HALO
    const = lambda i: (0, 0)
    once = pl.Buffered(1)
    nxt = lambda blocks, per_tile: (lambda i: (jnp.minimum((i + 1) * per_tile, blocks - 1), 0))
    prv = lambda per_tile: (lambda i: (jnp.maximum(i * per_tile - 1, 0), 0))
    cast_rows = [_cast_block_rows(w.shape[0], n_steps) for w in next_weights]
    cast_blocks = tuple(w.shape[0] // rb for w, rb in zip(next_weights, cast_rows))
    cast_specs = [pl.BlockSpec((rb, w.shape[1]), (lambda i, nb=nb: (jnp.minimum(i, nb - 1), 0)))
                  for w, rb, nb in zip(next_weights, cast_rows, cast_blocks)]
    y, *cast = pl.pallas_call(
        functools.partial(_ffn_kernel, tm=tm, tiles_per_seq=tiles_per_seq, n_chunks=n_chunks,
                          cast_blocks=cast_blocks),
        out_shape=[jax.ShapeDtypeStruct((m, D_MODEL), F32)]
                  + [jax.ShapeDtypeStruct(w.shape, BF16) for w in next_weights],
        grid=(n_steps,),
        in_specs=[pl.BlockSpec((tm, D_MODEL), lambda i: (i, 0)),
                  pl.BlockSpec((8, D_MODEL), nxt(m // 8, x_halo)),
                  pl.BlockSpec((8, D_MODEL), prv(x_halo)),
                  pl.BlockSpec((tm, kdim), lambda i: (i, 0)),
                  pl.BlockSpec((FFN_HALO, kdim), nxt(m // FFN_HALO, o_halo)),
                  pl.BlockSpec((FFN_HALO, kdim), prv(o_halo)),
                  pl.BlockSpec((kdim, D_MODEL), const, pipeline_mode=once),
                  pl.BlockSpec((1, D_MODEL), const),
                  pl.BlockSpec((D_MODEL, 2 * D_FF), const, pipeline_mode=once),
                  pl.BlockSpec((3, 2 * D_FF), const),
                  pl.BlockSpec((1, 2 * D_FF), const),
                  pl.BlockSpec((D_FF, D_MODEL), const, pipeline_mode=once)] + cast_specs,
        out_specs=[pl.BlockSpec((tm, D_MODEL), lambda i: (i, 0))] + cast_specs,
        scratch_shapes=[pltpu.VMEM((tm + FFN_HALO, kdim), BF16),
                        pltpu.VMEM((tm + FFN_HALO, D_MODEL), BF16), pltpu.VMEM((tm, D_MODEL), F32),
                        pltpu.VMEM((2 * FFN_CHUNK // LANES, tm + FFN_HALO, LANES), F32),
                        pltpu.VMEM((2 * FFN_CHUNK // LANES, tm + FFN_HALO, LANES), F32)],
        compiler_params=_params(("arbitrary",)),
        name="proj_conv_ffn",
    )(x2, x2, x2, o2, o2, o2, w_o, gain.reshape(1, D_MODEL), w_up, conv_w.reshape(3, 2 * D_FF),
      conv_b.reshape(1, 2 * D_FF), w_down, *next_weights)
    return y, cast


def _dup_heads(w, n_heads, dim):
    w = w.reshape(D_MODEL, n_heads, 1, dim)
    return jnp.broadcast_to(w, (D_MODEL, n_heads, 2, dim)).reshape(D_MODEL, 2 * n_heads * dim)


def _a_weight_layout(w_qkv):
    nq = A_HEADS * A_HEAD_DIM
    nk = A_KV_HEADS * A_HEAD_DIM
    return jnp.concatenate([w_qkv[:, :nq], _dup_heads(w_qkv[:, nq:nq + nk], A_KV_HEADS, A_HEAD_DIM),
                            _dup_heads(w_qkv[:, nq + nk:], A_KV_HEADS, A_HEAD_DIM)], axis=1)


def _mixer_a(x2, norm_gain, w, q_gain, k_gain, sink, bias, batch, seq):
    nq = A_HEADS * A_HEAD_DIM
    nk = A_KV_HEADS * A_HEAD_DIM
    colscale = jnp.concatenate([jnp.tile(q_gain, A_HEADS) * (LOG2E * A_HEAD_DIM ** -0.5), jnp.tile(k_gain, 2 * A_KV_HEADS),
                                jnp.ones((2 * nk,), F32)])
    (qkv,) = _project(x2, norm_gain, w, colscale, [2 * D_MODEL], nq + 2 * nk, A_HEAD_DIM)
    return _attention_a(qkv, bias, sink, batch, seq).reshape(batch * seq, D_MODEL)


def _mixer_b(x2, norm_gain, w, q_gain, k_gain, lam_q1, lam_k1, lam_q2, lam_k2, sub_gain, bias,
             lambda_init, batch, seq):
    colscale = jnp.concatenate([jnp.tile(q_gain, 2 * B_HEADS) * (LOG2E * B_HEAD_DIM ** -0.5), jnp.tile(k_gain, 2 * B_HEADS),
                                jnp.ones((D_MODEL,), F32)])
    (qkv,) = _project(x2, norm_gain, w, colscale, [3 * D_MODEL], 2 * D_MODEL, B_HEAD_DIM)
    lam_vecs = jnp.stack([lam_q1, lam_k1, lam_q2, lam_k2])
    return _attention_b(qkv, bias, lam_vecs, sub_gain, lambda_init, batch, seq).reshape(batch * seq, D_MODEL)


def _mixer_c(x2, norm_gain, w, q_gain, k_gain, bias, batch, seq):
    width = C_KV_HEADS * C_HEAD_DIM
    n_groups = len(C_BRANCHES)
    colscale = jnp.concatenate([jnp.tile(q_gain, n_groups * C_KV_HEADS) * (LOG2E * C_HEAD_DIM ** -0.5),
                                jnp.tile(k_gain, C_KV_HEADS), jnp.ones((width,), F32)])
    assert all(window // (2 * dil) == C_HALF for window, dil in C_BRANCHES)
    q, k, v = _project(x2, norm_gain, w, colscale, [n_groups * width, width, width],
                       (n_groups + 1) * width, C_HEAD_DIM, slabs=True)
    return _attention_c(q, k, v, bias, batch, seq)


def _lambda_init(layer):
    return 0.8 - 0.6 * math.exp(-0.3 * layer)


def kernel(x, rel_bias, l0_attn_norm, l0_w_qkv, l0_q_gain, l0_k_gain, l0_sink, l0_w_o, l0_ffn_norm, l0_w_up, l0_conv_w, l0_conv_b, l0_w_down, l1_attn_norm, l1_w_qkv, l1_q_gain, l1_k_gain, l1_lambda_q1, l1_lambda_k1, l1_lambda_q2, l1_lambda_k2, l1_sub_gain, l1_w_o, l1_ffn_norm, l1_w_up, l1_conv_w, l1_conv_b, l1_w_down, l2_attn_norm, l2_w_qkv, l2_q_gain, l2_k_gain, l2_w_o, l2_ffn_norm, l2_w_up, l2_conv_w, l2_conv_b, l2_w_down, l3_attn_norm, l3_w_qkv, l3_q_gain, l3_k_gain, l3_sink, l3_w_o, l3_ffn_norm, l3_w_up, l3_conv_w, l3_conv_b, l3_w_down):
    batch, seq, _ = x.shape
    assert seq % (2 * LANES * C_BRANCHES[-1][1]) == 0
    bias_a = _bias_tiles(rel_bias, A_HEADS, [1], [LANES * d for d in range(-2, 3)], A_WINDOW)
    bias_b = _bias_tiles(rel_bias, 2 * B_HEADS, [1], [LANES * d for d in range(-B_FAR_TILES, B_FAR_TILES + 1)], None)
    bias_c = _bias_tiles(rel_bias, len(C_BRANCHES) * C_KV_HEADS, [dil for _, dil in C_BRANCHES],
                         [C_HALF * d for d in range(-2, 3)], C_HALF)
    bias_b = bias_b.reshape(B_HEADS, 2, 2 * B_FAR_TILES + 1, LANES, LANES)

    w_qkv, w_o, w_up, w_down = (w.astype(BF16) for w in (_a_weight_layout(l0_w_qkv), l0_w_o, l0_w_up, l0_w_down))
    x2 = x.reshape(batch * seq, D_MODEL)
    o = _mixer_a(x2, l0_attn_norm, w_qkv, l0_q_gain, l0_k_gain, l0_sink, bias_a, batch, seq)
    x2, (w_qkv, w_o, w_up, w_down) = _proj_ffn(x2, o, w_o, l0_ffn_norm, w_up, l0_conv_w, l0_conv_b, w_down,
                                               [l1_w_qkv, l1_w_o, l1_w_up, l1_w_down], seq)
    o = _mixer_b(x2, l1_attn_norm, w_qkv, l1_q_gain, l1_k_gain, l1_lambda_q1, l1_lambda_k1, l1_lambda_q2,
                 l1_lambda_k2, l1_sub_gain, bias_b, _lambda_init(1), batch, seq)
    x2, (w_qkv, w_o, w_up, w_down) = _proj_ffn(x2, o, w_o, l1_ffn_norm, w_up, l1_conv_w, l1_conv_b, w_down,
                                               [l2_w_qkv, l2_w_o, l2_w_up, l2_w_down], seq)
    o = _mixer_c(x2, l2_attn_norm, w_qkv, l2_q_gain, l2_k_gain, bias_c, batch, seq)
    x2, (w_qkv, w_o, w_up, w_down) = _proj_ffn(x2, o, w_o, l2_ffn_norm, w_up, l2_conv_w, l2_conv_b, w_down,
                                               [_a_weight_layout(l3_w_qkv), l3_w_o, l3_w_up, l3_w_down], seq)
    o = _mixer_a(x2, l3_attn_norm, w_qkv, l3_q_gain, l3_k_gain, l3_sink, bias_a, batch, seq)
    x2, _ = _proj_ffn(x2, o, w_o, l3_ffn_norm, w_up, l3_conv_w, l3_conv_b, w_down, [], seq)
    return x2.reshape(batch, seq, D_MODEL)
```

```python
import functools
import math

import jax
import jax.numpy as jnp
from jax import lax
from jax.experimental import pallas as pl
from jax.experimental.pallas import tpu as pltpu

F32 = jnp.float32
BF16 = jnp.bfloat16

D_MODEL = 1024
EPS = 1e-6
NEG_INF = -1e30
LOG2E = math.log2(math.e)
LANES = 128
MXU_TILE = 256
VMEM_LIMIT = 56 * 1024 * 1024

NUM_BUCKETS = 32
MAX_DISTANCE = 1024
N_BIAS_HEADS = 16

A_HEADS, A_KV_HEADS, A_HEAD_DIM, A_WINDOW = 16, 4, 64, 128
B_HEADS, B_HEAD_DIM = 8, 64
B_FAR_TILES = 6
C_BRANCHES = ((128, 1), (512, 4), (2048, 16))
C_KV_HEADS, C_HEAD_DIM = 4, 128
C_HALF = 64
D_FF = 2816
FFN_CHUNK = 256
FFN_HALO = 16


def _params(sem, vmem=VMEM_LIMIT):
    return pltpu.CompilerParams(dimension_semantics=sem, vmem_limit_bytes=vmem)


def _rms(x, gain):
    return x * lax.rsqrt(jnp.mean(x * x, axis=-1, keepdims=True) + EPS) * gain


def _bias_tile_kernel(tab_ref, dil_ref, off_ref, out_ref, *, half, heads_per_group):
    grp = pl.program_id(0)
    t = pl.program_id(1)
    dil = dil_ref[grp]
    off = off_ref[t]
    q = lax.broadcasted_iota(jnp.int32, (LANES, LANES), 0)
    c = lax.broadcasted_iota(jnp.int32, (LANES, LANES), 1)
    x = c - q + off
    rel = x * dil
    nb = NUM_BUCKETS // 2
    max_exact = nb // 2
    n = jnp.abs(rel)
    nf = jnp.maximum(n, 1).astype(F32)
    large = max_exact + (jnp.log(nf * (1.0 / max_exact)) / math.log(MAX_DISTANCE / max_exact)
                         * (nb - max_exact)).astype(jnp.int32)
    large = jnp.minimum(large, nb - 1)
    bucket = jnp.where(rel > 0, nb, 0) + jnp.where(n < max_exact, n, large)
    masks = [(bucket & (1 << bit)) != 0 for bit in range(5)]
    inside = None if half is None else jnp.abs(x) <= half
    for hh in range(heads_per_group):
        col = grp * heads_per_group + hh
        level = [tab_ref[b, col] for b in range(NUM_BUCKETS)]
        for mask in masks:
            level = [jnp.where(mask, level[2 * k + 1], level[2 * k]) for k in range(len(level) // 2)]
        bias = level[0] * LOG2E
        if inside is not None:
            bias = jnp.where(inside, bias, NEG_INF)
        out_ref[hh] = bias


def _bias_tiles(rel_bias, n_heads, group_dils, offs, half):
    n_groups, nt = len(group_dils), len(offs)
    hpg = n_heads // n_groups
    smem = pl.BlockSpec(memory_space=pltpu.SMEM)
    return pl.pallas_call(
        functools.partial(_bias_tile_kernel, half=half, heads_per_group=hpg),
        out_shape=jax.ShapeDtypeStruct((n_heads, nt, LANES, LANES), F32),
        grid=(n_groups, nt),
        in_specs=[smem, smem, smem],
        out_specs=pl.BlockSpec((hpg, None, LANES, LANES), lambda g, t: (g, t, 0, 0)),
        compiler_params=_params(("arbitrary", "arbitrary")),
        name="bias_tiles",
    )(rel_bias, jnp.asarray(group_dils, jnp.int32), jnp.asarray(offs, jnp.int32))


def _proj_kernel(x_ref, g_ref, w_ref, cs_ref, p_ref, *out_refs, widths, n_norm_cols, group, slabs):
    h = _rms(x_ref[...], g_ref[...]).astype(BF16)
    group_w = 2 * MXU_TILE
    tiles = [(o_ref, off) for o_ref, width in zip(out_refs, widths) for off in range(0, width, group_w)]

    def project(t):
        return jnp.dot(h, w_ref[:, t * group_w:(t + 1) * group_w], preferred_element_type=F32)

    def finish(t, y):
        o_ref, off = tiles[t]
        for s in range(group_w // MXU_TILE):
            col = t * group_w + s * MXU_TILE
            ys = y[:, s * MXU_TILE:(s + 1) * MXU_TILE]
            if col < n_norm_cols:
                ss = jnp.dot((ys * ys).astype(BF16), p_ref[...], preferred_element_type=F32)
                ys = ys * lax.rsqrt(ss * (1.0 / group) + EPS) * cs_ref[:, col:col + MXU_TILE]
            lo = off + s * MXU_TILE
            if slabs:
                for half in range(MXU_TILE // LANES):
                    o_ref[lo // LANES + half] = ys[:, half * LANES:(half + 1) * LANES]
            else:
                o_ref[:, lo:lo + MXU_TILE] = ys.astype(BF16)

    y = project(0)
    for t in range(len(tiles)):
        y_next = project(t + 1) if t + 1 < len(tiles) else None
        finish(t, y)
        y = y_next


def _project(x2, gain, w, colscale, widths, n_norm_cols, group, slabs=False, tm=1024):
    m = x2.shape[0]
    n = w.shape[1]
    assert sum(widths) == n and m % tm == 0 and n_norm_cols % MXU_TILE == 0
    assert all(wd % (2 * MXU_TILE) == 0 for wd in widths)
    r = jnp.arange(MXU_TILE) // group
    ones_bd = (r[:, None] == r[None, :]).astype(BF16)
    const = lambda i: (0, 0)
    if slabs:
        out_shape = [jax.ShapeDtypeStruct((wd // LANES, m, LANES), F32) for wd in widths]
        out_specs = [pl.BlockSpec((wd // LANES, tm, LANES), lambda i: (0, i, 0)) for wd in widths]
    else:
        out_shape = [jax.ShapeDtypeStruct((m, wd), BF16) for wd in widths]
        out_specs = [pl.BlockSpec((tm, wd), lambda i: (i, 0)) for wd in widths]
    return pl.pallas_call(
        functools.partial(_proj_kernel, widths=tuple(widths), n_norm_cols=n_norm_cols, group=group, slabs=slabs),
        out_shape=out_shape,
        grid=(m // tm,),
        in_specs=[pl.BlockSpec((tm, D_MODEL), lambda i: (i, 0)),
                  pl.BlockSpec((1, D_MODEL), const),
                  pl.BlockSpec((D_MODEL, n), const),
                  pl.BlockSpec((1, n), const),
                  pl.BlockSpec((MXU_TILE, MXU_TILE), const)],
        out_specs=out_specs,
        compiler_params=_params(("arbitrary",)),
        name="qkv_proj",
    )(x2, gain.reshape(1, D_MODEL), w, colscale.reshape(1, n), ones_bd)


def _attn_a_kernel(sink_ref, q_ref, k_ref, v_ref, bias_ref, o_ref, *, nq, n_tiles):
    i = pl.program_id(1)
    lane = lax.broadcasted_iota(jnp.int32, (LANES, LANES), 1)
    lo = lane < A_HEAD_DIM
    group = A_HEADS // A_KV_HEADS

    def window(s):
        t = i * nq + s
        base = jnp.clip(t - 1, 0, n_tiles - 3)
        return t, base, pl.ds(pl.multiple_of(base * LANES, LANES), 3 * LANES)

    def scores(s, hk):
        rows = slice(s * LANES, (s + 1) * LANES)
        parts = []
        for c in (2 * hk, 2 * hk + 1):
            qc = q_ref[rows, c * LANES:(c + 1) * LANES]
            parts += [jnp.where(lo, qc, jnp.zeros_like(qc)), jnp.where(lo, jnp.zeros_like(qc), qc)]
        qs = jnp.concatenate(parts, axis=0)
        kw = k_ref[window(s)[2], hk * LANES:(hk + 1) * LANES]
        return lax.dot_general(qs, kw, (((1,), (1,)), ((), ())), preferred_element_type=F32)

    def finish(s, hk, sc):
        t, base, krows = window(s)
        rows = slice(s * LANES, (s + 1) * LANES)
        blocks = []
        for g in range(group):
            hq = hk * group + g
            blocks.append(jnp.concatenate(
                [sc[g * LANES:(g + 1) * LANES, w * LANES:(w + 1) * LANES] + bias_ref[hq, base + w - t + 2]
                 for w in range(3)], axis=1))
        sc = jnp.concatenate(blocks, axis=0)
        mx = jnp.max(sc, axis=-1, keepdims=True)
        p = jnp.exp2(sc - mx)
        den = jnp.sum(p, axis=-1, keepdims=True)
        pv = jnp.dot(p.astype(BF16), v_ref[krows, hk * LANES:(hk + 1) * LANES],
                     preferred_element_type=F32)
        outs = []
        for g in range(group):
            hq = hk * group + g
            gr = slice(g * LANES, (g + 1) * LANES)
            outs.append(pv[gr] / (den[gr] + jnp.exp2(sink_ref[hq] * LOG2E - mx[gr])))
        o_ref[rows, (2 * hk) * LANES:(2 * hk + 1) * LANES] = jnp.where(lo, outs[0], outs[1]).astype(BF16)
        o_ref[rows, (2 * hk + 1) * LANES:(2 * hk + 2) * LANES] = jnp.where(lo, outs[2], outs[3]).astype(BF16)

    blocks_todo = [(s, hk) for s in range(nq) for hk in range(A_KV_HEADS)]
    sc = scores(*blocks_todo[0])
    for n, blk in enumerate(blocks_todo):
        sc_next = scores(*blocks_todo[n + 1]) if n + 1 < len(blocks_todo) else None
        finish(*blk, sc)
        sc = sc_next


def _attention_a(qkv, bias, sink, batch, seq, tq=512):
    nq = tq // LANES
    n_tiles = seq // LANES
    qkv3 = qkv.reshape(batch, seq, 2 * D_MODEL)
    return pl.pallas_call(
        functools.partial(_attn_a_kernel, nq=nq, n_tiles=n_tiles),
        out_shape=jax.ShapeDtypeStruct((batch, seq, D_MODEL), BF16),
        grid=(batch, seq // tq),
        in_specs=[pl.BlockSpec(memory_space=pltpu.SMEM),
                  pl.BlockSpec((None, tq, D_MODEL), lambda b, i: (b, i, 0)),
                  pl.BlockSpec((None, seq, 512), lambda b, i: (b, 0, 2)),
                  pl.BlockSpec((None, seq, 512), lambda b, i: (b, 0, 3)),
                  pl.BlockSpec((A_HEADS, 5, LANES, LANES), lambda b, i: (0, 0, 0, 0))],
        out_specs=pl.BlockSpec((None, tq, D_MODEL), lambda b, i: (b, i, 0)),
        compiler_params=_params(("arbitrary", "arbitrary")),
        name="attn_window",
    )(sink, qkv3, qkv3, qkv3, bias)


def _attn_b_kernel(lam_ref, sg_ref, q_ref, k_ref, v_ref, bias_ref, o_ref, qs_ref, m_ref, acc_ref,
                   *, tq, tk, n_kc, lambda_init):
    i = pl.program_id(2)
    nqt = tq // LANES
    nkt = tk // LANES
    lo = lax.broadcasted_iota(jnp.int32, (tq, LANES), 1) < B_HEAD_DIM
    q = q_ref[...]
    zero = jnp.zeros_like(q)
    qs_ref[0:tq, :] = jnp.where(lo, q, zero)
    qs_ref[tq:2 * tq, :] = jnp.where(lo, zero, q)
    m_ref[...] = jnp.full(m_ref.shape, -jnp.inf, F32)
    acc_ref[...] = jnp.zeros(acc_ref.shape, F32)

    def map_chunk(mp, kc):
        mrows = slice(mp * tq, (mp + 1) * tq)
        keys = slice(kc * tk, (kc + 1) * tk)
        sc = lax.dot_general(qs_ref[mrows, :], k_ref[keys, :], (((1,), (1,)), ((), ())),
                             preferred_element_type=F32)
        probs = []
        for a in range(nqt):
            rows = slice(mp * tq + a * LANES, mp * tq + (a + 1) * LANES)
            tiles = []
            for w in range(nkt):
                d = jnp.clip(kc * nkt + w - (i * nqt + a), -B_FAR_TILES, B_FAR_TILES) + B_FAR_TILES
                tiles.append(sc[a * LANES:(a + 1) * LANES, w * LANES:(w + 1) * LANES] + bias_ref[mp, d])
            m_old = m_ref[rows, :]
            m_new = jnp.maximum(m_old, jnp.max(functools.reduce(jnp.maximum, tiles), axis=-1, keepdims=True))
            alpha = jnp.exp2(m_old - m_new)
            p = [jnp.exp2((t - m_new).astype(BF16)) for t in tiles]
            acc_ref[rows, :] = jnp.concatenate([alpha, alpha], axis=1) * acc_ref[rows, :]
            m_ref[rows, :] = m_new
            probs.append(jnp.concatenate(p, axis=1))
        vw = v_ref[keys, :]
        acc_ref[mrows, :] += jnp.dot(jnp.concatenate(probs, axis=0), jnp.concatenate([vw, jnp.ones_like(vw)], axis=1),
                                     preferred_element_type=F32)

    for kc in range(n_kc):
        for mp in range(2):
            map_chunk(mp, kc)
    lv = lam_ref[...]
    lam = (jnp.exp(jnp.sum(lv[0:1] * lv[1:2], axis=-1, keepdims=True))
           - jnp.exp(jnp.sum(lv[2:3] * lv[3:4], axis=-1, keepdims=True)) + lambda_init)
    o = acc_ref[:, 0:LANES] / acc_ref[:, LANES:2 * LANES]
    o = o[:tq] - lam * o[tq:]
    o_ref[...] = (_rms(o, sg_ref[...]) * (1.0 - lambda_init)).astype(BF16)


def _attention_b(qkv, bias, lam_vecs, sub_gain, lambda_init, batch, seq, tq=2048, tk=256):
    qkv3 = qkv.reshape(batch, seq, 3 * D_MODEL)
    nb = 2 * B_FAR_TILES + 1
    return pl.pallas_call(
        functools.partial(_attn_b_kernel, tq=tq, tk=tk, n_kc=seq // tk, lambda_init=lambda_init),
        out_shape=jax.ShapeDtypeStruct((batch, seq, D_MODEL), BF16),
        grid=(batch, B_HEADS, seq // tq),
        in_specs=[pl.BlockSpec((4, B_HEAD_DIM), lambda b, h, i: (0, 0)),
                  pl.BlockSpec((1, LANES), lambda b, h, i: (0, 0)),
                  pl.BlockSpec((None, tq, LANES), lambda b, h, i: (b, i, h)),
                  pl.BlockSpec((None, seq, LANES), lambda b, h, i: (b, 0, B_HEADS + h)),
                  pl.BlockSpec((None, seq, LANES), lambda b, h, i: (b, 0, 2 * B_HEADS + h)),
                  pl.BlockSpec((None, 2, nb, LANES, LANES), lambda b, h, i: (h, 0, 0, 0, 0))],
        out_specs=pl.BlockSpec((None, tq, LANES), lambda b, h, i: (b, i, h)),
        scratch_shapes=[pltpu.VMEM((2 * tq, LANES), BF16), pltpu.VMEM((2 * tq, LANES), F32),
                        pltpu.VMEM((2 * tq, 2 * LANES), F32)],
        compiler_params=_params(("arbitrary", "arbitrary", "arbitrary")),
        name="attn_diff",
    )(lam_vecs, sub_gain.reshape(1, LANES), qkv3, qkv3, qkv3, bias)


def _attn_c_kernel(q_ref, k_ref, v_ref, bias_ref, o_ref, og_ref, lg_ref, *, seq, npos, merge_rows):
    c = pl.program_id(2)

    def rows_of(start, n, dil):
        return pl.ds(start, n) if dil == 1 else pl.ds(start, n, stride=dil)

    def window(g, r, tt):
        dil = C_BRANCHES[g][1]
        t0 = c * (npos // dil) + tt * LANES
        ws = jnp.clip(t0 - C_HALF, 0, seq // dil - 2 * LANES)
        return t0, ws, rows_of(r + dil * LANES * tt, LANES, dil), rows_of(r + dil * ws, 2 * LANES, dil)

    def scores(g, r, tt):
        _, _, qrows, krows = window(g, r, tt)
        return lax.dot_general(q_ref[g, qrows, :].astype(BF16), k_ref[krows, :].astype(BF16),
                               (((1,), (1,)), ((), ())), preferred_element_type=F32)

    def finish(g, r, tt, sc):
        t0, ws, qrows, krows = window(g, r, tt)
        sc = jnp.concatenate(
            [sc[:, w * LANES:(w + 1) * LANES] + bias_ref[g, (ws + w * LANES - t0) // C_HALF + 2]
             for w in range(2)], axis=1)
        mx = jnp.max(sc, axis=-1, keepdims=True)
        p = jnp.exp2((sc - mx).astype(BF16))
        vw = v_ref[krows, :].astype(BF16)
        pv = jnp.dot(p, jnp.concatenate([vw, jnp.ones_like(vw)], axis=1), preferred_element_type=F32)
        den = pv[:, LANES:2 * LANES]
        og_ref[g, qrows, :] = pv[:, 0:LANES] / den
        lg_ref[g, qrows, :] = mx + jnp.log2(den)

    blocks_todo = [(g, r, tt) for g, (_, dil) in enumerate(C_BRANCHES) for r in range(dil)
                   for tt in range(npos // (dil * LANES))]
    sc = scores(*blocks_todo[0])
    for n, blk in enumerate(blocks_todo):
        sc_next = scores(*blocks_todo[n + 1]) if n + 1 < len(blocks_todo) else None
        finish(*blk, sc)
        sc = sc_next
    for blk in range(npos // merge_rows):
        rows = slice(blk * merge_rows, (blk + 1) * merge_rows)
        lses = [lg_ref[g, rows, :] for g in range(len(C_BRANCHES))]
        mx = functools.reduce(jnp.maximum, lses)
        es = [jnp.exp2(l - mx) for l in lses]
        num = functools.reduce(jnp.add, [e * og_ref[g, rows, :] for g, e in enumerate(es)])
        o_ref[rows, :] = (num / functools.reduce(jnp.add, es)).astype(BF16)


def _attention_c(q, k, v, bias, batch, seq, npos=4096, merge_rows=256):
    n_groups = len(C_BRANCHES)
    m = batch * seq
    chunks = seq // npos
    q4 = q.reshape(n_groups, C_KV_HEADS, m, LANES)
    bias5 = bias.reshape(n_groups, C_KV_HEADS, 5, LANES, LANES)
    return pl.pallas_call(
        functools.partial(_attn_c_kernel, seq=seq, npos=npos, merge_rows=merge_rows),
        out_shape=jax.ShapeDtypeStruct((m, C_KV_HEADS * C_HEAD_DIM), BF16),
        grid=(batch, C_KV_HEADS, chunks),
        in_specs=[pl.BlockSpec((n_groups, None, npos, LANES), lambda b, j, c: (0, j, b * chunks + c, 0)),
                  pl.BlockSpec((None, seq, LANES), lambda b, j, c: (j, b, 0)),
                  pl.BlockSpec((None, seq, LANES), lambda b, j, c: (j, b, 0)),
                  pl.BlockSpec((n_groups, None, 5, LANES, LANES), lambda b, j, c: (0, j, 0, 0, 0))],
        out_specs=pl.BlockSpec((npos, LANES), lambda b, j, c: (b * chunks + c, j)),
        scratch_shapes=[pltpu.VMEM((n_groups, npos, LANES), F32), pltpu.VMEM((n_groups, npos, LANES), F32)],
        compiler_params=_params(("arbitrary", "arbitrary", "arbitrary")),
        name="attn_dilated",
    )(q4, k, v, bias5)


def _ffn_kernel(x_ref, xn_ref, xp_ref, o_ref, on_ref, op_ref, wo_ref, g_ref, wup_ref, cw_ref, cb_ref, wdn_ref,
                *rest, tm, tiles_per_seq, n_chunks, cast_blocks):
    n_cast = len(cast_blocks)
    cast_in, y_ref, cast_out = rest[:n_cast], rest[n_cast], rest[n_cast + 1:2 * n_cast + 1]
    oe_ref, h_ref, acc_ref, ua_ref, ub_ref = rest[2 * n_cast + 1:]
    i = pl.program_id(0)
    for src, dst, blocks in zip(cast_in, cast_out, cast_blocks):
        @pl.when(i < blocks)
        def _():
            dst[...] = src[...].astype(BF16)
    pos = i % tiles_per_seq
    kdim = o_ref.shape[1]
    halo_row = lax.broadcasted_iota(jnp.int32, (FFN_HALO, kdim), 0)
    oe_ref[0:tm, :] = o_ref[...]
    oe_ref[tm:tm + FFN_HALO, :] = jnp.where(halo_row < 8, on_ref[...], op_ref[...])
    attn = jnp.dot(oe_ref[...], wo_ref[...], preferred_element_type=F32)
    x1 = x_ref[...] + attn[0:tm]
    x1n = xn_ref[...] + attn[tm:tm + 8]
    x1p = xp_ref[...] + attn[tm + 8:tm + FFN_HALO]
    g = g_ref[...]
    h_ref[0:tm, :] = _rms(x1, g).astype(BF16)
    hn = jnp.where(pos == tiles_per_seq - 1, 0.0, _rms(x1n, g))
    hp = jnp.where(pos == 0, 0.0, _rms(x1p, g))
    h_ref[tm:tm + FFN_HALO, :] = jnp.concatenate([hn, hp], axis=0).astype(BF16)
    acc_ref[...] = x1
    slabs = FFN_CHUNK // LANES

    def chunk_cols(k):
        return slice(k * FFN_CHUNK, (k + 1) * FFN_CHUNK)

    def up_proj(j, u_ref):
        h = h_ref[...]
        for half, k in enumerate((j, j + n_chunks)):
            u = jnp.dot(h, wup_ref[:, chunk_cols(k)], preferred_element_type=F32)
            for s in range(slabs):
                cols = slice(s * LANES, (s + 1) * LANES)
                slab = half * slabs + s
                u_ref[slab, 8:8 + tm, :] = u[0:tm, cols]
                u_ref[slab, 0:8, :] = u[tm + 8:tm + 16, cols]
                u_ref[slab, tm + 8:tm + 16, :] = u[tm:tm + 8, cols]

    def conv(u_ref, slab, k, cols):
        w = cw_ref[:, chunk_cols(k)]
        return (w[0:1, cols] * u_ref[slab, pl.ds(7, tm), :] + w[1:2, cols] * u_ref[slab, pl.ds(8, tm), :]
                + w[2:3, cols] * u_ref[slab, pl.ds(9, tm), :] + cb_ref[:, chunk_cols(k)][:, cols])

    def gate(j, u_ref):
        acts = []
        for s in range(slabs):
            cols = slice(s * LANES, (s + 1) * LANES)
            g_ = conv(u_ref, s, j, cols)
            val = conv(u_ref, slabs + s, j + n_chunks, cols)
            acts.append((g_ / (1.0 + jnp.exp(-g_)) * val).astype(BF16))
        return acts

    def down(j, acts):
        acc_ref[...] += jnp.dot(jnp.concatenate(acts, axis=1), wdn_ref[chunk_cols(j), :],
                                preferred_element_type=F32)

    bufs = (ua_ref, ub_ref)
    up_proj(0, bufs[0])
    pending = None
    for j in range(n_chunks):
        if j + 1 < n_chunks:
            up_proj(j + 1, bufs[(j + 1) % 2])
        if pending is not None:
            down(j - 1, pending)
        pending = gate(j, bufs[j % 2])
    down(n_chunks - 1, pending)
    y_ref[...] = acc_ref[...]


def _cast_block_rows(rows, n_steps):
    return next(rb for rb in range(16, rows + 1, 16) if rows % rb == 0 and rows // rb <= n_steps)


def _proj_ffn(x2, o2, w_o, gain, w_up, conv_w, conv_b, w_down, next_weights, seq, tm=512):
    m = x2.shape[0]
    kdim = o2.shape[1]
    n_chunks = D_FF // FFN_CHUNK
    tiles_per_seq = seq // tm
    n_steps = m // tm
    x_halo = tm // 8
    o_halo = tm // FFN_HALO
    const = lambda i: (0, 0)
    once = pl.Buffered(1)
    nxt = lambda blocks, per_tile: (lambda i: (jnp.minimum((i + 1) * per_tile, blocks - 1), 0))
    prv = lambda per_tile: (lambda i: (jnp.maximum(i * per_tile - 1, 0), 0))
    cast_rows = [_cast_block_rows(w.shape[0], n_steps) for w in next_weights]
    cast_blocks = tuple(w.shape[0] // rb for w, rb in zip(next_weights, cast_rows))
    cast_specs = [pl.BlockSpec((rb, w.shape[1]), (lambda i, nb=nb: (jnp.minimum(i, nb - 1), 0)))
                  for w, rb, nb in zip(next_weights, cast_rows, cast_blocks)]
    y, *cast = pl.pallas_call(
        functools.partial(_ffn_kernel, tm=tm, tiles_per_seq=tiles_per_seq, n_chunks=n_chunks,
                          cast_blocks=cast_blocks),
        out_shape=[jax.ShapeDtypeStruct((m, D_MODEL), F32)]
                  + [jax.ShapeDtypeStruct(w.shape, BF16) for w in next_weights],
        grid=(n_steps,),
        in_specs=[pl.BlockSpec((tm, D_MODEL), lambda i: (i, 0)),
                  pl.BlockSpec((8, D_MODEL), nxt(m // 8, x_halo)),
                  pl.BlockSpec((8, D_MODEL), prv(x_halo)),
                  pl.BlockSpec((tm, kdim), lambda i: (i, 0)),
                  pl.BlockSpec((FFN_HALO, kdim), nxt(m // FFN_HALO, o_halo)),
                  pl.BlockSpec((FFN_HALO, kdim), prv(o_halo)),
                  pl.BlockSpec((kdim, D_MODEL), const, pipeline_mode=once),
                  pl.BlockSpec((1, D_MODEL), const),
                  pl.BlockSpec((D_MODEL, 2 * D_FF), const, pipeline_mode=once),
                  pl.BlockSpec((3, 2 * D_FF), const),
                  pl.BlockSpec((1, 2 * D_FF), const),
                  pl.BlockSpec((D_FF, D_MODEL), const, pipeline_mode=once)] + cast_specs,
        out_specs=[pl.BlockSpec((tm, D_MODEL), lambda i: (i, 0))] + cast_specs,
        scratch_shapes=[pltpu.VMEM((tm + FFN_HALO, kdim), BF16),
                        pltpu.VMEM((tm + FFN_HALO, D_MODEL), BF16), pltpu.VMEM((tm, D_MODEL), F32),
                        pltpu.VMEM((2 * FFN_CHUNK // LANES, tm + FFN_HALO, LANES), F32),
                        pltpu.VMEM((2 * FFN_CHUNK // LANES, tm + FFN_HALO, LANES), F32)],
        compiler_params=_params(("arbitrary",)),
        name="proj_conv_ffn",
    )(x2, x2, x2, o2, o2, o2, w_o, gain.reshape(1, D_MODEL), w_up, conv_w.reshape(3, 2 * D_FF),
      conv_b.reshape(1, 2 * D_FF), w_down, *next_weights)
    return y, cast


def _dup_heads(w, n_heads, dim):
    w = w.reshape(D_MODEL, n_heads, 1, dim)
    return jnp.broadcast_to(w, (D_MODEL, n_heads, 2, dim)).reshape(D_MODEL, 2 * n_heads * dim)


def _a_weight_layout(w_qkv):
    nq = A_HEADS * A_HEAD_DIM
    nk = A_KV_HEADS * A_HEAD_DIM
    return jnp.concatenate([w_qkv[:, :nq], _dup_heads(w_qkv[:, nq:nq + nk], A_KV_HEADS, A_HEAD_DIM),
                            _dup_heads(w_qkv[:, nq + nk:], A_KV_HEADS, A_HEAD_DIM)], axis=1)


def _mixer_a(x2, norm_gain, w, q_gain, k_gain, sink, bias, batch, seq):
    nq = A_HEADS * A_HEAD_DIM
    nk = A_KV_HEADS * A_HEAD_DIM
    colscale = jnp.concatenate([jnp.tile(q_gain, A_HEADS) * (LOG2E * A_HEAD_DIM ** -0.5), jnp.tile(k_gain, 2 * A_KV_HEADS),
                                jnp.ones((2 * nk,), F32)])
    (qkv,) = _project(x2, norm_gain, w, colscale, [2 * D_MODEL], nq + 2 * nk, A_HEAD_DIM)
    return _attention_a(qkv, bias, sink, batch, seq).reshape(batch * seq, D_MODEL)


def _mixer_b(x2, norm_gain, w, q_gain, k_gain, lam_q1, lam_k1, lam_q2, lam_k2, sub_gain, bias,
             lambda_init, batch, seq):
    colscale = jnp.concatenate([jnp.tile(q_gain, 2 * B_HEADS) * (LOG2E * B_HEAD_DIM ** -0.5), jnp.tile(k_gain, 2 * B_HEADS),
                                jnp.ones((D_MODEL,), F32)])
    (qkv,) = _project(x2, norm_gain, w, colscale, [3 * D_MODEL], 2 * D_MODEL, B_HEAD_DIM)
    lam_vecs = jnp.stack([lam_q1, lam_k1, lam_q2, lam_k2])
    return _attention_b(qkv, bias, lam_vecs, sub_gain, lambda_init, batch, seq).reshape(batch * seq, D_MODEL)


def _mixer_c(x2, norm_gain, w, q_gain, k_gain, bias, batch, seq):
    width = C_KV_HEADS * C_HEAD_DIM
    n_groups = len(C_BRANCHES)
    colscale = jnp.concatenate([jnp.tile(q_gain, n_groups * C_KV_HEADS) * (LOG2E * C_HEAD_DIM ** -0.5),
                                jnp.tile(k_gain, C_KV_HEADS), jnp.ones((width,), F32)])
    assert all(window // (2 * dil) == C_HALF for window, dil in C_BRANCHES)
    q, k, v = _project(x2, norm_gain, w, colscale, [n_groups * width, width, width],
                       (n_groups + 1) * width, C_HEAD_DIM, slabs=True)
    return _attention_c(q, k, v, bias, batch, seq)


def _lambda_init(layer):
    return 0.8 - 0.6 * math.exp(-0.3 * layer)


def kernel(x, rel_bias, l0_attn_norm, l0_w_qkv, l0_q_gain, l0_k_gain, l0_sink, l0_w_o, l0_ffn_norm, l0_w_up, l0_conv_w, l0_conv_b, l0_w_down, l1_attn_norm, l1_w_qkv, l1_q_gain, l1_k_gain, l1_lambda_q1, l1_lambda_k1, l1_lambda_q2, l1_lambda_k2, l1_sub_gain, l1_w_o, l1_ffn_norm, l1_w_up, l1_conv_w, l1_conv_b, l1_w_down, l2_attn_norm, l2_w_qkv, l2_q_gain, l2_k_gain, l2_w_o, l2_ffn_norm, l2_w_up, l2_conv_w, l2_conv_b, l2_w_down, l3_attn_norm, l3_w_qkv, l3_q_gain, l3_k_gain, l3_sink, l3_w_o, l3_ffn_norm, l3_w_up, l3_conv_w, l3_conv_b, l3_w_down):
    batch, seq, _ = x.shape
    assert seq % (2 * LANES * C_BRANCHES[-1][1]) == 0
    bias_a = _bias_tiles(rel_bias, A_HEADS, [1], [LANES * d for d in range(-2, 3)], A_WINDOW)
    bias_b = _bias_tiles(rel_bias, 2 * B_HEADS, [1], [LANES * d for d in range(-B_FAR_TILES, B_FAR_TILES + 1)], None)
    bias_c = _bias_tiles(rel_bias, len(C_BRANCHES) * C_KV_HEADS, [dil for _, dil in C_BRANCHES],
                         [C_HALF * d for d in range(-2, 3)], C_HALF)
    bias_b = bias_b.reshape(B_HEADS, 2, 2 * B_FAR_TILES + 1, LANES, LANES)

    w_qkv, w_o, w_up, w_down = (w.astype(BF16) for w in (_a_weight_layout(l0_w_qkv), l0_w_o, l0_w_up, l0_w_down))
    x2 = x.reshape(batch * seq, D_MODEL)
    o = _mixer_a(x2, l0_attn_norm, w_qkv, l0_q_gain, l0_k_gain, l0_sink, bias_a, batch, seq)
    x2, (w_qkv, w_o, w_up, w_down) = _proj_ffn(x2, o, w_o, l0_ffn_norm, w_up, l0_conv_w, l0_conv_b, w_down,
                                               [l1_w_qkv, l1_w_o, l1_w_up, l1_w_down], seq)
    o = _mixer_b(x2, l1_attn_norm, w_qkv, l1_q_gain, l1_k_gain, l1_lambda_q1, l1_lambda_k1, l1_lambda_q2,
                 l1_lambda_k2, l1_sub_gain, bias_b, _lambda_init(1), batch, seq)
    x2, (w_qkv, w_o, w_up, w_down) = _proj_ffn(x2, o, w_o, l1_ffn_norm, w_up, l1_conv_w, l1_conv_b, w_down,
                                               [l2_w_qkv, l2_w_o, l2_w_up, l2_w_down], seq)
    o = _mixer_c(x2, l2_attn_norm, w_qkv, l2_q_gain, l2_k_gain, bias_c, batch, seq)
    x2, (w_qkv, w_o, w_up, w_down) = _proj_ffn(x2, o, w_o, l2_ffn_norm, w_up, l2_conv_w, l2_conv_b, w_down,
                                               [_a_weight_layout(l3_w_qkv), l3_w_o, l3_w_up, l3_w_down], seq)
    o = _mixer_a(x2, l3_attn_norm, w_qkv, l3_q_gain, l3_k_gain, l3_sink, bias_a, batch, seq)
    x2, _ = _proj_ffn(x2, o, w_o, l3_ffn_norm, w_up, l3_conv_w, l3_conv_b, w_down, [], seq)
    return x2.reshape(batch, seq, D_MODEL)
```

```python
import functools
import math

import jax
import jax.numpy as jnp
from jax import lax
from jax.experimental import pallas as pl
from jax.experimental.pallas import tpu as pltpu

F32 = jnp.float32
BF16 = jnp.bfloat16

D_MODEL = 1024
EPS = 1e-6
NEG_INF = -1e30
LOG2E = math.log2(math.e)
LANES = 128
MXU_TILE = 256
VMEM_LIMIT = 56 * 1024 * 1024

NUM_BUCKETS = 32
MAX_DISTANCE = 1024
N_BIAS_HEADS = 16

A_HEADS, A_KV_HEADS, A_HEAD_DIM, A_WINDOW = 16, 4, 64, 128
B_HEADS, B_HEAD_DIM = 8, 64
B_FAR_TILES = 6
C_BRANCHES = ((128, 1), (512, 4), (2048, 16))
C_KV_HEADS, C_HEAD_DIM = 4, 128
C_HALF = 64
D_FF = 2816
FFN_CHUNK = 256
FFN_HALO = 16


def _params(sem, vmem=VMEM_LIMIT):
    return pltpu.CompilerParams(dimension_semantics=sem, vmem_limit_bytes=vmem)


def _rms(x, gain):
    return x * lax.rsqrt(jnp.mean(x * x, axis=-1, keepdims=True) + EPS) * gain


def _bias_tile_kernel(tab_ref, dil_ref, off_ref, out_ref, *, half, heads_per_group):
    grp = pl.program_id(0)
    t = pl.program_id(1)
    dil = dil_ref[grp]
    off = off_ref[t]
    q = lax.broadcasted_iota(jnp.int32, (LANES, LANES), 0)
    c = lax.broadcasted_iota(jnp.int32, (LANES, LANES), 1)
    x = c - q + off
    rel = x * dil
    nb = NUM_BUCKETS // 2
    max_exact = nb // 2
    n = jnp.abs(rel)
    nf = jnp.maximum(n, 1).astype(F32)
    large = max_exact + (jnp.log(nf * (1.0 / max_exact)) / math.log(MAX_DISTANCE / max_exact)
                         * (nb - max_exact)).astype(jnp.int32)
    large = jnp.minimum(large, nb - 1)
    bucket = jnp.where(rel > 0, nb, 0) + jnp.where(n < max_exact, n, large)
    masks = [(bucket & (1 << bit)) != 0 for bit in range(5)]
    inside = None if half is None else jnp.abs(x) <= half
    for hh in range(heads_per_group):
        col = grp * heads_per_group + hh
        level = [tab_ref[b, col] for b in range(NUM_BUCKETS)]
        for mask in masks:
            level = [jnp.where(mask, level[2 * k + 1], level[2 * k]) for k in range(len(level) // 2)]
        bias = level[0] * LOG2E
        if inside is not None:
            bias = jnp.where(inside, bias, NEG_INF)
        out_ref[hh] = bias


def _bias_tiles(rel_bias, n_heads, group_dils, offs, half):
    n_groups, nt = len(group_dils), len(offs)
    hpg = n_heads // n_groups
    smem = pl.BlockSpec(memory_space=pltpu.SMEM)
    return pl.pallas_call(
        functools.partial(_bias_tile_kernel, half=half, heads_per_group=hpg),
        out_shape=jax.ShapeDtypeStruct((n_heads, nt, LANES, LANES), F32),
        grid=(n_groups, nt),
        in_specs=[smem, smem, smem],
        out_specs=pl.BlockSpec((hpg, None, LANES, LANES), lambda g, t: (g, t, 0, 0)),
        compiler_params=_params(("arbitrary", "arbitrary")),
        name="bias_tiles",
    )(rel_bias, jnp.asarray(group_dils, jnp.int32), jnp.asarray(offs, jnp.int32))


def _proj_kernel(x_ref, g_ref, w_ref, cs_ref, p_ref, *out_refs, widths, n_norm_cols, group, slabs):
    h = _rms(x_ref[...], g_ref[...]).astype(BF16)
    group_w = 2 * MXU_TILE
    tiles = [(o_ref, off) for o_ref, width in zip(out_refs, widths) for off in range(0, width, group_w)]

    def project(t):
        return jnp.dot(h, w_ref[:, t * group_w:(t + 1) * group_w], preferred_element_type=F32)

    def finish(t, y):
        o_ref, off = tiles[t]
        for s in range(group_w // MXU_TILE):
            col = t * group_w + s * MXU_TILE
            ys = y[:, s * MXU_TILE:(s + 1) * MXU_TILE]
            if col < n_norm_cols:
                ss = jnp.dot((ys * ys).astype(BF16), p_ref[...], preferred_element_type=F32)
                ys = ys * lax.rsqrt(ss * (1.0 / group) + EPS) * cs_ref[:, col:col + MXU_TILE]
            lo = off + s * MXU_TILE
            if slabs:
                for half in range(MXU_TILE // LANES):
                    o_ref[lo // LANES + half] = ys[:, half * LANES:(half + 1) * LANES]
            else:
                o_ref[:, lo:lo + MXU_TILE] = ys.astype(BF16)

    y = project(0)
    for t in range(len(tiles)):
        y_next = project(t + 1) if t + 1 < len(tiles) else None
        finish(t, y)
        y = y_next


def _project(x2, gain, w, colscale, widths, n_norm_cols, group, slabs=False, tm=1024):
    m = x2.shape[0]
    n = w.shape[1]
    assert sum(widths) == n and m % tm == 0 and n_norm_cols % MXU_TILE == 0
    assert all(wd % (2 * MXU_TILE) == 0 for wd in widths)
    r = jnp.arange(MXU_TILE) // group
    ones_bd = (r[:, None] == r[None, :]).astype(BF16)
    const = lambda i: (0, 0)
    if slabs:
        out_shape = [jax.ShapeDtypeStruct((wd // LANES, m, LANES), F32) for wd in widths]
        out_specs = [pl.BlockSpec((wd // LANES, tm, LANES), lambda i: (0, i, 0)) for wd in widths]
    else:
        out_shape = [jax.ShapeDtypeStruct((m, wd), BF16) for wd in widths]
        out_specs = [pl.BlockSpec((tm, wd), lambda i: (i, 0)) for wd in widths]
    return pl.pallas_call(
        functools.partial(_proj_kernel, widths=tuple(widths), n_norm_cols=n_norm_cols, group=group, slabs=slabs),
        out_shape=out_shape,
        grid=(m // tm,),
        in_specs=[pl.BlockSpec((tm, D_MODEL), lambda i: (i, 0)),
                  pl.BlockSpec((1, D_MODEL), const),
                  pl.BlockSpec((D_MODEL, n), const),
                  pl.BlockSpec((1, n), const),
                  pl.BlockSpec((MXU_TILE, MXU_TILE), const)],
        out_specs=out_specs,
        compiler_params=_params(("arbitrary",)),
        name="qkv_proj",
    )(x2, gain.reshape(1, D_MODEL), w, colscale.reshape(1, n), ones_bd)


def _attn_a_kernel(sink_ref, q_ref, k_ref, v_ref, bias_ref, o_ref, *, nq, n_tiles):
    i = pl.program_id(1)
    lane = lax.broadcasted_iota(jnp.int32, (LANES, LANES), 1)
    lo = lane < A_HEAD_DIM
    group = A_HEADS // A_KV_HEADS

    def window(s):
        t = i * nq + s
        base = jnp.clip(t - 1, 0, n_tiles - 3)
        return t, base, pl.ds(pl.multiple_of(base * LANES, LANES), 3 * LANES)

    def scores(s, hk):
        rows = slice(s * LANES, (s + 1) * LANES)
        parts = []
        for c in (2 * hk, 2 * hk + 1):
            qc = q_ref[rows, c * LANES:(c + 1) * LANES]
            parts += [jnp.where(lo, qc, jnp.zeros_like(qc)), jnp.where(lo, jnp.zeros_like(qc), qc)]
        qs = jnp.concatenate(parts, axis=0)
        kw = k_ref[window(s)[2], hk * LANES:(hk + 1) * LANES]
        return lax.dot_general(qs, kw, (((1,), (1,)), ((), ())), preferred_element_type=F32)

    def finish(s, hk, sc):
        t, base, krows = window(s)
        rows = slice(s * LANES, (s + 1) * LANES)
        blocks = []
        for g in range(group):
            hq = hk * group + g
            blocks.append(jnp.concatenate(
                [sc[g * LANES:(g + 1) * LANES, w * LANES:(w + 1) * LANES] + bias_ref[hq, base + w - t + 2]
                 for w in range(3)], axis=1))
        sc = jnp.concatenate(blocks, axis=0)
        mx = jnp.max(sc, axis=-1, keepdims=True)
        p = jnp.exp2(sc - mx)
        den = jnp.sum(p, axis=-1, keepdims=True)
        pv = jnp.dot(p.astype(BF16), v_ref[krows, hk * LANES:(hk + 1) * LANES],
                     preferred_element_type=F32)
        outs = []
        for g in range(group):
            hq = hk * group + g
            gr = slice(g * LANES, (g + 1) * LANES)
            outs.append(pv[gr] / (den[gr] + jnp.exp2(sink_ref[hq] * LOG2E - mx[gr])))
        o_ref[rows, (2 * hk) * LANES:(2 * hk + 1) * LANES] = jnp.where(lo, outs[0], outs[1]).astype(BF16)
        o_ref[rows, (2 * hk + 1) * LANES:(2 * hk + 2) * LANES] = jnp.where(lo, outs[2], outs[3]).astype(BF16)

    blocks_todo = [(s, hk) for s in range(nq) for hk in range(A_KV_HEADS)]
    sc = scores(*blocks_todo[0])
    for n, blk in enumerate(blocks_todo):
        sc_next = scores(*blocks_todo[n + 1]) if n + 1 < len(blocks_todo) else None
        finish(*blk, sc)
        sc = sc_next


def _attention_a(qkv, bias, sink, batch, seq, tq=512):
    nq = tq // LANES
    n_tiles = seq // LANES
    qkv3 = qkv.reshape(batch, seq, 2 * D_MODEL)
    return pl.pallas_call(
        functools.partial(_attn_a_kernel, nq=nq, n_tiles=n_tiles),
        out_shape=jax.ShapeDtypeStruct((batch, seq, D_MODEL), BF16),
        grid=(batch, seq // tq),
        in_specs=[pl.BlockSpec(memory_space=pltpu.SMEM),
                  pl.BlockSpec((None, tq, D_MODEL), lambda b, i: (b, i, 0)),
                  pl.BlockSpec((None, seq, 512), lambda b, i: (b, 0, 2)),
                  pl.BlockSpec((None, seq, 512), lambda b, i: (b, 0, 3)),
                  pl.BlockSpec((A_HEADS, 5, LANES, LANES), lambda b, i: (0, 0, 0, 0))],
        out_specs=pl.BlockSpec((None, tq, D_MODEL), lambda b, i: (b, i, 0)),
        compiler_params=_params(("arbitrary", "arbitrary")),
        name="attn_window",
    )(sink, qkv3, qkv3, qkv3, bias)


def _attn_b_kernel(lam_ref, sg_ref, q_ref, k_ref, v_ref, bias_ref, o_ref, qs_ref, m_ref, acc_ref,
                   *, tq, tk, n_kc, unroll, lambda_init):
    i = pl.program_id(2)
    nqt = tq // LANES
    nkt = tk // LANES
    lo = lax.broadcasted_iota(jnp.int32, (tq, LANES), 1) < B_HEAD_DIM
    q = q_ref[...]
    zero = jnp.zeros_like(q)
    qs_ref[0:tq, :] = jnp.where(lo, q, zero)
    qs_ref[tq:2 * tq, :] = jnp.where(lo, zero, q)
    m_ref[...] = jnp.full(m_ref.shape, -jnp.inf, F32)
    acc_ref[...] = jnp.zeros(acc_ref.shape, F32)

    def map_chunk(mp, kc):
        mrows = slice(mp * tq, (mp + 1) * tq)
        keys = pl.ds(pl.multiple_of(kc * tk, tk), tk)
        sc = lax.dot_general(qs_ref[mrows, :], k_ref[keys, :], (((1,), (1,)), ((), ())),
                             preferred_element_type=F32)
        probs = []
        for a in range(nqt):
            rows = slice(mp * tq + a * LANES, mp * tq + (a + 1) * LANES)
            tiles = []
            for w in range(nkt):
                d = jnp.clip(kc * nkt + w - (i * nqt + a), -B_FAR_TILES, B_FAR_TILES) + B_FAR_TILES
                tiles.append(sc[a * LANES:(a + 1) * LANES, w * LANES:(w + 1) * LANES] + bias_ref[mp, d])
            m_old = m_ref[rows, :]
            m_new = jnp.maximum(m_old, jnp.max(functools.reduce(jnp.maximum, tiles), axis=-1, keepdims=True))
            alpha = jnp.exp2(m_old - m_new)
            p = [jnp.exp2((t - m_new).astype(BF16)) for t in tiles]
            acc_ref[rows, :] = jnp.concatenate([alpha, alpha], axis=1) * acc_ref[rows, :]
            m_ref[rows, :] = m_new
            probs.append(jnp.concatenate(p, axis=1))
        vw = v_ref[keys, :]
        acc_ref[mrows, :] += jnp.dot(jnp.concatenate(probs, axis=0), jnp.concatenate([vw, jnp.ones_like(vw)], axis=1),
                                     preferred_element_type=F32)

    def key_chunk(kc, carry):
        for mp in range(2):
            map_chunk(mp, kc)
        return carry

    lax.fori_loop(0, n_kc, key_chunk, 0, unroll=unroll)
    lv = lam_ref[...]
    lam = (jnp.exp(jnp.sum(lv[0:1] * lv[1:2], axis=-1, keepdims=True))
           - jnp.exp(jnp.sum(lv[2:3] * lv[3:4], axis=-1, keepdims=True)) + lambda_init)
    o = acc_ref[:, 0:LANES] / acc_ref[:, LANES:2 * LANES]
    o = o[:tq] - lam * o[tq:]
    o_ref[...] = (_rms(o, sg_ref[...]) * (1.0 - lambda_init)).astype(BF16)


def _attention_b(qkv, bias, lam_vecs, sub_gain, lambda_init, batch, seq, tq=2048, tk=256, unroll=4):
    qkv3 = qkv.reshape(batch, seq, 3 * D_MODEL)
    nb = 2 * B_FAR_TILES + 1
    return pl.pallas_call(
        functools.partial(_attn_b_kernel, tq=tq, tk=tk, n_kc=seq // tk, unroll=unroll, lambda_init=lambda_init),
        out_shape=jax.ShapeDtypeStruct((batch, seq, D_MODEL), BF16),
        grid=(batch, B_HEADS, seq // tq),
        in_specs=[pl.BlockSpec((4, B_HEAD_DIM), lambda b, h, i: (0, 0)),
                  pl.BlockSpec((1, LANES), lambda b, h, i: (0, 0)),
                  pl.BlockSpec((None, tq, LANES), lambda b, h, i: (b, i, h)),
                  pl.BlockSpec((None, seq, LANES), lambda b, h, i: (b, 0, B_HEADS + h)),
                  pl.BlockSpec((None, seq, LANES), lambda b, h, i: (b, 0, 2 * B_HEADS + h)),
                  pl.BlockSpec((None, 2, nb, LANES, LANES), lambda b, h, i: (h, 0, 0, 0, 0))],
        out_specs=pl.BlockSpec((None, tq, LANES), lambda b, h, i: (b, i, h)),
        scratch_shapes=[pltpu.VMEM((2 * tq, LANES), BF16), pltpu.VMEM((2 * tq, LANES), F32),
                        pltpu.VMEM((2 * tq, 2 * LANES), F32)],
        compiler_params=_params(("arbitrary", "arbitrary", "arbitrary")),
        name="attn_diff",
    )(lam_vecs, sub_gain.reshape(1, LANES), qkv3, qkv3, qkv3, bias)


def _attn_c_kernel(q_ref, k_ref, v_ref, bias_ref, o_ref, og_ref, lg_ref, *, seq, npos, merge_rows):
    c = pl.program_id(2)

    def rows_of(start, n, dil):
        return pl.ds(start, n) if dil == 1 else pl.ds(start, n, stride=dil)

    def window(g, r, tt):
        dil = C_BRANCHES[g][1]
        t0 = c * (npos // dil) + tt * LANES
        ws = jnp.clip(t0 - C_HALF, 0, seq // dil - 2 * LANES)
        return t0, ws, rows_of(r + dil * LANES * tt, LANES, dil), rows_of(r + dil * ws, 2 * LANES, dil)

    def scores(g, r, tt):
        _, _, qrows, krows = window(g, r, tt)
        return lax.dot_general(q_ref[g, qrows, :].astype(BF16), k_ref[krows, :].astype(BF16),
                               (((1,), (1,)), ((), ())), preferred_element_type=F32)

    def finish(g, r, tt, sc):
        t0, ws, qrows, krows = window(g, r, tt)
        sc = jnp.concatenate(
            [sc[:, w * LANES:(w + 1) * LANES] + bias_ref[g, (ws + w * LANES - t0) // C_HALF + 2]
             for w in range(2)], axis=1)
        mx = jnp.max(sc, axis=-1, keepdims=True)
        p = jnp.exp2((sc - mx).astype(BF16))
        vw = v_ref[krows, :].astype(BF16)
        pv = jnp.dot(p, jnp.concatenate([vw, jnp.ones_like(vw)], axis=1), preferred_element_type=F32)
        den = pv[:, LANES:2 * LANES]
        og_ref[g, qrows, :] = pv[:, 0:LANES] / den
        lg_ref[g, qrows, :] = mx + jnp.log2(den)

    blocks_todo = [(g, r, tt) for g, (_, dil) in enumerate(C_BRANCHES) for r in range(dil)
                   for tt in range(npos // (dil * LANES))]
    sc = scores(*blocks_todo[0])
    for n, blk in enumerate(blocks_todo):
        sc_next = scores(*blocks_todo[n + 1]) if n + 1 < len(blocks_todo) else None
        finish(*blk, sc)
        sc = sc_next
    for blk in range(npos // merge_rows):
        rows = slice(blk * merge_rows, (blk + 1) * merge_rows)
        lses = [lg_ref[g, rows, :] for g in range(len(C_BRANCHES))]
        mx = functools.reduce(jnp.maximum, lses)
        es = [jnp.exp2(l - mx) for l in lses]
        num = functools.reduce(jnp.add, [e * og_ref[g, rows, :] for g, e in enumerate(es)])
        o_ref[rows, :] = (num / functools.reduce(jnp.add, es)).astype(BF16)


def _attention_c(q, k, v, bias, batch, seq, npos=4096, merge_rows=256):
    n_groups = len(C_BRANCHES)
    m = batch * seq
    chunks = seq // npos
    q4 = q.reshape(n_groups, C_KV_HEADS, m, LANES)
    bias5 = bias.reshape(n_groups, C_KV_HEADS, 5, LANES, LANES)
    return pl.pallas_call(
        functools.partial(_attn_c_kernel, seq=seq, npos=npos, merge_rows=merge_rows),
        out_shape=jax.ShapeDtypeStruct((m, C_KV_HEADS * C_HEAD_DIM), BF16),
        grid=(batch, C_KV_HEADS, chunks),
        in_specs=[pl.BlockSpec((n_groups, None, npos, LANES), lambda b, j, c: (0, j, b * chunks + c, 0)),
                  pl.BlockSpec((None, seq, LANES), lambda b, j, c: (j, b, 0)),
                  pl.BlockSpec((None, seq, LANES), lambda b, j, c: (j, b, 0)),
                  pl.BlockSpec((n_groups, None, 5, LANES, LANES), lambda b, j, c: (0, j, 0, 0, 0))],
        out_specs=pl.BlockSpec((npos, LANES), lambda b, j, c: (b * chunks + c, j)),
        scratch_shapes=[pltpu.VMEM((n_groups, npos, LANES), F32), pltpu.VMEM((n_groups, npos, LANES), F32)],
        compiler_params=_params(("arbitrary", "arbitrary", "arbitrary")),
        name="attn_dilated",
    )(q4, k, v, bias5)


def _ffn_kernel(x_ref, xn_ref, xp_ref, o_ref, on_ref, op_ref, wo_ref, g_ref, wup_ref, cw_ref, cb_ref, wdn_ref,
                *rest, tm, tiles_per_seq, n_chunks, cast_blocks):
    n_cast = len(cast_blocks)
    cast_in, y_ref, cast_out = rest[:n_cast], rest[n_cast], rest[n_cast + 1:2 * n_cast + 1]
    oe_ref, h_ref, acc_ref, ua_ref, ub_ref = rest[2 * n_cast + 1:]
    i = pl.program_id(0)
    for src, dst, blocks in zip(cast_in, cast_out, cast_blocks):
        @pl.when(i < blocks)
        def _():
            dst[...] = src[...].astype(BF16)
    pos = i % tiles_per_seq
    kdim = o_ref.shape[1]
    halo_row = lax.broadcasted_iota(jnp.int32, (FFN_HALO, kdim), 0)
    oe_ref[0:tm, :] = o_ref[...]
    oe_ref[tm:tm + FFN_HALO, :] = jnp.where(halo_row < 8, on_ref[...], op_ref[...])
    attn = jnp.dot(oe_ref[...], wo_ref[...], preferred_element_type=F32)
    x1 = x_ref[...] + attn[0:tm]
    x1n = xn_ref[...] + attn[tm:tm + 8]
    x1p = xp_ref[...] + attn[tm + 8:tm + FFN_HALO]
    g = g_ref[...]
    h_ref[0:tm, :] = _rms(x1, g).astype(BF16)
    hn = jnp.where(pos == tiles_per_seq - 1, 0.0, _rms(x1n, g))
    hp = jnp.where(pos == 0, 0.0, _rms(x1p, g))
    h_ref[tm:tm + FFN_HALO, :] = jnp.concatenate([hn, hp], axis=0).astype(BF16)
    acc_ref[...] = x1
    slabs = FFN_CHUNK // LANES

    def chunk_cols(k):
        return slice(k * FFN_CHUNK, (k + 1) * FFN_CHUNK)

    def up_proj(j, u_ref):
        h = h_ref[...]
        for half, k in enumerate((j, j + n_chunks)):
            u = jnp.dot(h, wup_ref[:, chunk_cols(k)], preferred_element_type=F32)
            for s in range(slabs):
                cols = slice(s * LANES, (s + 1) * LANES)
                slab = half * slabs + s
                u_ref[slab, 8:8 + tm, :] = u[0:tm, cols]
                u_ref[slab, 0:8, :] = u[tm + 8:tm + 16, cols]
                u_ref[slab, tm + 8:tm + 16, :] = u[tm:tm + 8, cols]

    def conv(u_ref, slab, k, cols):
        w = cw_ref[:, chunk_cols(k)]
        return (w[0:1, cols] * u_ref[slab, pl.ds(7, tm), :] + w[1:2, cols] * u_ref[slab, pl.ds(8, tm), :]
                + w[2:3, cols] * u_ref[slab, pl.ds(9, tm), :] + cb_ref[:, chunk_cols(k)][:, cols])

    def gate(j, u_ref):
        acts = []
        for s in range(slabs):
            cols = slice(s * LANES, (s + 1) * LANES)
            g_ = conv(u_ref, s, j, cols)
            val = conv(u_ref, slabs + s, j + n_chunks, cols)
            acts.append((g_ / (1.0 + jnp.exp(-g_)) * val).astype(BF16))
        return acts

    def down(j, acts):
        acc_ref[...] += jnp.dot(jnp.concatenate(acts, axis=1), wdn_ref[chunk_cols(j), :],
                                preferred_element_type=F32)

    bufs = (ua_ref, ub_ref)
    up_proj(0, bufs[0])
    pending = None
    for j in range(n_chunks):
        if j + 1 < n_chunks:
            up_proj(j + 1, bufs[(j + 1) % 2])
        if pending is not None:
            down(j - 1, pending)
        pending = gate(j, bufs[j % 2])
    down(n_chunks - 1, pending)
    y_ref[...] = acc_ref[...]


def _cast_block_rows(rows, n_steps):
    return next(rb for rb in range(16, rows + 1, 16) if rows % rb == 0 and rows // rb <= n_steps)


def _proj_ffn(x2, o2, w_o, gain, w_up, conv_w, conv_b, w_down, next_weights, seq, tm=512):
    m = x2.shape[0]
    kdim = o2.shape[1]
    n_chunks = D_FF // FFN_CHUNK
    tiles_per_seq = seq // tm
    n_steps = m // tm
    x_halo = tm // 8
    o_halo = tm // FFN_HALO
    const = lambda i: (0, 0)
    once = pl.Buffered(1)
    nxt = lambda blocks, per_tile: (lambda i: (jnp.minimum((i + 1) * per_tile, blocks - 1), 0))
    prv = lambda per_tile: (lambda i: (jnp.maximum(i * per_tile - 1, 0), 0))
    cast_rows = [_cast_block_rows(w.shape[0], n_steps) for w in next_weights]
    cast_blocks = tuple(w.shape[0] // rb for w, rb in zip(next_weights, cast_rows))
    cast_specs = [pl.BlockSpec((rb, w.shape[1]), (lambda i, nb=nb: (jnp.minimum(i, nb - 1), 0)))
                  for w, rb, nb in zip(next_weights, cast_rows, cast_blocks)]
    y, *cast = pl.pallas_call(
        functools.partial(_ffn_kernel, tm=tm, tiles_per_seq=tiles_per_seq, n_chunks=n_chunks,
                          cast_blocks=cast_blocks),
        out_shape=[jax.ShapeDtypeStruct((m, D_MODEL), F32)]
                  + [jax.ShapeDtypeStruct(w.shape, BF16) for w in next_weights],
        grid=(n_steps,),
        in_specs=[pl.BlockSpec((tm, D_MODEL), lambda i: (i, 0)),
                  pl.BlockSpec((8, D_MODEL), nxt(m // 8, x_halo)),
                  pl.BlockSpec((8, D_MODEL), prv(x_halo)),
                  pl.BlockSpec((tm, kdim), lambda i: (i, 0)),
                  pl.BlockSpec((FFN_HALO, kdim), nxt(m // FFN_HALO, o_halo)),
                  pl.BlockSpec((FFN_HALO, kdim), prv(o_halo)),
                  pl.BlockSpec((kdim, D_MODEL), const, pipeline_mode=once),
                  pl.BlockSpec((1, D_MODEL), const),
                  pl.BlockSpec((D_MODEL, 2 * D_FF), const, pipeline_mode=once),
                  pl.BlockSpec((3, 2 * D_FF), const),
                  pl.BlockSpec((1, 2 * D_FF), const),
                  pl.BlockSpec((D_FF, D_MODEL), const, pipeline_mode=once)] + cast_specs,
        out_specs=[pl.BlockSpec((tm, D_MODEL), lambda i: (i, 0))] + cast_specs,
        scratch_shapes=[pltpu.VMEM((tm + FFN_HALO, kdim), BF16),
                        pltpu.VMEM((tm + FFN_HALO, D_MODEL), BF16), pltpu.VMEM((tm, D_MODEL), F32),
                        pltpu.VMEM((2 * FFN_CHUNK // LANES, tm + FFN_HALO, LANES), F32),
                        pltpu.VMEM((2 * FFN_CHUNK // LANES, tm + FFN_HALO, LANES), F32)],
        compiler_params=_params(("arbitrary",)),
        name="proj_conv_ffn",
    )(x2, x2, x2, o2, o2, o2, w_o, gain.reshape(1, D_MODEL), w_up, conv_w.reshape(3, 2 * D_FF),
      conv_b.reshape(1, 2 * D_FF), w_down, *next_weights)
    return y, cast


def _dup_heads(w, n_heads, dim):
    w = w.reshape(D_MODEL, n_heads, 1, dim)
    return jnp.broadcast_to(w, (D_MODEL, n_heads, 2, dim)).reshape(D_MODEL, 2 * n_heads * dim)


def _a_weight_layout(w_qkv):
    nq = A_HEADS * A_HEAD_DIM
    nk = A_KV_HEADS * A_HEAD_DIM
    return jnp.concatenate([w_qkv[:, :nq], _dup_heads(w_qkv[:, nq:nq + nk], A_KV_HEADS, A_HEAD_DIM),
                            _dup_heads(w_qkv[:, nq + nk:], A_KV_HEADS, A_HEAD_DIM)], axis=1)


def _mixer_a(x2, norm_gain, w, q_gain, k_gain, sink, bias, batch, seq):
    nq = A_HEADS * A_HEAD_DIM
    nk = A_KV_HEADS * A_HEAD_DIM
    colscale = jnp.concatenate([jnp.tile(q_gain, A_HEADS) * (LOG2E * A_HEAD_DIM ** -0.5), jnp.tile(k_gain, 2 * A_KV_HEADS),
                                jnp.ones((2 * nk,), F32)])
    (qkv,) = _project(x2, norm_gain, w, colscale, [2 * D_MODEL], nq + 2 * nk, A_HEAD_DIM)
    return _attention_a(qkv, bias, sink, batch, seq).reshape(batch * seq, D_MODEL)


def _mixer_b(x2, norm_gain, w, q_gain, k_gain, lam_q1, lam_k1, lam_q2, lam_k2, sub_gain, bias,
             lambda_init, batch, seq):
    colscale = jnp.concatenate([jnp.tile(q_gain, 2 * B_HEADS) * (LOG2E * B_HEAD_DIM ** -0.5), jnp.tile(k_gain, 2 * B_HEADS),
                                jnp.ones((D_MODEL,), F32)])
    (qkv,) = _project(x2, norm_gain, w, colscale, [3 * D_MODEL], 2 * D_MODEL, B_HEAD_DIM)
    lam_vecs = jnp.stack([lam_q1, lam_k1, lam_q2, lam_k2])
    return _attention_b(qkv, bias, lam_vecs, sub_gain, lambda_init, batch, seq).reshape(batch * seq, D_MODEL)


def _mixer_c(x2, norm_gain, w, q_gain, k_gain, bias, batch, seq):
    width = C_KV_HEADS * C_HEAD_DIM
    n_groups = len(C_BRANCHES)
    colscale = jnp.concatenate([jnp.tile(q_gain, n_groups * C_KV_HEADS) * (LOG2E * C_HEAD_DIM ** -0.5),
                                jnp.tile(k_gain, C_KV_HEADS), jnp.ones((width,), F32)])
    assert all(window // (2 * dil) == C_HALF for window, dil in C_BRANCHES)
    q, k, v = _project(x2, norm_gain, w, colscale, [n_groups * width, width, width],
                       (n_groups + 1) * width, C_HEAD_DIM, slabs=True)
    return _attention_c(q, k, v, bias, batch, seq)


def _lambda_init(layer):
    return 0.8 - 0.6 * math.exp(-0.3 * layer)


def kernel(x, rel_bias, l0_attn_norm, l0_w_qkv, l0_q_gain, l0_k_gain, l0_sink, l0_w_o, l0_ffn_norm, l0_w_up, l0_conv_w, l0_conv_b, l0_w_down, l1_attn_norm, l1_w_qkv, l1_q_gain, l1_k_gain, l1_lambda_q1, l1_lambda_k1, l1_lambda_q2, l1_lambda_k2, l1_sub_gain, l1_w_o, l1_ffn_norm, l1_w_up, l1_conv_w, l1_conv_b, l1_w_down, l2_attn_norm, l2_w_qkv, l2_q_gain, l2_k_gain, l2_w_o, l2_ffn_norm, l2_w_up, l2_conv_w, l2_conv_b, l2_w_down, l3_attn_norm, l3_w_qkv, l3_q_gain, l3_k_gain, l3_sink, l3_w_o, l3_ffn_norm, l3_w_up, l3_conv_w, l3_conv_b, l3_w_down):
    batch, seq, _ = x.shape
    assert seq % (2 * LANES * C_BRANCHES[-1][1]) == 0
    bias_a = _bias_tiles(rel_bias, A_HEADS, [1], [LANES * d for d in range(-2, 3)], A_WINDOW)
    bias_b = _bias_tiles(rel_bias, 2 * B_HEADS, [1], [LANES * d for d in range(-B_FAR_TILES, B_FAR_TILES + 1)], None)
    bias_c = _bias_tiles(rel_bias, len(C_BRANCHES) * C_KV_HEADS, [dil for _, dil in C_BRANCHES],
                         [C_HALF * d for d in range(-2, 3)], C_HALF)
    bias_b = bias_b.reshape(B_HEADS, 2, 2 * B_FAR_TILES + 1, LANES, LANES)

    w_qkv, w_o, w_up, w_down = (w.astype(BF16) for w in (_a_weight_layout(l0_w_qkv), l0_w_o, l0_w_up, l0_w_down))
    x2 = x.reshape(batch * seq, D_MODEL)
    o = _mixer_a(x2, l0_attn_norm, w_qkv, l0_q_gain, l0_k_gain, l0_sink, bias_a, batch, seq)
    x2, (w_qkv, w_o, w_up, w_down) = _proj_ffn(x2, o, w_o, l0_ffn_norm, w_up, l0_conv_w, l0_conv_b, w_down,
                                               [l1_w_qkv, l1_w_o, l1_w_up, l1_w_down], seq)
    o = _mixer_b(x2, l1_attn_norm, w_qkv, l1_q_gain, l1_k_gain, l1_lambda_q1, l1_lambda_k1, l1_lambda_q2,
                 l1_lambda_k2, l1_sub_gain, bias_b, _lambda_init(1), batch, seq)
    x2, (w_qkv, w_o, w_up, w_down) = _proj_ffn(x2, o, w_o, l1_ffn_norm, w_up, l1_conv_w, l1_conv_b, w_down,
                                               [l2_w_qkv, l2_w_o, l2_w_up, l2_w_down], seq)
    o = _mixer_c(x2, l2_attn_norm, w_qkv, l2_q_gain, l2_k_gain, bias_c, batch, seq)
    x2, (w_qkv, w_o, w_up, w_down) = _proj_ffn(x2, o, w_o, l2_ffn_norm, w_up, l2_conv_w, l2_conv_b, w_down,
                                               [_a_weight_layout(l3_w_qkv), l3_w_o, l3_w_up, l3_w_down], seq)
    o = _mixer_a(x2, l3_attn_norm, w_qkv, l3_q_gain, l3_k_gain, l3_sink, bias_a, batch, seq)
    x2, _ = _proj_ffn(x2, o, w_o, l3_ffn_norm, w_up, l3_conv_w, l3_conv_b, w_down, [], seq)
    return x2.reshape(batch, seq, D_MODEL)
```

```python
import functools
import math

import jax
import jax.numpy as jnp
from jax import lax
from jax.experimental import pallas as pl
from jax.experimental.pallas import tpu as pltpu

F32 = jnp.float32
BF16 = jnp.bfloat16

D_MODEL = 1024
EPS = 1e-6
NEG_INF = -1e30
LOG2E = math.log2(math.e)
LANES = 128
MXU_TILE = 256
VMEM_LIMIT = 56 * 1024 * 1024

NUM_BUCKETS = 32
MAX_DISTANCE = 1024

A_HEADS, A_KV_HEADS, A_HEAD_DIM, A_WINDOW = 16, 4, 64, 128
B_HEADS, B_HEAD_DIM = 8, 64
B_FAR_TILES = 6
C_BRANCHES = ((128, 1), (512, 4), (2048, 16))
C_KV_HEADS, C_HEAD_DIM = 4, 128
C_HALF = 64
D_FF = 2816
FFN_CHUNK = 256
FFN_HALO = 16


def _params(sem, vmem=VMEM_LIMIT):
    return pltpu.CompilerParams(dimension_semantics=sem, vmem_limit_bytes=vmem)


def _rms(x, gain):
    return x * lax.rsqrt(jnp.mean(x * x, axis=-1, keepdims=True) + EPS) * gain


def _bias_tile_kernel(tab_ref, dil_ref, off_ref, out_ref, *, half, heads_per_group):
    grp = pl.program_id(0)
    t = pl.program_id(1)
    dil = dil_ref[grp]
    off = off_ref[t]
    q = lax.broadcasted_iota(jnp.int32, (LANES, LANES), 0)
    c = lax.broadcasted_iota(jnp.int32, (LANES, LANES), 1)
    x = c - q + off
    rel = x * dil
    nb = NUM_BUCKETS // 2
    max_exact = nb // 2
    n = jnp.abs(rel)
    nf = jnp.maximum(n, 1).astype(F32)
    large = max_exact + (jnp.log(nf * (1.0 / max_exact)) / math.log(MAX_DISTANCE / max_exact)
                         * (nb - max_exact)).astype(jnp.int32)
    large = jnp.minimum(large, nb - 1)
    bucket = jnp.where(rel > 0, nb, 0) + jnp.where(n < max_exact, n, large)
    masks = [(bucket & (1 << bit)) != 0 for bit in range(5)]
    inside = None if half is None else jnp.abs(x) <= half
    for hh in range(heads_per_group):
        col = grp * heads_per_group + hh
        level = [tab_ref[b, col] for b in range(NUM_BUCKETS)]
        for mask in masks:
            level = [jnp.where(mask, level[2 * k + 1], level[2 * k]) for k in range(len(level) // 2)]
        bias = level[0] * LOG2E
        if inside is not None:
            bias = jnp.where(inside, bias, NEG_INF)
        out_ref[hh] = bias


def _bias_tiles(rel_bias, n_heads, group_dils, offs, half):
    n_groups, nt = len(group_dils), len(offs)
    hpg = n_heads // n_groups
    smem = pl.BlockSpec(memory_space=pltpu.SMEM)
    return pl.pallas_call(
        functools.partial(_bias_tile_kernel, half=half, heads_per_group=hpg),
        out_shape=jax.ShapeDtypeStruct((n_heads, nt, LANES, LANES), F32),
        grid=(n_groups, nt),
        in_specs=[smem, smem, smem],
        out_specs=pl.BlockSpec((hpg, None, LANES, LANES), lambda g, t: (g, t, 0, 0)),
        compiler_params=_params(("arbitrary", "arbitrary")),
        name="bias_tiles",
    )(rel_bias, jnp.asarray(group_dils, jnp.int32), jnp.asarray(offs, jnp.int32))


def _proj_kernel(x_ref, g_ref, w_ref, cs_ref, p_ref, *out_refs, widths, n_norm_cols, group, slabs):
    h = _rms(x_ref[...], g_ref[...]).astype(BF16)
    group_w = 2 * MXU_TILE
    tiles = [(o_ref, off) for o_ref, width in zip(out_refs, widths) for off in range(0, width, group_w)]

    def project(t):
        return jnp.dot(h, w_ref[:, t * group_w:(t + 1) * group_w], preferred_element_type=F32)

    def finish(t, y):
        o_ref, off = tiles[t]
        for s in range(group_w // MXU_TILE):
            col = t * group_w + s * MXU_TILE
            ys = y[:, s * MXU_TILE:(s + 1) * MXU_TILE]
            if col < n_norm_cols:
                ss = jnp.dot((ys * ys).astype(BF16), p_ref[...], preferred_element_type=F32)
                ys = ys * lax.rsqrt(ss * (1.0 / group) + EPS) * cs_ref[:, col:col + MXU_TILE]
            lo = off + s * MXU_TILE
            if slabs:
                for half in range(MXU_TILE // LANES):
                    o_ref[lo // LANES + half] = ys[:, half * LANES:(half + 1) * LANES]
            else:
                o_ref[:, lo:lo + MXU_TILE] = ys.astype(BF16)

    y = project(0)
    for t in range(len(tiles)):
        y_next = project(t + 1) if t + 1 < len(tiles) else None
        finish(t, y)
        y = y_next


def _project(x2, gain, w, colscale, widths, n_norm_cols, group, slabs=False, tm=1024):
    m = x2.shape[0]
    n = w.shape[1]
    assert sum(widths) == n and m % tm == 0 and n_norm_cols % MXU_TILE == 0
    assert all(wd % (2 * MXU_TILE) == 0 for wd in widths)
    r = jnp.arange(MXU_TILE) // group
    ones_bd = (r[:, None] == r[None, :]).astype(BF16)
    const = lambda i: (0, 0)
    if slabs:
        out_shape = [jax.ShapeDtypeStruct((wd // LANES, m, LANES), F32) for wd in widths]
        out_specs = [pl.BlockSpec((wd // LANES, tm, LANES), lambda i: (0, i, 0)) for wd in widths]
    else:
        out_shape = [jax.ShapeDtypeStruct((m, wd), BF16) for wd in widths]
        out_specs = [pl.BlockSpec((tm, wd), lambda i: (i, 0)) for wd in widths]
    return pl.pallas_call(
        functools.partial(_proj_kernel, widths=tuple(widths), n_norm_cols=n_norm_cols, group=group, slabs=slabs),
        out_shape=out_shape,
        grid=(m // tm,),
        in_specs=[pl.BlockSpec((tm, D_MODEL), lambda i: (i, 0)),
                  pl.BlockSpec((1, D_MODEL), const),
                  pl.BlockSpec((D_MODEL, n), const),
                  pl.BlockSpec((1, n), const),
                  pl.BlockSpec((MXU_TILE, MXU_TILE), const)],
        out_specs=out_specs,
        compiler_params=_params(("arbitrary",)),
        name="qkv_proj",
    )(x2, gain.reshape(1, D_MODEL), w, colscale.reshape(1, n), ones_bd)


def _attn_a_kernel(sink_ref, q_ref, k_ref, v_ref, bias_ref, o_ref, *, nq, n_tiles):
    i = pl.program_id(1)
    lane = lax.broadcasted_iota(jnp.int32, (LANES, LANES), 1)
    lo = lane < A_HEAD_DIM
    group = A_HEADS // A_KV_HEADS

    def window(s):
        t = i * nq + s
        base = jnp.clip(t - 1, 0, n_tiles - 3)
        return t, base, pl.ds(pl.multiple_of(base * LANES, LANES), 3 * LANES)

    def scores(s, hk):
        rows = slice(s * LANES, (s + 1) * LANES)
        parts = []
        for c in (2 * hk, 2 * hk + 1):
            qc = q_ref[rows, c * LANES:(c + 1) * LANES]
            parts += [jnp.where(lo, qc, jnp.zeros_like(qc)), jnp.where(lo, jnp.zeros_like(qc), qc)]
        qs = jnp.concatenate(parts, axis=0)
        kw = k_ref[window(s)[2], hk * LANES:(hk + 1) * LANES]
        return lax.dot_general(qs, kw, (((1,), (1,)), ((), ())), preferred_element_type=F32)

    def finish(s, hk, sc):
        t, base, krows = window(s)
        rows = slice(s * LANES, (s + 1) * LANES)
        blocks = []
        for g in range(group):
            hq = hk * group + g
            blocks.append(jnp.concatenate(
                [sc[g * LANES:(g + 1) * LANES, w * LANES:(w + 1) * LANES] + bias_ref[hq, base + w - t + 2]
                 for w in range(3)], axis=1))
        sc = jnp.concatenate(blocks, axis=0)
        mx = jnp.max(sc, axis=-1, keepdims=True)
        p = jnp.exp2(sc - mx)
        den = jnp.sum(p, axis=-1, keepdims=True)
        pv = jnp.dot(p.astype(BF16), v_ref[krows, hk * LANES:(hk + 1) * LANES],
                     preferred_element_type=F32)
        outs = []
        for g in range(group):
            hq = hk * group + g
            gr = slice(g * LANES, (g + 1) * LANES)
            outs.append(pv[gr] / (den[gr] + jnp.exp2(sink_ref[hq] * LOG2E - mx[gr])))
        o_ref[rows, (2 * hk) * LANES:(2 * hk + 1) * LANES] = jnp.where(lo, outs[0], outs[1]).astype(BF16)
        o_ref[rows, (2 * hk + 1) * LANES:(2 * hk + 2) * LANES] = jnp.where(lo, outs[2], outs[3]).astype(BF16)

    blocks_todo = [(s, hk) for s in range(nq) for hk in range(A_KV_HEADS)]
    sc = scores(*blocks_todo[0])
    for n, blk in enumerate(blocks_todo):
        sc_next = scores(*blocks_todo[n + 1]) if n + 1 < len(blocks_todo) else None
        finish(*blk, sc)
        sc = sc_next


def _attention_a(qkv, bias, sink, batch, seq, tq=512):
    nq = tq // LANES
    n_tiles = seq // LANES
    qkv3 = qkv.reshape(batch, seq, 2 * D_MODEL)
    return pl.pallas_call(
        functools.partial(_attn_a_kernel, nq=nq, n_tiles=n_tiles),
        out_shape=jax.ShapeDtypeStruct((batch, seq, D_MODEL), BF16),
        grid=(batch, seq // tq),
        in_specs=[pl.BlockSpec(memory_space=pltpu.SMEM),
                  pl.BlockSpec((None, tq, D_MODEL), lambda b, i: (b, i, 0)),
                  pl.BlockSpec((None, seq, 512), lambda b, i: (b, 0, 2)),
                  pl.BlockSpec((None, seq, 512), lambda b, i: (b, 0, 3)),
                  pl.BlockSpec((A_HEADS, 5, LANES, LANES), lambda b, i: (0, 0, 0, 0))],
        out_specs=pl.BlockSpec((None, tq, D_MODEL), lambda b, i: (b, i, 0)),
        compiler_params=_params(("arbitrary", "arbitrary")),
        name="attn_window",
    )(sink, qkv3, qkv3, qkv3, bias)


def _attn_b_kernel(lam_ref, sg_ref, q_ref, k_ref, v_ref, bias_ref, o_ref, qs_ref, m_ref, acc_ref,
                   *, tq, tk, n_kc, lambda_init):
    i = pl.program_id(2)
    nqt = tq // LANES
    nkt = tk // LANES
    lo = lax.broadcasted_iota(jnp.int32, (tq, LANES), 1) < B_HEAD_DIM
    q = q_ref[...]
    zero = jnp.zeros_like(q)
    qs_ref[0:tq, :] = jnp.where(lo, q, zero)
    qs_ref[tq:2 * tq, :] = jnp.where(lo, zero, q)
    m_ref[...] = jnp.full(m_ref.shape, -jnp.inf, F32)
    acc_ref[...] = jnp.zeros(acc_ref.shape, F32)

    def map_chunk(mp, kc):
        mrows = slice(mp * tq, (mp + 1) * tq)
        keys = slice(kc * tk, (kc + 1) * tk)
        sc = lax.dot_general(qs_ref[mrows, :], k_ref[keys, :], (((1,), (1,)), ((), ())),
                             preferred_element_type=F32)
        probs = []
        for a in range(nqt):
            rows = slice(mp * tq + a * LANES, mp * tq + (a + 1) * LANES)
            tiles = []
            for w in range(nkt):
                d = jnp.clip(kc * nkt + w - (i * nqt + a), -B_FAR_TILES, B_FAR_TILES) + B_FAR_TILES
                tiles.append(sc[a * LANES:(a + 1) * LANES, w * LANES:(w + 1) * LANES] + bias_ref[mp, d])
            m_old = m_ref[rows, :]
            m_new = jnp.maximum(m_old, jnp.max(functools.reduce(jnp.maximum, tiles), axis=-1, keepdims=True))
            alpha = jnp.exp2(m_old - m_new)
            p = [jnp.exp2((t - m_new).astype(BF16)) for t in tiles]
            acc_ref[rows, :] = jnp.concatenate([alpha, alpha], axis=1) * acc_ref[rows, :]
            m_ref[rows, :] = m_new
            probs.append(jnp.concatenate(p, axis=1))
        vw = v_ref[keys, :]
        acc_ref[mrows, :] += jnp.dot(jnp.concatenate(probs, axis=0), jnp.concatenate([vw, jnp.ones_like(vw)], axis=1),
                                     preferred_element_type=F32)

    for kc in range(n_kc):
        for mp in range(2):
            map_chunk(mp, kc)
    lv = lam_ref[...]
    lam = (jnp.exp(jnp.sum(lv[0:1] * lv[1:2], axis=-1, keepdims=True))
           - jnp.exp(jnp.sum(lv[2:3] * lv[3:4], axis=-1, keepdims=True)) + lambda_init)
    o = acc_ref[:, 0:LANES] / acc_ref[:, LANES:2 * LANES]
    o = o[:tq] - lam * o[tq:]
    o_ref[...] = (_rms(o, sg_ref[...]) * (1.0 - lambda_init)).astype(BF16)


def _attention_b(qkv, bias, lam_vecs, sub_gain, lambda_init, batch, seq, tq=2048, tk=256):
    qkv3 = qkv.reshape(batch, seq, 3 * D_MODEL)
    nb = 2 * B_FAR_TILES + 1
    return pl.pallas_call(
        functools.partial(_attn_b_kernel, tq=tq, tk=tk, n_kc=seq // tk, lambda_init=lambda_init),
        out_shape=jax.ShapeDtypeStruct((batch, seq, D_MODEL), BF16),
        grid=(batch, B_HEADS, seq // tq),
        in_specs=[pl.BlockSpec((4, B_HEAD_DIM), lambda b, h, i: (0, 0)),
                  pl.BlockSpec((1, LANES), lambda b, h, i: (0, 0)),
                  pl.BlockSpec((None, tq, LANES), lambda b, h, i: (b, i, h)),
                  pl.BlockSpec((None, seq, LANES), lambda b, h, i: (b, 0, B_HEADS + h)),
                  pl.BlockSpec((None, seq, LANES), lambda b, h, i: (b, 0, 2 * B_HEADS + h)),
                  pl.BlockSpec((None, 2, nb, LANES, LANES), lambda b, h, i: (h, 0, 0, 0, 0))],
        out_specs=pl.BlockSpec((None, tq, LANES), lambda b, h, i: (b, i, h)),
        scratch_shapes=[pltpu.VMEM((2 * tq, LANES), BF16), pltpu.VMEM((2 * tq, LANES), F32),
                        pltpu.VMEM((2 * tq, 2 * LANES), F32)],
        compiler_params=_params(("arbitrary", "arbitrary", "arbitrary")),
        name="attn_diff",
    )(lam_vecs, sub_gain.reshape(1, LANES), qkv3, qkv3, qkv3, bias)


def _attn_c_kernel(q_ref, k_ref, v_ref, bias_ref, o_ref, og_ref, lg_ref, *, seq, npos, merge_rows):
    c = pl.program_id(2)

    def rows_of(start, n, dil):
        return pl.ds(start, n) if dil == 1 else pl.ds(start, n, stride=dil)

    def window(g, r, tt):
        dil = C_BRANCHES[g][1]
        t0 = c * (npos // dil) + tt * LANES
        ws = jnp.clip(t0 - C_HALF, 0, seq // dil - 2 * LANES)
        return t0, ws, rows_of(r + dil * LANES * tt, LANES, dil), rows_of(r + dil * ws, 2 * LANES, dil)

    def scores(g, r, tt):
        _, _, qrows, krows = window(g, r, tt)
        return lax.dot_general(q_ref[g, qrows, :].astype(BF16), k_ref[krows, :].astype(BF16),
                               (((1,), (1,)), ((), ())), preferred_element_type=F32)

    def finish(g, r, tt, sc):
        t0, ws, qrows, krows = window(g, r, tt)
        sc = jnp.concatenate(
            [sc[:, w * LANES:(w + 1) * LANES] + bias_ref[g, (ws + w * LANES - t0) // C_HALF + 2]
             for w in range(2)], axis=1)
        mx = jnp.max(sc, axis=-1, keepdims=True)
        p = jnp.exp2((sc - mx).astype(BF16))
        vw = v_ref[krows, :].astype(BF16)
        pv = jnp.dot(p, jnp.concatenate([vw, jnp.ones_like(vw)], axis=1), preferred_element_type=F32)
        den = pv[:, LANES:2 * LANES]
        og_ref[g, qrows, :] = pv[:, 0:LANES] / den
        lg_ref[g, qrows, :] = mx + jnp.log2(den)

    blocks_todo = [(g, r, tt) for g, (_, dil) in enumerate(C_BRANCHES) for r in range(dil)
                   for tt in range(npos // (dil * LANES))]
    sc = scores(*blocks_todo[0])
    for n, blk in enumerate(blocks_todo):
        sc_next = scores(*blocks_todo[n + 1]) if n + 1 < len(blocks_todo) else None
        finish(*blk, sc)
        sc = sc_next
    for blk in range(npos // merge_rows):
        rows = slice(blk * merge_rows, (blk + 1) * merge_rows)
        lses = [lg_ref[g, rows, :] for g in range(len(C_BRANCHES))]
        mx = functools.reduce(jnp.maximum, lses)
        es = [jnp.exp2(l - mx) for l in lses]
        num = functools.reduce(jnp.add, [e * og_ref[g, rows, :] for g, e in enumerate(es)])
        o_ref[rows, :] = (num / functools.reduce(jnp.add, es)).astype(BF16)


def _attention_c(q, k, v, bias, batch, seq, npos=4096, merge_rows=256):
    n_groups = len(C_BRANCHES)
    m = batch * seq
    chunks = seq // npos
    q4 = q.reshape(n_groups, C_KV_HEADS, m, LANES)
    bias5 = bias.reshape(n_groups, C_KV_HEADS, 5, LANES, LANES)
    return pl.pallas_call(
        functools.partial(_attn_c_kernel, seq=seq, npos=npos, merge_rows=merge_rows),
        out_shape=jax.ShapeDtypeStruct((m, C_KV_HEADS * C_HEAD_DIM), BF16),
        grid=(batch, C_KV_HEADS, chunks),
        in_specs=[pl.BlockSpec((n_groups, None, npos, LANES), lambda b, j, c: (0, j, b * chunks + c, 0)),
                  pl.BlockSpec((None, seq, LANES), lambda b, j, c: (j, b, 0)),
                  pl.BlockSpec((None, seq, LANES), lambda b, j, c: (j, b, 0)),
                  pl.BlockSpec((n_groups, None, 5, LANES, LANES), lambda b, j, c: (0, j, 0, 0, 0))],
        out_specs=pl.BlockSpec((npos, LANES), lambda b, j, c: (b * chunks + c, j)),
        scratch_shapes=[pltpu.VMEM((n_groups, npos, LANES), F32), pltpu.VMEM((n_groups, npos, LANES), F32)],
        compiler_params=_params(("arbitrary", "arbitrary", "arbitrary")),
        name="attn_dilated",
    )(q4, k, v, bias5)


def _ffn_kernel(x_ref, xn_ref, xp_ref, o_ref, on_ref, op_ref, wo_ref, g_ref, wup_ref, cw_ref, cb_ref, wdn_ref,
                *rest, tm, tiles_per_seq, n_chunks, cast_blocks):
    n_cast = len(cast_blocks)
    cast_in, y_ref, cast_out = rest[:n_cast], rest[n_cast], rest[n_cast + 1:2 * n_cast + 1]
    oe_ref, h_ref, acc_ref, ua_ref, ub_ref = rest[2 * n_cast + 1:]
    i = pl.program_id(0)
    for src, dst, blocks in zip(cast_in, cast_out, cast_blocks):
        @pl.when(i < blocks)
        def _():
            dst[...] = src[...].astype(BF16)
    pos = i % tiles_per_seq
    kdim = o_ref.shape[1]
    halo_row = lax.broadcasted_iota(jnp.int32, (FFN_HALO, kdim), 0)
    oe_ref[0:tm, :] = o_ref[...]
    oe_ref[tm:tm + FFN_HALO, :] = jnp.where(halo_row < 8, on_ref[...], op_ref[...])
    attn = jnp.dot(oe_ref[...], wo_ref[...], preferred_element_type=F32)
    x1 = x_ref[...] + attn[0:tm]
    x1n = xn_ref[...] + attn[tm:tm + 8]
    x1p = xp_ref[...] + attn[tm + 8:tm + FFN_HALO]
    g = g_ref[...]
    h_ref[0:tm, :] = _rms(x1, g).astype(BF16)
    hn = jnp.where(pos == tiles_per_seq - 1, 0.0, _rms(x1n, g))
    hp = jnp.where(pos == 0, 0.0, _rms(x1p, g))
    h_ref[tm:tm + FFN_HALO, :] = jnp.concatenate([hn, hp], axis=0).astype(BF16)
    acc_ref[...] = x1
    slabs = FFN_CHUNK // LANES

    def chunk_cols(k):
        return slice(k * FFN_CHUNK, (k + 1) * FFN_CHUNK)

    def up_proj(j, u_ref):
        h = h_ref[...]
        for half, k in enumerate((j, j + n_chunks)):
            u = jnp.dot(h, wup_ref[:, chunk_cols(k)], preferred_element_type=F32)
            for s in range(slabs):
                cols = slice(s * LANES, (s + 1) * LANES)
                slab = half * slabs + s
                u_ref[slab, 8:8 + tm, :] = u[0:tm, cols]
                u_ref[slab, 0:8, :] = u[tm + 8:tm + 16, cols]
                u_ref[slab, tm + 8:tm + 16, :] = u[tm:tm + 8, cols]

    def conv(u_ref, slab, k, cols):
        w = cw_ref[:, chunk_cols(k)]
        return (w[0:1, cols] * u_ref[slab, pl.ds(7, tm), :] + w[1:2, cols] * u_ref[slab, pl.ds(8, tm), :]
                + w[2:3, cols] * u_ref[slab, pl.ds(9, tm), :] + cb_ref[:, chunk_cols(k)][:, cols])

    def gate(j, u_ref):
        acts = []
        for s in range(slabs):
            cols = slice(s * LANES, (s + 1) * LANES)
            g_ = conv(u_ref, s, j, cols)
            val = conv(u_ref, slabs + s, j + n_chunks, cols)
            acts.append((g_ / (1.0 + jnp.exp(-g_)) * val).astype(BF16))
        return acts

    def down(j, acts):
        acc_ref[...] += jnp.dot(jnp.concatenate(acts, axis=1), wdn_ref[chunk_cols(j), :],
                                preferred_element_type=F32)

    bufs = (ua_ref, ub_ref)
    up_proj(0, bufs[0])
    pending = None
    for j in range(n_chunks):
        if j + 1 < n_chunks:
            up_proj(j + 1, bufs[(j + 1) % 2])
        if pending is not None:
            down(j - 1, pending)
        pending = gate(j, bufs[j % 2])
    down(n_chunks - 1, pending)
    y_ref[...] = acc_ref[...]


def _cast_block_rows(rows, n_steps):
    return next(rb for rb in range(16, rows + 1, 16) if rows % rb == 0 and rows // rb <= n_steps)


def _proj_ffn(x2, o2, w_o, gain, w_up, conv_w, conv_b, w_down, next_weights, seq, tm=512):
    m = x2.shape[0]
    kdim = o2.shape[1]
    n_chunks = D_FF // FFN_CHUNK
    tiles_per_seq = seq // tm
    n_steps = m // tm
    x_halo = tm // 8
    o_halo = tm // FFN_HALO
    const = lambda i: (0, 0)
    once = pl.Buffered(1)
    nxt = lambda blocks, per_tile: (lambda i: (jnp.minimum((i + 1) * per_tile, blocks - 1), 0))
    prv = lambda per_tile: (lambda i: (jnp.maximum(i * per_tile - 1, 0), 0))
    cast_rows = [_cast_block_rows(w.shape[0], n_steps) for w in next_weights]
    cast_blocks = tuple(w.shape[0] // rb for w, rb in zip(next_weights, cast_rows))
    cast_specs = [pl.BlockSpec((rb, w.shape[1]), (lambda i, nb=nb: (jnp.minimum(i, nb - 1), 0)))
                  for w, rb, nb in zip(next_weights, cast_rows, cast_blocks)]
    y, *cast = pl.pallas_call(
        functools.partial(_ffn_kernel, tm=tm, tiles_per_seq=tiles_per_seq, n_chunks=n_chunks,
                          cast_blocks=cast_blocks),
        out_shape=[jax.ShapeDtypeStruct((m, D_MODEL), F32)]
                  + [jax.ShapeDtypeStruct(w.shape, BF16) for w in next_weights],
        grid=(n_steps,),
        in_specs=[pl.BlockSpec((tm, D_MODEL), lambda i: (i, 0)),
                  pl.BlockSpec((8, D_MODEL), nxt(m // 8, x_halo)),
                  pl.BlockSpec((8, D_MODEL), prv(x_halo)),
                  pl.BlockSpec((tm, kdim), lambda i: (i, 0)),
                  pl.BlockSpec((FFN_HALO, kdim), nxt(m // FFN_HALO, o_halo)),
                  pl.BlockSpec((FFN_HALO, kdim), prv(o_halo)),
                  pl.BlockSpec((kdim, D_MODEL), const, pipeline_mode=once),
                  pl.BlockSpec((1, D_MODEL), const),
                  pl.BlockSpec((D_MODEL, 2 * D_FF), const, pipeline_mode=once),
                  pl.BlockSpec((3, 2 * D_FF), const),
                  pl.BlockSpec((1, 2 * D_FF), const),
                  pl.BlockSpec((D_FF, D_MODEL), const, pipeline_mode=once)] + cast_specs,
        out_specs=[pl.BlockSpec((tm, D_MODEL), lambda i: (i, 0))] + cast_specs,
        scratch_shapes=[pltpu.VMEM((tm + FFN_HALO, kdim), BF16),
                        pltpu.VMEM((tm + FFN_HALO, D_MODEL), BF16), pltpu.VMEM((tm, D_MODEL), F32),
                        pltpu.VMEM((2 * FFN_CHUNK // LANES, tm + FFN_HALO, LANES), F32),
                        pltpu.VMEM((2 * FFN_CHUNK // LANES, tm + FFN_HALO, LANES), F32)],
        compiler_params=_params(("arbitrary",)),
        name="proj_conv_ffn",
    )(x2, x2, x2, o2, o2, o2, w_o, gain.reshape(1, D_MODEL), w_up, conv_w.reshape(3, 2 * D_FF),
      conv_b.reshape(1, 2 * D_FF), w_down, *next_weights)
    return y, cast


def _dup_heads(w, n_heads, dim):
    w = w.reshape(D_MODEL, n_heads, 1, dim)
    return jnp.broadcast_to(w, (D_MODEL, n_heads, 2, dim)).reshape(D_MODEL, 2 * n_heads * dim)


def _a_weight_layout(w_qkv):
    nq = A_HEADS * A_HEAD_DIM
    nk = A_KV_HEADS * A_HEAD_DIM
    return jnp.concatenate([w_qkv[:, :nq], _dup_heads(w_qkv[:, nq:nq + nk], A_KV_HEADS, A_HEAD_DIM),
                            _dup_heads(w_qkv[:, nq + nk:], A_KV_HEADS, A_HEAD_DIM)], axis=1)


def _mixer_a(x2, norm_gain, w, q_gain, k_gain, sink, bias, batch, seq):
    nq = A_HEADS * A_HEAD_DIM
    nk = A_KV_HEADS * A_HEAD_DIM
    colscale = jnp.concatenate([jnp.tile(q_gain, A_HEADS) * (LOG2E * A_HEAD_DIM ** -0.5), jnp.tile(k_gain, 2 * A_KV_HEADS),
                                jnp.ones((2 * nk,), F32)])
    (qkv,) = _project(x2, norm_gain, w, colscale, [2 * D_MODEL], nq + 2 * nk, A_HEAD_DIM)
    return _attention_a(qkv, bias, sink, batch, seq).reshape(batch * seq, D_MODEL)


def _mixer_b(x2, norm_gain, w, q_gain, k_gain, lam_q1, lam_k1, lam_q2, lam_k2, sub_gain, bias,
             lambda_init, batch, seq):
    colscale = jnp.concatenate([jnp.tile(q_gain, 2 * B_HEADS) * (LOG2E * B_HEAD_DIM ** -0.5), jnp.tile(k_gain, 2 * B_HEADS),
                                jnp.ones((D_MODEL,), F32)])
    (qkv,) = _project(x2, norm_gain, w, colscale, [3 * D_MODEL], 2 * D_MODEL, B_HEAD_DIM)
    lam_vecs = jnp.stack([lam_q1, lam_k1, lam_q2, lam_k2])
    return _attention_b(qkv, bias, lam_vecs, sub_gain, lambda_init, batch, seq).reshape(batch * seq, D_MODEL)


def _mixer_c(x2, norm_gain, w, q_gain, k_gain, bias, batch, seq):
    width = C_KV_HEADS * C_HEAD_DIM
    n_groups = len(C_BRANCHES)
    colscale = jnp.concatenate([jnp.tile(q_gain, n_groups * C_KV_HEADS) * (LOG2E * C_HEAD_DIM ** -0.5),
                                jnp.tile(k_gain, C_KV_HEADS), jnp.ones((width,), F32)])
    assert all(window // (2 * dil) == C_HALF for window, dil in C_BRANCHES)
    q, k, v = _project(x2, norm_gain, w, colscale, [n_groups * width, width, width],
                       (n_groups + 1) * width, C_HEAD_DIM, slabs=True)
    return _attention_c(q, k, v, bias, batch, seq)


def _lambda_init(layer):
    return 0.8 - 0.6 * math.exp(-0.3 * layer)


def kernel(x, rel_bias, l0_attn_norm, l0_w_qkv, l0_q_gain, l0_k_gain, l0_sink, l0_w_o, l0_ffn_norm, l0_w_up, l0_conv_w, l0_conv_b, l0_w_down, l1_attn_norm, l1_w_qkv, l1_q_gain, l1_k_gain, l1_lambda_q1, l1_lambda_k1, l1_lambda_q2, l1_lambda_k2, l1_sub_gain, l1_w_o, l1_ffn_norm, l1_w_up, l1_conv_w, l1_conv_b, l1_w_down, l2_attn_norm, l2_w_qkv, l2_q_gain, l2_k_gain, l2_w_o, l2_ffn_norm, l2_w_up, l2_conv_w, l2_conv_b, l2_w_down, l3_attn_norm, l3_w_qkv, l3_q_gain, l3_k_gain, l3_sink, l3_w_o, l3_ffn_norm, l3_w_up, l3_conv_w, l3_conv_b, l3_w_down):
    batch, seq, _ = x.shape
    assert seq % (2 * LANES * C_BRANCHES[-1][1]) == 0
    bias_a = _bias_tiles(rel_bias, A_HEADS, [1], [LANES * d for d in range(-2, 3)], A_WINDOW)
    bias_b = _bias_tiles(rel_bias, 2 * B_HEADS, [1], [LANES * d for d in range(-B_FAR_TILES, B_FAR_TILES + 1)], None)
    bias_c = _bias_tiles(rel_bias, len(C_BRANCHES) * C_KV_HEADS, [dil for _, dil in C_BRANCHES],
                         [C_HALF * d for d in range(-2, 3)], C_HALF)
    bias_b = bias_b.reshape(B_HEADS, 2, 2 * B_FAR_TILES + 1, LANES, LANES)

    w_qkv, w_o, w_up, w_down = (w.astype(BF16) for w in (_a_weight_layout(l0_w_qkv), l0_w_o, l0_w_up, l0_w_down))
    x2 = x.reshape(batch * seq, D_MODEL)
    o = _mixer_a(x2, l0_attn_norm, w_qkv, l0_q_gain, l0_k_gain, l0_sink, bias_a, batch, seq)
    x2, (w_qkv, w_o, w_up, w_down) = _proj_ffn(x2, o, w_o, l0_ffn_norm, w_up, l0_conv_w, l0_conv_b, w_down,
                                               [l1_w_qkv, l1_w_o, l1_w_up, l1_w_down], seq)
    o = _mixer_b(x2, l1_attn_norm, w_qkv, l1_q_gain, l1_k_gain, l1_lambda_q1, l1_lambda_k1, l1_lambda_q2,
                 l1_lambda_k2, l1_sub_gain, bias_b, _lambda_init(1), batch, seq)
    x2, (w_qkv, w_o, w_up, w_down) = _proj_ffn(x2, o, w_o, l1_ffn_norm, w_up, l1_conv_w, l1_conv_b, w_down,
                                               [l2_w_qkv, l2_w_o, l2_w_up, l2_w_down], seq)
    o = _mixer_c(x2, l2_attn_norm, w_qkv, l2_q_gain, l2_k_gain, bias_c, batch, seq)
    x2, (w_qkv, w_o, w_up, w_down) = _proj_ffn(x2, o, w_o, l2_ffn_norm, w_up, l2_conv_w, l2_conv_b, w_down,
                                               [_a_weight_layout(l3_w_qkv), l3_w_o, l3_w_up, l3_w_down], seq)
    o = _mixer_a(x2, l3_attn_norm, w_qkv, l3_q_gain, l3_k_gain, l3_sink, bias_a, batch, seq)
    x2, _ = _proj_ffn(x2, o, w_o, l3_ffn_norm, w_up, l3_conv_w, l3_conv_b, w_down, [], seq)
    return x2.reshape(batch, seq, D_MODEL)
```

```python
import functools
import math

import jax
import jax.numpy as jnp
from jax import lax
from jax.experimental import pallas as pl
from jax.experimental.pallas import tpu as pltpu

F32 = jnp.float32
BF16 = jnp.bfloat16

D_MODEL = 1024
EPS = 1e-6
NEG_INF = -1e30
LOG2E = math.log2(math.e)
LANES = 128
MXU_TILE = 256
VMEM_LIMIT = 56 * 1024 * 1024

NUM_BUCKETS = 32
MAX_DISTANCE = 1024

A_HEADS, A_KV_HEADS, A_HEAD_DIM, A_WINDOW = 16, 4, 64, 128
B_HEADS, B_HEAD_DIM = 8, 64
B_FAR_TILES = 6
C_BRANCHES = ((128, 1), (512, 4), (2048, 16))
C_KV_HEADS, C_HEAD_DIM = 4, 128
C_HALF = 64
D_FF = 2816
FFN_CHUNK = 256
FFN_HALO = 16


def _params(sem, vmem=VMEM_LIMIT):
    return pltpu.CompilerParams(dimension_semantics=sem, vmem_limit_bytes=vmem)


def _rms(x, gain):
    return x * lax.rsqrt(jnp.mean(x * x, axis=-1, keepdims=True) + EPS) * gain


def _bias_tile_kernel(tab_ref, dil_ref, off_ref, out_ref, *, half, heads_per_group):
    grp = pl.program_id(0)
    t = pl.program_id(1)
    dil = dil_ref[grp]
    off = off_ref[t]
    q = lax.broadcasted_iota(jnp.int32, (LANES, LANES), 0)
    c = lax.broadcasted_iota(jnp.int32, (LANES, LANES), 1)
    x = c - q + off
    rel = x * dil
    nb = NUM_BUCKETS // 2
    max_exact = nb // 2
    n = jnp.abs(rel)
    nf = jnp.maximum(n, 1).astype(F32)
    large = max_exact + (jnp.log(nf * (1.0 / max_exact)) / math.log(MAX_DISTANCE / max_exact)
                         * (nb - max_exact)).astype(jnp.int32)
    large = jnp.minimum(large, nb - 1)
    bucket = jnp.where(rel > 0, nb, 0) + jnp.where(n < max_exact, n, large)
    masks = [(bucket & (1 << bit)) != 0 for bit in range(5)]
    inside = None if half is None else jnp.abs(x) <= half
    for hh in range(heads_per_group):
        col = grp * heads_per_group + hh
        level = [tab_ref[b, col] for b in range(NUM_BUCKETS)]
        for mask in masks:
            level = [jnp.where(mask, level[2 * k + 1], level[2 * k]) for k in range(len(level) // 2)]
        bias = level[0] * LOG2E
        if inside is not None:
            bias = jnp.where(inside, bias, NEG_INF)
        out_ref[hh] = bias


def _bias_tiles(rel_bias, n_heads, group_dils, offs, half):
    n_groups, nt = len(group_dils), len(offs)
    hpg = n_heads // n_groups
    smem = pl.BlockSpec(memory_space=pltpu.SMEM)
    return pl.pallas_call(
        functools.partial(_bias_tile_kernel, half=half, heads_per_group=hpg),
        out_shape=jax.ShapeDtypeStruct((n_heads, nt, LANES, LANES), F32),
        grid=(n_groups, nt),
        in_specs=[smem, smem, smem],
        out_specs=pl.BlockSpec((hpg, None, LANES, LANES), lambda g, t: (g, t, 0, 0)),
        compiler_params=_params(("arbitrary", "arbitrary")),
        name="bias_tiles",
    )(rel_bias, jnp.asarray(group_dils, jnp.int32), jnp.asarray(offs, jnp.int32))


def _proj_kernel(x_ref, g_ref, w_ref, cs_ref, p_ref, *out_refs, widths, n_norm_cols, group, slabs):
    h = _rms(x_ref[...], g_ref[...]).astype(BF16)
    group_w = 2 * MXU_TILE
    tiles = [(o_ref, off) for o_ref, width in zip(out_refs, widths) for off in range(0, width, group_w)]

    def project(t):
        return jnp.dot(h, w_ref[:, t * group_w:(t + 1) * group_w], preferred_element_type=F32)

    def finish(t, y):
        o_ref, off = tiles[t]
        for s in range(group_w // MXU_TILE):
            col = t * group_w + s * MXU_TILE
            ys = y[:, s * MXU_TILE:(s + 1) * MXU_TILE]
            if col < n_norm_cols:
                ss = jnp.dot((ys * ys).astype(BF16), p_ref[...], preferred_element_type=F32)
                ys = ys * lax.rsqrt(ss * (1.0 / group) + EPS) * cs_ref[:, col:col + MXU_TILE]
            lo = off + s * MXU_TILE
            if slabs:
                for half in range(MXU_TILE // LANES):
                    o_ref[lo // LANES + half] = ys[:, half * LANES:(half + 1) * LANES]
            else:
                o_ref[:, lo:lo + MXU_TILE] = ys.astype(BF16)

    y = project(0)
    for t in range(len(tiles)):
        y_next = project(t + 1) if t + 1 < len(tiles) else None
        finish(t, y)
        y = y_next


def _project(x2, gain, w, colscale, widths, n_norm_cols, group, slabs=False, tm=1024):
    m = x2.shape[0]
    n = w.shape[1]
    assert sum(widths) == n and m % tm == 0 and n_norm_cols % MXU_TILE == 0
    assert all(wd % (2 * MXU_TILE) == 0 for wd in widths)
    r = jnp.arange(MXU_TILE) // group
    ones_bd = (r[:, None] == r[None, :]).astype(BF16)
    const = lambda i: (0, 0)
    if slabs:
        out_shape = [jax.ShapeDtypeStruct((wd // LANES, m, LANES), F32) for wd in widths]
        out_specs = [pl.BlockSpec((wd // LANES, tm, LANES), lambda i: (0, i, 0)) for wd in widths]
    else:
        out_shape = [jax.ShapeDtypeStruct((m, wd), BF16) for wd in widths]
        out_specs = [pl.BlockSpec((tm, wd), lambda i: (i, 0)) for wd in widths]
    return pl.pallas_call(
        functools.partial(_proj_kernel, widths=tuple(widths), n_norm_cols=n_norm_cols, group=group, slabs=slabs),
        out_shape=out_shape,
        grid=(m // tm,),
        in_specs=[pl.BlockSpec((tm, D_MODEL), lambda i: (i, 0)),
                  pl.BlockSpec((1, D_MODEL), const),
                  pl.BlockSpec((D_MODEL, n), const),
                  pl.BlockSpec((1, n), const),
                  pl.BlockSpec((MXU_TILE, MXU_TILE), const)],
        out_specs=out_specs,
        compiler_params=_params(("arbitrary",)),
        name="qkv_proj",
    )(x2, gain.reshape(1, D_MODEL), w, colscale.reshape(1, n), ones_bd)


def _attn_a_kernel(sink_ref, q_ref, k_ref, v_ref, bias_ref, o_ref, *, nq, n_tiles):
    i = pl.program_id(1)
    lane = lax.broadcasted_iota(jnp.int32, (LANES, LANES), 1)
    lo = lane < A_HEAD_DIM
    lo3 = lax.broadcasted_iota(jnp.int32, (3 * LANES, LANES), 1) < A_HEAD_DIM
    group = A_HEADS // A_KV_HEADS

    def window(s):
        t = i * nq + s
        base = jnp.clip(t - 1, 0, n_tiles - 3)
        return t, base, pl.ds(pl.multiple_of(base * LANES, LANES), 3 * LANES)

    def scores(s, hk):
        rows = slice(s * LANES, (s + 1) * LANES)
        parts = []
        for c in (2 * hk, 2 * hk + 1):
            qc = q_ref[rows, c * LANES:(c + 1) * LANES]
            parts += [jnp.where(lo, qc, jnp.zeros_like(qc)), jnp.where(lo, jnp.zeros_like(qc), qc)]
        qs = jnp.concatenate(parts, axis=0)
        kw = k_ref[window(s)[2], hk * LANES:(hk + 1) * LANES]
        return lax.dot_general(qs, kw, (((1,), (1,)), ((), ())), preferred_element_type=F32)

    def finish(s, hk, sc):
        t, base, krows = window(s)
        rows = slice(s * LANES, (s + 1) * LANES)
        blocks = []
        for g in range(group):
            hq = hk * group + g
            blocks.append(jnp.concatenate(
                [sc[g * LANES:(g + 1) * LANES, w * LANES:(w + 1) * LANES] + bias_ref[hq, base + w - t + 2]
                 for w in range(3)], axis=1))
        mxs = [jnp.max(b, axis=-1, keepdims=True) for b in blocks]
        ps = [jnp.exp2((b - m).astype(BF16)) for b, m in zip(blocks, mxs)]
        vw = v_ref[krows, hk * LANES:(hk + 1) * LANES]
        one = jnp.ones_like(vw)
        pv_even = jnp.dot(jnp.concatenate([ps[0], ps[2]], axis=0), jnp.where(lo3, vw, one),
                          preferred_element_type=F32)
        pv_odd = jnp.dot(jnp.concatenate([ps[1], ps[3]], axis=0), jnp.where(lo3, one, vw),
                         preferred_element_type=F32)
        for pair in range(group // 2):
            he, ho = hk * group + 2 * pair, hk * group + 2 * pair + 1
            pr = slice(pair * LANES, (pair + 1) * LANES)
            num = jnp.where(lo, pv_even[pr], pv_odd[pr])
            den = pltpu.roll(jnp.where(lo, pv_odd[pr], pv_even[pr]), A_HEAD_DIM, 1)
            snk = jnp.exp2(jnp.where(lo, sink_ref[he] * LOG2E - mxs[2 * pair], sink_ref[ho] * LOG2E - mxs[2 * pair + 1]))
            chunk = 2 * hk + pair
            o_ref[rows, chunk * LANES:(chunk + 1) * LANES] = (num / (den + snk)).astype(BF16)

    blocks_todo = [(s, hk) for s in range(nq) for hk in range(A_KV_HEADS)]
    sc = scores(*blocks_todo[0])
    for n, blk in enumerate(blocks_todo):
        sc_next = scores(*blocks_todo[n + 1]) if n + 1 < len(blocks_todo) else None
        finish(*blk, sc)
        sc = sc_next


def _attention_a(qkv, bias, sink, batch, seq, tq=512):
    nq = tq // LANES
    n_tiles = seq // LANES
    qkv3 = qkv.reshape(batch, seq, 2 * D_MODEL)
    return pl.pallas_call(
        functools.partial(_attn_a_kernel, nq=nq, n_tiles=n_tiles),
        out_shape=jax.ShapeDtypeStruct((batch, seq, D_MODEL), BF16),
        grid=(batch, seq // tq),
        in_specs=[pl.BlockSpec(memory_space=pltpu.SMEM),
                  pl.BlockSpec((None, tq, D_MODEL), lambda b, i: (b, i, 0)),
                  pl.BlockSpec((None, seq, 512), lambda b, i: (b, 0, 2)),
                  pl.BlockSpec((None, seq, 512), lambda b, i: (b, 0, 3)),
                  pl.BlockSpec((A_HEADS, 5, LANES, LANES), lambda b, i: (0, 0, 0, 0))],
        out_specs=pl.BlockSpec((None, tq, D_MODEL), lambda b, i: (b, i, 0)),
        compiler_params=_params(("arbitrary", "arbitrary")),
        name="attn_window",
    )(sink, qkv3, qkv3, qkv3, bias)


def _attn_b_kernel(lam_ref, sg_ref, q_ref, k_ref, v_ref, bias_ref, o_ref, qs_ref, m_ref, acc_ref,
                   *, tq, tk, n_kc, lambda_init):
    i = pl.program_id(2)
    nqt = tq // LANES
    nkt = tk // LANES
    lo = lax.broadcasted_iota(jnp.int32, (tq, LANES), 1) < B_HEAD_DIM
    q = q_ref[...]
    zero = jnp.zeros_like(q)
    qs_ref[0:tq, :] = jnp.where(lo, q, zero)
    qs_ref[tq:2 * tq, :] = jnp.where(lo, zero, q)
    m_ref[...] = jnp.full(m_ref.shape, -jnp.inf, F32)
    acc_ref[...] = jnp.zeros(acc_ref.shape, F32)

    def map_chunk(mp, kc):
        mrows = slice(mp * tq, (mp + 1) * tq)
        keys = slice(kc * tk, (kc + 1) * tk)
        sc = lax.dot_general(qs_ref[mrows, :], k_ref[keys, :], (((1,), (1,)), ((), ())),
                             preferred_element_type=F32)
        probs = []
        for a in range(nqt):
            rows = slice(mp * tq + a * LANES, mp * tq + (a + 1) * LANES)
            tiles = []
            for w in range(nkt):
                d = jnp.clip(kc * nkt + w - (i * nqt + a), -B_FAR_TILES, B_FAR_TILES) + B_FAR_TILES
                tiles.append(sc[a * LANES:(a + 1) * LANES, w * LANES:(w + 1) * LANES] + bias_ref[mp, d])
            m_old = m_ref[rows, :]
            m_new = jnp.maximum(m_old, jnp.max(functools.reduce(jnp.maximum, tiles), axis=-1, keepdims=True))
            alpha = jnp.exp2(m_old - m_new)
            p = [jnp.exp2((t - m_new).astype(BF16)) for t in tiles]
            acc_ref[rows, :] = jnp.concatenate([alpha, alpha], axis=1) * acc_ref[rows, :]
            m_ref[rows, :] = m_new
            probs.append(jnp.concatenate(p, axis=1))
        vw = v_ref[keys, :]
        acc_ref[mrows, :] += jnp.dot(jnp.concatenate(probs, axis=0), jnp.concatenate([vw, jnp.ones_like(vw)], axis=1),
                                     preferred_element_type=F32)

    for kc in range(n_kc):
        for mp in range(2):
            map_chunk(mp, kc)
    lv = lam_ref[...]
    lam = (jnp.exp(jnp.sum(lv[0:1] * lv[1:2], axis=-1, keepdims=True))
           - jnp.exp(jnp.sum(lv[2:3] * lv[3:4], axis=-1, keepdims=True)) + lambda_init)
    o = acc_ref[:, 0:LANES] / acc_ref[:, LANES:2 * LANES]
    o = o[:tq] - lam * o[tq:]
    o_ref[...] = (_rms(o, sg_ref[...]) * (1.0 - lambda_init)).astype(BF16)


def _attention_b(qkv, bias, lam_vecs, sub_gain, lambda_init, batch, seq, tq=2048, tk=256):
    qkv3 = qkv.reshape(batch, seq, 3 * D_MODEL)
    nb = 2 * B_FAR_TILES + 1
    return pl.pallas_call(
        functools.partial(_attn_b_kernel, tq=tq, tk=tk, n_kc=seq // tk, lambda_init=lambda_init),
        out_shape=jax.ShapeDtypeStruct((batch, seq, D_MODEL), BF16),
        grid=(batch, B_HEADS, seq // tq),
        in_specs=[pl.BlockSpec((4, B_HEAD_DIM), lambda b, h, i: (0, 0)),
                  pl.BlockSpec((1, LANES), lambda b, h, i: (0, 0)),
                  pl.BlockSpec((None, tq, LANES), lambda b, h, i: (b, i, h)),
                  pl.BlockSpec((None, seq, LANES), lambda b, h, i: (b, 0, B_HEADS + h)),
                  pl.BlockSpec((None, seq, LANES), lambda b, h, i: (b, 0, 2 * B_HEADS + h)),
                  pl.BlockSpec((None, 2, nb, LANES, LANES), lambda b, h, i: (h, 0, 0, 0, 0))],
        out_specs=pl.BlockSpec((None, tq, LANES), lambda b, h, i: (b, i, h)),
        scratch_shapes=[pltpu.VMEM((2 * tq, LANES), BF16), pltpu.VMEM((2 * tq, LANES), F32),
                        pltpu.VMEM((2 * tq, 2 * LANES), F32)],
        compiler_params=_params(("arbitrary", "arbitrary", "arbitrary")),
        name="attn_diff",
    )(lam_vecs, sub_gain.reshape(1, LANES), qkv3, qkv3, qkv3, bias)


def _attn_c_kernel(q_ref, k_ref, v_ref, bias_ref, o_ref, og_ref, lg_ref, *, seq, npos, merge_rows):
    c = pl.program_id(2)

    def rows_of(start, n, dil):
        return pl.ds(start, n) if dil == 1 else pl.ds(start, n, stride=dil)

    def window(g, r, tt):
        dil = C_BRANCHES[g][1]
        t0 = c * (npos // dil) + tt * LANES
        ws = jnp.clip(t0 - C_HALF, 0, seq // dil - 2 * LANES)
        return t0, ws, rows_of(r + dil * LANES * tt, LANES, dil), rows_of(r + dil * ws, 2 * LANES, dil)

    def scores(g, r, tt):
        _, _, qrows, krows = window(g, r, tt)
        return lax.dot_general(q_ref[g, qrows, :].astype(BF16), k_ref[krows, :].astype(BF16),
                               (((1,), (1,)), ((), ())), preferred_element_type=F32)

    def finish(g, r, tt, sc):
        t0, ws, qrows, krows = window(g, r, tt)
        sc = jnp.concatenate(
            [sc[:, w * LANES:(w + 1) * LANES] + bias_ref[g, (ws + w * LANES - t0) // C_HALF + 2]
             for w in range(2)], axis=1)
        mx = jnp.max(sc, axis=-1, keepdims=True)
        p = jnp.exp2((sc - mx).astype(BF16))
        vw = v_ref[krows, :].astype(BF16)
        pv = jnp.dot(p, jnp.concatenate([vw, jnp.ones_like(vw)], axis=1), preferred_element_type=F32)
        den = pv[:, LANES:2 * LANES]
        og_ref[g, qrows, :] = pv[:, 0:LANES] / den
        lg_ref[g, qrows, :] = mx + jnp.log2(den)

    blocks_todo = [(g, r, tt) for g, (_, dil) in enumerate(C_BRANCHES) for r in range(dil)
                   for tt in range(npos // (dil * LANES))]
    sc = scores(*blocks_todo[0])
    for n, blk in enumerate(blocks_todo):
        sc_next = scores(*blocks_todo[n + 1]) if n + 1 < len(blocks_todo) else None
        finish(*blk, sc)
        sc = sc_next
    for blk in range(npos // merge_rows):
        rows = slice(blk * merge_rows, (blk + 1) * merge_rows)
        lses = [lg_ref[g, rows, :] for g in range(len(C_BRANCHES))]
        mx = functools.reduce(jnp.maximum, lses)
        es = [jnp.exp2(l - mx) for l in lses]
        num = functools.reduce(jnp.add, [e * og_ref[g, rows, :] for g, e in enumerate(es)])
        o_ref[rows, :] = (num / functools.reduce(jnp.add, es)).astype(BF16)


def _attention_c(q, k, v, bias, batch, seq, npos=4096, merge_rows=256):
    n_groups = len(C_BRANCHES)
    m = batch * seq
    chunks = seq // npos
    q4 = q.reshape(n_groups, C_KV_HEADS, m, LANES)
    bias5 = bias.reshape(n_groups, C_KV_HEADS, 5, LANES, LANES)
    return pl.pallas_call(
        functools.partial(_attn_c_kernel, seq=seq, npos=npos, merge_rows=merge_rows),
        out_shape=jax.ShapeDtypeStruct((m, C_KV_HEADS * C_HEAD_DIM), BF16),
        grid=(batch, C_KV_HEADS, chunks),
        in_specs=[pl.BlockSpec((n_groups, None, npos, LANES), lambda b, j, c: (0, j, b * chunks + c, 0)),
                  pl.BlockSpec((None, seq, LANES), lambda b, j, c: (j, b, 0)),
                  pl.BlockSpec((None, seq, LANES), lambda b, j, c: (j, b, 0)),
                  pl.BlockSpec((n_groups, None, 5, LANES, LANES), lambda b, j, c: (0, j, 0, 0, 0))],
        out_specs=pl.BlockSpec((npos, LANES), lambda b, j, c: (b * chunks + c, j)),
        scratch_shapes=[pltpu.VMEM((n_groups, npos, LANES), F32), pltpu.VMEM((n_groups, npos, LANES), F32)],
        compiler_params=_params(("arbitrary", "arbitrary", "arbitrary")),
        name="attn_dilated",
    )(q4, k, v, bias5)


def _ffn_kernel(x_ref, xn_ref, xp_ref, o_ref, on_ref, op_ref, wo_ref, g_ref, wup_ref, cw_ref, cb_ref, wdn_ref,
                *rest, tm, tiles_per_seq, n_chunks, cast_blocks):
    n_cast = len(cast_blocks)
    cast_in, y_ref, cast_out = rest[:n_cast], rest[n_cast], rest[n_cast + 1:2 * n_cast + 1]
    oe_ref, h_ref, acc_ref, ua_ref, ub_ref = rest[2 * n_cast + 1:]
    i = pl.program_id(0)
    for src, dst, blocks in zip(cast_in, cast_out, cast_blocks):
        @pl.when(i < blocks)
        def _():
            dst[...] = src[...].astype(BF16)
    pos = i % tiles_per_seq
    kdim = o_ref.shape[1]
    halo_row = lax.broadcasted_iota(jnp.int32, (FFN_HALO, kdim), 0)
    oe_ref[0:tm, :] = o_ref[...]
    oe_ref[tm:tm + FFN_HALO, :] = jnp.where(halo_row < 8, on_ref[...], op_ref[...])
    attn = jnp.dot(oe_ref[...], wo_ref[...], preferred_element_type=F32)
    x1 = x_ref[...] + attn[0:tm]
    x1n = xn_ref[...] + attn[tm:tm + 8]
    x1p = xp_ref[...] + attn[tm + 8:tm + FFN_HALO]
    g = g_ref[...]
    h_ref[0:tm, :] = _rms(x1, g).astype(BF16)
    hn = jnp.where(pos == tiles_per_seq - 1, 0.0, _rms(x1n, g))
    hp = jnp.where(pos == 0, 0.0, _rms(x1p, g))
    h_ref[tm:tm + FFN_HALO, :] = jnp.concatenate([hn, hp], axis=0).astype(BF16)
    acc_ref[...] = x1
    slabs = FFN_CHUNK // LANES

    def chunk_cols(k):
        return slice(k * FFN_CHUNK, (k + 1) * FFN_CHUNK)

    def up_proj(j, u_ref):
        h = h_ref[...]
        for half, k in enumerate((j, j + n_chunks)):
            u = jnp.dot(h, wup_ref[:, chunk_cols(k)], preferred_element_type=F32)
            for s in range(slabs):
                cols = slice(s * LANES, (s + 1) * LANES)
                slab = half * slabs + s
                u_ref[slab, 8:8 + tm, :] = u[0:tm, cols]
                u_ref[slab, 0:8, :] = u[tm + 8:tm + 16, cols]
                u_ref[slab, tm + 8:tm + 16, :] = u[tm:tm + 8, cols]

    def conv(u_ref, slab, k, cols):
        w = cw_ref[:, chunk_cols(k)]
        return (w[0:1, cols] * u_ref[slab, pl.ds(7, tm), :] + w[1:2, cols] * u_ref[slab, pl.ds(8, tm), :]
                + w[2:3, cols] * u_ref[slab, pl.ds(9, tm), :] + cb_ref[:, chunk_cols(k)][:, cols])

    def gate(j, u_ref):
        acts = []
        for s in range(slabs):
            cols = slice(s * LANES, (s + 1) * LANES)
            g_ = conv(u_ref, s, j, cols)
            val = conv(u_ref, slabs + s, j + n_chunks, cols)
            acts.append((g_ / (1.0 + jnp.exp(-g_)) * val).astype(BF16))
        return acts

    def down(j, acts):
        acc_ref[...] += jnp.dot(jnp.concatenate(acts, axis=1), wdn_ref[chunk_cols(j), :],
                                preferred_element_type=F32)

    bufs = (ua_ref, ub_ref)
    up_proj(0, bufs[0])
    pending = None
    for j in range(n_chunks):
        if j + 1 < n_chunks:
            up_proj(j + 1, bufs[(j + 1) % 2])
        if pending is not None:
            down(j - 1, pending)
        pending = gate(j, bufs[j % 2])
    down(n_chunks - 1, pending)
    y_ref[...] = acc_ref[...]


def _cast_block_rows(rows, n_steps):
    return next(rb for rb in range(16, rows + 1, 16) if rows % rb == 0 and rows // rb <= n_steps)


def _proj_ffn(x2, o2, w_o, gain, w_up, conv_w, conv_b, w_down, next_weights, seq, tm=512):
    m = x2.shape[0]
    kdim = o2.shape[1]
    n_chunks = D_FF // FFN_CHUNK
    tiles_per_seq = seq // tm
    n_steps = m // tm
    x_halo = tm // 8
    o_halo = tm // FFN_HALO
    const = lambda i: (0, 0)
    once = pl.Buffered(1)
    nxt = lambda blocks, per_tile: (lambda i: (jnp.minimum((i + 1) * per_tile, blocks - 1), 0))
    prv = lambda per_tile: (lambda i: (jnp.maximum(i * per_tile - 1, 0), 0))
    cast_rows = [_cast_block_rows(w.shape[0], n_steps) for w in next_weights]
    cast_blocks = tuple(w.shape[0] // rb for w, rb in zip(next_weights, cast_rows))
    cast_specs = [pl.BlockSpec((rb, w.shape[1]), (lambda i, nb=nb: (jnp.minimum(i, nb - 1), 0)))
                  for w, rb, nb in zip(next_weights, cast_rows, cast_blocks)]
    y, *cast = pl.pallas_call(
        functools.partial(_ffn_kernel, tm=tm, tiles_per_seq=tiles_per_seq, n_chunks=n_chunks,
                          cast_blocks=cast_blocks),
        out_shape=[jax.ShapeDtypeStruct((m, D_MODEL), F32)]
                  + [jax.ShapeDtypeStruct(w.shape, BF16) for w in next_weights],
        grid=(n_steps,),
        in_specs=[pl.BlockSpec((tm, D_MODEL), lambda i: (i, 0)),
                  pl.BlockSpec((8, D_MODEL), nxt(m // 8, x_halo)),
                  pl.BlockSpec((8, D_MODEL), prv(x_halo)),
                  pl.BlockSpec((tm, kdim), lambda i: (i, 0)),
                  pl.BlockSpec((FFN_HALO, kdim), nxt(m // FFN_HALO, o_halo)),
                  pl.BlockSpec((FFN_HALO, kdim), prv(o_halo)),
                  pl.BlockSpec((kdim, D_MODEL), const, pipeline_mode=once),
                  pl.BlockSpec((1, D_MODEL), const),
                  pl.BlockSpec((D_MODEL, 2 * D_FF), const, pipeline_mode=once),
                  pl.BlockSpec((3, 2 * D_FF), const),
                  pl.BlockSpec((1, 2 * D_FF), const),
                  pl.BlockSpec((D_FF, D_MODEL), const, pipeline_mode=once)] + cast_specs,
        out_specs=[pl.BlockSpec((tm, D_MODEL), lambda i: (i, 0))] + cast_specs,
        scratch_shapes=[pltpu.VMEM((tm + FFN_HALO, kdim), BF16),
                        pltpu.VMEM((tm + FFN_HALO, D_MODEL), BF16), pltpu.VMEM((tm, D_MODEL), F32),
                        pltpu.VMEM((2 * FFN_CHUNK // LANES, tm + FFN_HALO, LANES), F32),
                        pltpu.VMEM((2 * FFN_CHUNK // LANES, tm + FFN_HALO, LANES), F32)],
        compiler_params=_params(("arbitrary",)),
        name="proj_conv_ffn",
    )(x2, x2, x2, o2, o2, o2, w_o, gain.reshape(1, D_MODEL), w_up, conv_w.reshape(3, 2 * D_FF),
      conv_b.reshape(1, 2 * D_FF), w_down, *next_weights)
    return y, cast


def _dup_heads(w, n_heads, dim):
    w = w.reshape(D_MODEL, n_heads, 1, dim)
    return jnp.broadcast_to(w, (D_MODEL, n_heads, 2, dim)).reshape(D_MODEL, 2 * n_heads * dim)


def _a_weight_layout(w_qkv):
    nq = A_HEADS * A_HEAD_DIM
    nk = A_KV_HEADS * A_HEAD_DIM
    return jnp.concatenate([w_qkv[:, :nq], _dup_heads(w_qkv[:, nq:nq + nk], A_KV_HEADS, A_HEAD_DIM),
                            _dup_heads(w_qkv[:, nq + nk:], A_KV_HEADS, A_HEAD_DIM)], axis=1)


def _mixer_a(x2, norm_gain, w, q_gain, k_gain, sink, bias, batch, seq):
    nq = A_HEADS * A_HEAD_DIM
    nk = A_KV_HEADS * A_HEAD_DIM
    colscale = jnp.concatenate([jnp.tile(q_gain, A_HEADS) * (LOG2E * A_HEAD_DIM ** -0.5), jnp.tile(k_gain, 2 * A_KV_HEADS),
                                jnp.ones((2 * nk,), F32)])
    (qkv,) = _project(x2, norm_gain, w, colscale, [2 * D_MODEL], nq + 2 * nk, A_HEAD_DIM)
    return _attention_a(qkv, bias, sink, batch, seq).reshape(batch * seq, D_MODEL)


def _mixer_b(x2, norm_gain, w, q_gain, k_gain, lam_q1, lam_k1, lam_q2, lam_k2, sub_gain, bias,
             lambda_init, batch, seq):
    colscale = jnp.concatenate([jnp.tile(q_gain, 2 * B_HEADS) * (LOG2E * B_HEAD_DIM ** -0.5), jnp.tile(k_gain, 2 * B_HEADS),
                                jnp.ones((D_MODEL,), F32)])
    (qkv,) = _project(x2, norm_gain, w, colscale, [3 * D_MODEL], 2 * D_MODEL, B_HEAD_DIM)
    lam_vecs = jnp.stack([lam_q1, lam_k1, lam_q2, lam_k2])
    return _attention_b(qkv, bias, lam_vecs, sub_gain, lambda_init, batch, seq).reshape(batch * seq, D_MODEL)


def _mixer_c(x2, norm_gain, w, q_gain, k_gain, bias, batch, seq):
    width = C_KV_HEADS * C_HEAD_DIM
    n_groups = len(C_BRANCHES)
    colscale = jnp.concatenate([jnp.tile(q_gain, n_groups * C_KV_HEADS) * (LOG2E * C_HEAD_DIM ** -0.5),
                                jnp.tile(k_gain, C_KV_HEADS), jnp.ones((width,), F32)])
    assert all(window // (2 * dil) == C_HALF for window, dil in C_BRANCHES)
    q, k, v = _project(x2, norm_gain, w, colscale, [n_groups * width, width, width],
                       (n_groups + 1) * width, C_HEAD_DIM, slabs=True)
    return _attention_c(q, k, v, bias, batch, seq)


def _lambda_init(layer):
    return 0.8 - 0.6 * math.exp(-0.3 * layer)


def kernel(x, rel_bias, l0_attn_norm, l0_w_qkv, l0_q_gain, l0_k_gain, l0_sink, l0_w_o, l0_ffn_norm, l0_w_up, l0_conv_w, l0_conv_b, l0_w_down, l1_attn_norm, l1_w_qkv, l1_q_gain, l1_k_gain, l1_lambda_q1, l1_lambda_k1, l1_lambda_q2, l1_lambda_k2, l1_sub_gain, l1_w_o, l1_ffn_norm, l1_w_up, l1_conv_w, l1_conv_b, l1_w_down, l2_attn_norm, l2_w_qkv, l2_q_gain, l2_k_gain, l2_w_o, l2_ffn_norm, l2_w_up, l2_conv_w, l2_conv_b, l2_w_down, l3_attn_norm, l3_w_qkv, l3_q_gain, l3_k_gain, l3_sink, l3_w_o, l3_ffn_norm, l3_w_up, l3_conv_w, l3_conv_b, l3_w_down):
    batch, seq, _ = x.shape
    assert seq % (2 * LANES * C_BRANCHES[-1][1]) == 0
    bias_a = _bias_tiles(rel_bias, A_HEADS, [1], [LANES * d for d in range(-2, 3)], A_WINDOW)
    bias_b = _bias_tiles(rel_bias, 2 * B_HEADS, [1], [LANES * d for d in range(-B_FAR_TILES, B_FAR_TILES + 1)], None)
    bias_c = _bias_tiles(rel_bias, len(C_BRANCHES) * C_KV_HEADS, [dil for _, dil in C_BRANCHES],
                         [C_HALF * d for d in range(-2, 3)], C_HALF)
    bias_b = bias_b.reshape(B_HEADS, 2, 2 * B_FAR_TILES + 1, LANES, LANES)

    w_qkv, w_o, w_up, w_down = (w.astype(BF16) for w in (_a_weight_layout(l0_w_qkv), l0_w_o, l0_w_up, l0_w_down))
    x2 = x.reshape(batch * seq, D_MODEL)
    o = _mixer_a(x2, l0_attn_norm, w_qkv, l0_q_gain, l0_k_gain, l0_sink, bias_a, batch, seq)
    x2, (w_qkv, w_o, w_up, w_down) = _proj_ffn(x2, o, w_o, l0_ffn_norm, w_up, l0_conv_w, l0_conv_b, w_down,
                                               [l1_w_qkv, l1_w_o, l1_w_up, l1_w_down], seq)
    o = _mixer_b(x2, l1_attn_norm, w_qkv, l1_q_gain, l1_k_gain, l1_lambda_q1, l1_lambda_k1, l1_lambda_q2,
                 l1_lambda_k2, l1_sub_gain, bias_b, _lambda_init(1), batch, seq)
    x2, (w_qkv, w_o, w_up, w_down) = _proj_ffn(x2, o, w_o, l1_ffn_norm, w_up, l1_conv_w, l1_conv_b, w_down,
                                               [l2_w_qkv, l2_w_o, l2_w_up, l2_w_down], seq)
    o = _mixer_c(x2, l2_attn_norm, w_qkv, l2_q_gain, l2_k_gain, bias_c, batch, seq)
    x2, (w_qkv, w_o, w_up, w_down) = _proj_ffn(x2, o, w_o, l2_ffn_norm, w_up, l2_conv_w, l2_conv_b, w_down,
                                               [_a_weight_layout(l3_w_qkv), l3_w_o, l3_w_up, l3_w_down], seq)
    o = _mixer_a(x2, l3_attn_norm, w_qkv, l3_q_gain, l3_k_gain, l3_sink, bias_a, batch, seq)
    x2, _ = _proj_ffn(x2, o, w_o, l3_ffn_norm, w_up, l3_conv_w, l3_conv_b, w_down, [], seq)
    return x2.reshape(batch, seq, D_MODEL)
```

```python
import functools
import math

import jax
import jax.numpy as jnp
from jax import lax
from jax.experimental import pallas as pl
from jax.experimental.pallas import tpu as pltpu

F32 = jnp.float32
BF16 = jnp.bfloat16

D_MODEL = 1024
EPS = 1e-6
NEG_INF = -1e30
LOG2E = math.log2(math.e)
LANES = 128
MXU_TILE = 256
VMEM_LIMIT = 56 * 1024 * 1024

NUM_BUCKETS = 32
MAX_DISTANCE = 1024

A_HEADS, A_KV_HEADS, A_HEAD_DIM, A_WINDOW = 16, 4, 64, 128
B_HEADS, B_HEAD_DIM = 8, 64
B_FAR_TILES = 6
C_BRANCHES = ((128, 1), (512, 4), (2048, 16))
C_KV_HEADS, C_HEAD_DIM = 4, 128
C_HALF = 64
D_FF = 2816
FFN_CHUNK = 256
FFN_HALO = 16


def _params(sem, vmem=VMEM_LIMIT):
    return pltpu.CompilerParams(dimension_semantics=sem, vmem_limit_bytes=vmem)


def _rms(x, gain):
    return x * lax.rsqrt(jnp.mean(x * x, axis=-1, keepdims=True) + EPS) * gain


def _bias_tile_kernel(tab_ref, dil_ref, off_ref, out_ref, *, half, heads_per_group):
    grp = pl.program_id(0)
    t = pl.program_id(1)
    dil = dil_ref[grp]
    off = off_ref[t]
    q = lax.broadcasted_iota(jnp.int32, (LANES, LANES), 0)
    c = lax.broadcasted_iota(jnp.int32, (LANES, LANES), 1)
    x = c - q + off
    rel = x * dil
    nb = NUM_BUCKETS // 2
    max_exact = nb // 2
    n = jnp.abs(rel)
    nf = jnp.maximum(n, 1).astype(F32)
    large = max_exact + (jnp.log(nf * (1.0 / max_exact)) / math.log(MAX_DISTANCE / max_exact)
                         * (nb - max_exact)).astype(jnp.int32)
    large = jnp.minimum(large, nb - 1)
    bucket = jnp.where(rel > 0, nb, 0) + jnp.where(n < max_exact, n, large)
    masks = [(bucket & (1 << bit)) != 0 for bit in range(5)]
    inside = None if half is None else jnp.abs(x) <= half
    for hh in range(heads_per_group):
        col = grp * heads_per_group + hh
        level = [tab_ref[b, col] for b in range(NUM_BUCKETS)]
        for mask in masks:
            level = [jnp.where(mask, level[2 * k + 1], level[2 * k]) for k in range(len(level) // 2)]
        bias = level[0] * LOG2E
        if inside is not None:
            bias = jnp.where(inside, bias, NEG_INF)
        out_ref[hh] = bias


def _bias_tiles(rel_bias, n_heads, group_dils, offs, half):
    n_groups, nt = len(group_dils), len(offs)
    hpg = n_heads // n_groups
    smem = pl.BlockSpec(memory_space=pltpu.SMEM)
    return pl.pallas_call(
        functools.partial(_bias_tile_kernel, half=half, heads_per_group=hpg),
        out_shape=jax.ShapeDtypeStruct((n_heads, nt, LANES, LANES), F32),
        grid=(n_groups, nt),
        in_specs=[smem, smem, smem],
        out_specs=pl.BlockSpec((hpg, None, LANES, LANES), lambda g, t: (g, t, 0, 0)),
        compiler_params=_params(("arbitrary", "arbitrary")),
        name="bias_tiles",
    )(rel_bias, jnp.asarray(group_dils, jnp.int32), jnp.asarray(offs, jnp.int32))


def _proj_kernel(x_ref, g_ref, w_ref, cs_ref, p_ref, *out_refs, widths, n_norm_cols, group, slabs):
    h = _rms(x_ref[...], g_ref[...]).astype(BF16)
    group_w = 2 * MXU_TILE
    tiles = [(o_ref, off) for o_ref, width in zip(out_refs, widths) for off in range(0, width, group_w)]

    def project(t):
        return jnp.dot(h, w_ref[:, t * group_w:(t + 1) * group_w], preferred_element_type=F32)

    def finish(t, y):
        o_ref, off = tiles[t]
        for s in range(group_w // MXU_TILE):
            col = t * group_w + s * MXU_TILE
            ys = y[:, s * MXU_TILE:(s + 1) * MXU_TILE]
            if col < n_norm_cols:
                ss = jnp.dot((ys * ys).astype(BF16), p_ref[...], preferred_element_type=F32)
                ys = ys * lax.rsqrt(ss * (1.0 / group) + EPS) * cs_ref[:, col:col + MXU_TILE]
            lo = off + s * MXU_TILE
            if slabs:
                for half in range(MXU_TILE // LANES):
                    o_ref[lo // LANES + half] = ys[:, half * LANES:(half + 1) * LANES]
            else:
                o_ref[:, lo:lo + MXU_TILE] = ys.astype(BF16)

    y = project(0)
    for t in range(len(tiles)):
        y_next = project(t + 1) if t + 1 < len(tiles) else None
        finish(t, y)
        y = y_next


def _project(x2, gain, w, colscale, widths, n_norm_cols, group, slabs=False, tm=1024):
    m = x2.shape[0]
    n = w.shape[1]
    assert sum(widths) == n and m % tm == 0 and n_norm_cols % MXU_TILE == 0
    assert all(wd % (2 * MXU_TILE) == 0 for wd in widths)
    r = jnp.arange(MXU_TILE) // group
    ones_bd = (r[:, None] == r[None, :]).astype(BF16)
    const = lambda i: (0, 0)
    if slabs:
        out_shape = [jax.ShapeDtypeStruct((wd // LANES, m, LANES), F32) for wd in widths]
        out_specs = [pl.BlockSpec((wd // LANES, tm, LANES), lambda i: (0, i, 0)) for wd in widths]
    else:
        out_shape = [jax.ShapeDtypeStruct((m, wd), BF16) for wd in widths]
        out_specs = [pl.BlockSpec((tm, wd), lambda i: (i, 0)) for wd in widths]
    return pl.pallas_call(
        functools.partial(_proj_kernel, widths=tuple(widths), n_norm_cols=n_norm_cols, group=group, slabs=slabs),
        out_shape=out_shape,
        grid=(m // tm,),
        in_specs=[pl.BlockSpec((tm, D_MODEL), lambda i: (i, 0)),
                  pl.BlockSpec((1, D_MODEL), const),
                  pl.BlockSpec((D_MODEL, n), const),
                  pl.BlockSpec((1, n), const),
                  pl.BlockSpec((MXU_TILE, MXU_TILE), const)],
        out_specs=out_specs,
        compiler_params=_params(("arbitrary",)),
        name="qkv_proj",
    )(x2, gain.reshape(1, D_MODEL), w, colscale.reshape(1, n), ones_bd)


def _attn_a_kernel(sink_ref, q_ref, k_ref, v_ref, bias_ref, o_ref, *, nq, n_tiles):
    i = pl.program_id(1)
    lane = lax.broadcasted_iota(jnp.int32, (LANES, LANES), 1)
    lo = lane < A_HEAD_DIM
    group = A_HEADS // A_KV_HEADS

    def window(s):
        t = i * nq + s
        base = jnp.clip(t - 1, 0, n_tiles - 3)
        return t, base, pl.ds(pl.multiple_of(base * LANES, LANES), 3 * LANES)

    def scores(s, hk):
        rows = slice(s * LANES, (s + 1) * LANES)
        parts = []
        for c in (2 * hk, 2 * hk + 1):
            qc = q_ref[rows, c * LANES:(c + 1) * LANES]
            parts += [jnp.where(lo, qc, jnp.zeros_like(qc)), jnp.where(lo, jnp.zeros_like(qc), qc)]
        qs = jnp.concatenate(parts, axis=0)
        kw = k_ref[window(s)[2], hk * LANES:(hk + 1) * LANES]
        return lax.dot_general(qs, kw, (((1,), (1,)), ((), ())), preferred_element_type=F32)

    def finish(s, hk, sc):
        t, base, krows = window(s)
        rows = slice(s * LANES, (s + 1) * LANES)
        blocks = []
        for g in range(group):
            hq = hk * group + g
            blocks.append(jnp.concatenate(
                [sc[g * LANES:(g + 1) * LANES, w * LANES:(w + 1) * LANES] + bias_ref[hq, base + w - t + 2]
                 for w in range(3)], axis=1))
        sc = jnp.concatenate(blocks, axis=0)
        mx = jnp.max(sc, axis=-1, keepdims=True)
        p = jnp.exp2(sc - mx)
        den = jnp.sum(p, axis=-1, keepdims=True)
        pv = jnp.dot(p.astype(BF16), v_ref[krows, hk * LANES:(hk + 1) * LANES],
                     preferred_element_type=F32)
        outs = []
        for g in range(group):
            hq = hk * group + g
            gr = slice(g * LANES, (g + 1) * LANES)
            outs.append(pv[gr] / (den[gr] + jnp.exp2(sink_ref[hq] * LOG2E - mx[gr])))
        o_ref[rows, (2 * hk) * LANES:(2 * hk + 1) * LANES] = jnp.where(lo, outs[0], outs[1]).astype(BF16)
        o_ref[rows, (2 * hk + 1) * LANES:(2 * hk + 2) * LANES] = jnp.where(lo, outs[2], outs[3]).astype(BF16)

    blocks_todo = [(s, hk) for s in range(nq) for hk in range(A_KV_HEADS)]
    sc = scores(*blocks_todo[0])
    for n, blk in enumerate(blocks_todo):
        sc_next = scores(*blocks_todo[n + 1]) if n + 1 < len(blocks_todo) else None
        finish(*blk, sc)
        sc = sc_next


def _attention_a(qkv, bias, sink, batch, seq, tq=512):
    nq = tq // LANES
    n_tiles = seq // LANES
    qkv3 = qkv.reshape(batch, seq, 2 * D_MODEL)
    return pl.pallas_call(
        functools.partial(_attn_a_kernel, nq=nq, n_tiles=n_tiles),
        out_shape=jax.ShapeDtypeStruct((batch, seq, D_MODEL), BF16),
        grid=(batch, seq // tq),
        in_specs=[pl.BlockSpec(memory_space=pltpu.SMEM),
                  pl.BlockSpec((None, tq, D_MODEL), lambda b, i: (b, i, 0)),
                  pl.BlockSpec((None, seq, 512), lambda b, i: (b, 0, 2)),
                  pl.BlockSpec((None, seq, 512), lambda b, i: (b, 0, 3)),
                  pl.BlockSpec((A_HEADS, 5, LANES, LANES), lambda b, i: (0, 0, 0, 0))],
        out_specs=pl.BlockSpec((None, tq, D_MODEL), lambda b, i: (b, i, 0)),
        compiler_params=_params(("arbitrary", "arbitrary")),
        name="attn_window",
    )(sink, qkv3, qkv3, qkv3, bias)


def _attn_b_kernel(lam_ref, sg_ref, q_ref, k_ref, v_ref, bias_ref, o_ref, qs_ref, m_ref, acc_ref,
                   *, tq, tk, n_kc, lambda_init):
    i = pl.program_id(2)
    nqt = tq // LANES
    nkt = tk // LANES
    lo = lax.broadcasted_iota(jnp.int32, (tq, LANES), 1) < B_HEAD_DIM
    q = q_ref[...]
    zero = jnp.zeros_like(q)
    qs_ref[0:tq, :] = jnp.where(lo, q, zero)
    qs_ref[tq:2 * tq, :] = jnp.where(lo, zero, q)
    m_ref[...] = jnp.full(m_ref.shape, -jnp.inf, F32)
    acc_ref[...] = jnp.zeros(acc_ref.shape, F32)

    def map_chunk(mp, kc):
        mrows = slice(mp * tq, (mp + 1) * tq)
        keys = slice(kc * tk, (kc + 1) * tk)
        sc = lax.dot_general(qs_ref[mrows, :], k_ref[keys, :], (((1,), (1,)), ((), ())),
                             preferred_element_type=F32)
        probs = []
        for a in range(nqt):
            rows = slice(mp * tq + a * LANES, mp * tq + (a + 1) * LANES)
            tiles = []
            for w in range(nkt):
                d = jnp.clip(kc * nkt + w - (i * nqt + a), -B_FAR_TILES, B_FAR_TILES) + B_FAR_TILES
                tiles.append(sc[a * LANES:(a + 1) * LANES, w * LANES:(w + 1) * LANES] + bias_ref[mp, d])
            m_old = m_ref[rows, :]
            m_new = jnp.maximum(m_old, jnp.max(functools.reduce(jnp.maximum, tiles), axis=-1, keepdims=True))
            alpha = jnp.exp2(m_old - m_new)
            p = [jnp.exp2((t - m_new).astype(BF16)) for t in tiles]
            acc_ref[rows, :] = jnp.concatenate([alpha, alpha], axis=1) * acc_ref[rows, :]
            m_ref[rows, :] = m_new
            probs.append(jnp.concatenate(p, axis=1))
        vw = v_ref[keys, :]
        acc_ref[mrows, :] += jnp.dot(jnp.concatenate(probs, axis=0), jnp.concatenate([vw, jnp.ones_like(vw)], axis=1),
                                     preferred_element_type=F32)

    for kc in range(n_kc):
        for mp in range(2):
            map_chunk(mp, kc)
    lv = lam_ref[...]
    lam = (jnp.exp(jnp.sum(lv[0:1] * lv[1:2], axis=-1, keepdims=True))
           - jnp.exp(jnp.sum(lv[2:3] * lv[3:4], axis=-1, keepdims=True)) + lambda_init)
    o = acc_ref[:, 0:LANES] / acc_ref[:, LANES:2 * LANES]
    o = o[:tq] - lam * o[tq:]
    o_ref[...] = (_rms(o, sg_ref[...]) * (1.0 - lambda_init)).astype(BF16)


def _attention_b(qkv, bias, lam_vecs, sub_gain, lambda_init, batch, seq, tq=2048, tk=256):
    qkv3 = qkv.reshape(batch, seq, 3 * D_MODEL)
    nb = 2 * B_FAR_TILES + 1
    return pl.pallas_call(
        functools.partial(_attn_b_kernel, tq=tq, tk=tk, n_kc=seq // tk, lambda_init=lambda_init),
        out_shape=jax.ShapeDtypeStruct((batch, seq, D_MODEL), BF16),
        grid=(batch, B_HEADS, seq // tq),
        in_specs=[pl.BlockSpec((4, B_HEAD_DIM), lambda b, h, i: (0, 0)),
                  pl.BlockSpec((1, LANES), lambda b, h, i: (0, 0)),
                  pl.BlockSpec((None, tq, LANES), lambda b, h, i: (b, i, h)),
                  pl.BlockSpec((None, seq, LANES), lambda b, h, i: (b, 0, B_HEADS + h)),
                  pl.BlockSpec((None, seq, LANES), lambda b, h, i: (b, 0, 2 * B_HEADS + h)),
                  pl.BlockSpec((None, 2, nb, LANES, LANES), lambda b, h, i: (h, 0, 0, 0, 0))],
        out_specs=pl.BlockSpec((None, tq, LANES), lambda b, h, i: (b, i, h)),
        scratch_shapes=[pltpu.VMEM((2 * tq, LANES), BF16), pltpu.VMEM((2 * tq, LANES), F32),
                        pltpu.VMEM((2 * tq, 2 * LANES), F32)],
        compiler_params=_params(("arbitrary", "arbitrary", "arbitrary")),
        name="attn_diff",
    )(lam_vecs, sub_gain.reshape(1, LANES), qkv3, qkv3, qkv3, bias)


def _attn_c_kernel(q_ref, k_ref, v_ref, bias_ref, o_ref, og_ref, lg_ref, *, seq, npos, merge_rows):
    c = pl.program_id(2)

    def rows_of(start, n, dil):
        return pl.ds(start, n) if dil == 1 else pl.ds(start, n, stride=dil)

    def window(g, r, tt):
        dil = C_BRANCHES[g][1]
        t0 = c * (npos // dil) + tt * LANES
        ws = jnp.clip(t0 - C_HALF, 0, seq // dil - 2 * LANES)
        return t0, ws, rows_of(r + dil * LANES * tt, LANES, dil), rows_of(r + dil * ws, 2 * LANES, dil)

    def scores(g, r, tt):
        _, _, qrows, krows = window(g, r, tt)
        return lax.dot_general(q_ref[g, qrows, :].astype(BF16), k_ref[krows, :].astype(BF16),
                               (((1,), (1,)), ((), ())), preferred_element_type=F32)

    def finish(g, r, tt, sc):
        t0, ws, qrows, krows = window(g, r, tt)
        sc = jnp.concatenate(
            [sc[:, w * LANES:(w + 1) * LANES] + bias_ref[g, (ws + w * LANES - t0) // C_HALF + 2]
             for w in range(2)], axis=1)
        mx = jnp.max(sc, axis=-1, keepdims=True)
        p = jnp.exp2((sc - mx).astype(BF16))
        vw = v_ref[krows, :].astype(BF16)
        pv = jnp.dot(p, jnp.concatenate([vw, jnp.ones_like(vw)], axis=1), preferred_element_type=F32)
        den = pv[:, LANES:2 * LANES]
        og_ref[g, qrows, :] = pv[:, 0:LANES] / den
        lg_ref[g, qrows, :] = mx + jnp.log2(den)

    blocks_todo = [(g, r, tt) for g, (_, dil) in enumerate(C_BRANCHES) for r in range(dil)
                   for tt in range(npos // (dil * LANES))]
    sc = scores(*blocks_todo[0])
    for n, blk in enumerate(blocks_todo):
        sc_next = scores(*blocks_todo[n + 1]) if n + 1 < len(blocks_todo) else None
        finish(*blk, sc)
        sc = sc_next
    for blk in range(npos // merge_rows):
        rows = slice(blk * merge_rows, (blk + 1) * merge_rows)
        lses = [lg_ref[g, rows, :] for g in range(len(C_BRANCHES))]
        mx = functools.reduce(jnp.maximum, lses)
        es = [jnp.exp2(l - mx) for l in lses]
        num = functools.reduce(jnp.add, [e * og_ref[g, rows, :] for g, e in enumerate(es)])
        o_ref[rows, :] = (num / functools.reduce(jnp.add, es)).astype(BF16)


def _attention_c(q, k, v, bias, batch, seq, npos=4096, merge_rows=256):
    n_groups = len(C_BRANCHES)
    m = batch * seq
    chunks = seq // npos
    q4 = q.reshape(n_groups, C_KV_HEADS, m, LANES)
    bias5 = bias.reshape(n_groups, C_KV_HEADS, 5, LANES, LANES)
    return pl.pallas_call(
        functools.partial(_attn_c_kernel, seq=seq, npos=npos, merge_rows=merge_rows),
        out_shape=jax.ShapeDtypeStruct((m, C_KV_HEADS * C_HEAD_DIM), BF16),
        grid=(batch, C_KV_HEADS, chunks),
        in_specs=[pl.BlockSpec((n_groups, None, npos, LANES), lambda b, j, c: (0, j, b * chunks + c, 0)),
                  pl.BlockSpec((None, seq, LANES), lambda b, j, c: (j, b, 0)),
                  pl.BlockSpec((None, seq, LANES), lambda b, j, c: (j, b, 0)),
                  pl.BlockSpec((n_groups, None, 5, LANES, LANES), lambda b, j, c: (0, j, 0, 0, 0))],
        out_specs=pl.BlockSpec((npos, LANES), lambda b, j, c: (b * chunks + c, j)),
        scratch_shapes=[pltpu.VMEM((n_groups, npos, LANES), F32), pltpu.VMEM((n_groups, npos, LANES), F32)],
        compiler_params=_params(("arbitrary", "arbitrary", "arbitrary")),
        name="attn_dilated",
    )(q4, k, v, bias5)


def _ffn_kernel(x_ref, xn_ref, xp_ref, o_ref, on_ref, op_ref, wo_ref, g_ref, wup_ref, cw_ref, cb_ref, wdn_ref,
                *rest, tm, tiles_per_seq, n_chunks, cast_blocks):
    n_cast = len(cast_blocks)
    cast_in, y_ref, cast_out = rest[:n_cast], rest[n_cast], rest[n_cast + 1:2 * n_cast + 1]
    oe_ref, h_ref, acc_ref, ua_ref, ub_ref = rest[2 * n_cast + 1:]
    i = pl.program_id(0)
    for src, dst, blocks in zip(cast_in, cast_out, cast_blocks):
        @pl.when(i < blocks)
        def _():
            dst[...] = src[...].astype(BF16)
    pos = i % tiles_per_seq
    kdim = o_ref.shape[1]
    halo_row = lax.broadcasted_iota(jnp.int32, (FFN_HALO, kdim), 0)
    oe_ref[0:tm, :] = o_ref[...]
    oe_ref[tm:tm + FFN_HALO, :] = jnp.where(halo_row < 8, on_ref[...], op_ref[...])
    attn = jnp.dot(oe_ref[...], wo_ref[...], preferred_element_type=F32)
    x1 = x_ref[...] + attn[0:tm]
    x1n = xn_ref[...] + attn[tm:tm + 8]
    x1p = xp_ref[...] + attn[tm + 8:tm + FFN_HALO]
    g = g_ref[...]
    h_ref[0:tm, :] = _rms(x1, g).astype(BF16)
    hn = jnp.where(pos == tiles_per_seq - 1, 0.0, _rms(x1n, g))
    hp = jnp.where(pos == 0, 0.0, _rms(x1p, g))
    h_ref[tm:tm + FFN_HALO, :] = jnp.concatenate([hn, hp], axis=0).astype(BF16)
    acc_ref[...] = x1
    slabs = FFN_CHUNK // LANES

    def chunk_cols(k):
        return slice(k * FFN_CHUNK, (k + 1) * FFN_CHUNK)

    def up_proj(j, u_ref):
        h = h_ref[...]
        for half, k in enumerate((j, j + n_chunks)):
            u = jnp.dot(h, wup_ref[:, chunk_cols(k)], preferred_element_type=F32)
            for s in range(slabs):
                cols = slice(s * LANES, (s + 1) * LANES)
                slab = half * slabs + s
                u_ref[slab, 8:8 + tm, :] = u[0:tm, cols]
                u_ref[slab, 0:8, :] = u[tm + 8:tm + 16, cols]
                u_ref[slab, tm + 8:tm + 16, :] = u[tm:tm + 8, cols]

    def conv(u_ref, slab, k, cols):
        w = cw_ref[:, chunk_cols(k)]
        return (w[0:1, cols] * u_ref[slab, pl.ds(7, tm), :] + w[1:2, cols] * u_ref[slab, pl.ds(8, tm), :]
                + w[2:3, cols] * u_ref[slab, pl.ds(9, tm), :] + cb_ref[:, chunk_cols(k)][:, cols])

    def gate(j, u_ref):
        acts = []
        for s in range(slabs):
            cols = slice(s * LANES, (s + 1) * LANES)
            g_ = conv(u_ref, s, j, cols)
            val = conv(u_ref, slabs + s, j + n_chunks, cols)
            acts.append((g_ / (1.0 + jnp.exp(-g_)) * val).astype(BF16))
        return acts

    def down(j, acts):
        acc_ref[...] += jnp.dot(jnp.concatenate(acts, axis=1), wdn_ref[chunk_cols(j), :],
                                preferred_element_type=F32)

    bufs = (ua_ref, ub_ref)
    up_proj(0, bufs[0])
    pending = None
    for j in range(n_chunks):
        if j + 1 < n_chunks:
            up_proj(j + 1, bufs[(j + 1) % 2])
        if pending is not None:
            down(j - 1, pending)
        pending = gate(j, bufs[j % 2])
    down(n_chunks - 1, pending)
    y_ref[...] = acc_ref[...]


def _cast_block_rows(rows, n_steps):
    return next(rb for rb in range(16, rows + 1, 16) if rows % rb == 0 and rows // rb <= n_steps)


def _proj_ffn(x2, o2, w_o, gain, w_up, conv_w, conv_b, w_down, next_weights, seq, tm=512):
    m = x2.shape[0]
    kdim = o2.shape[1]
    n_chunks = D_FF // FFN_CHUNK
    tiles_per_seq = seq // tm
    n_steps = m // tm
    x_halo = tm // 8
    o_halo = tm // FFN_HALO
    const = lambda i: (0, 0)
    once = pl.Buffered(1)
    nxt = lambda blocks, per_tile: (lambda i: (jnp.minimum((i + 1) * per_tile, blocks - 1), 0))
    prv = lambda per_tile: (lambda i: (jnp.maximum(i * per_tile - 1, 0), 0))
    cast_rows = [_cast_block_rows(w.shape[0], n_steps) for w in next_weights]
    cast_blocks = tuple(w.shape[0] // rb for w, rb in zip(next_weights, cast_rows))
    cast_specs = [pl.BlockSpec((rb, w.shape[1]), (lambda i, nb=nb: (jnp.minimum(i, nb - 1), 0)))
                  for w, rb, nb in zip(next_weights, cast_rows, cast_blocks)]
    y, *cast = pl.pallas_call(
        functools.partial(_ffn_kernel, tm=tm, tiles_per_seq=tiles_per_seq, n_chunks=n_chunks,
                          cast_blocks=cast_blocks),
        out_shape=[jax.ShapeDtypeStruct((m, D_MODEL), F32)]
                  + [jax.ShapeDtypeStruct(w.shape, BF16) for w in next_weights],
        grid=(n_steps,),
        in_specs=[pl.BlockSpec((tm, D_MODEL), lambda i: (i, 0)),
                  pl.BlockSpec((8, D_MODEL), nxt(m // 8, x_halo)),
                  pl.BlockSpec((8, D_MODEL), prv(x_halo)),
                  pl.BlockSpec((tm, kdim), lambda i: (i, 0)),
                  pl.BlockSpec((FFN_HALO, kdim), nxt(m // FFN_HALO, o_halo)),
                  pl.BlockSpec((FFN_HALO, kdim), prv(o_halo)),
                  pl.BlockSpec((kdim, D_MODEL), const, pipeline_mode=once),
                  pl.BlockSpec((1, D_MODEL), const),
                  pl.BlockSpec((D_MODEL, 2 * D_FF), const, pipeline_mode=once),
                  pl.BlockSpec((3, 2 * D_FF), const),
                  pl.BlockSpec((1, 2 * D_FF), const),
                  pl.BlockSpec((D_FF, D_MODEL), const, pipeline_mode=once)] + cast_specs,
        out_specs=[pl.BlockSpec((tm, D_MODEL), lambda i: (i, 0))] + cast_specs,
        scratch_shapes=[pltpu.VMEM((tm + FFN_HALO, kdim), BF16),
                        pltpu.VMEM((tm + FFN_HALO, D_MODEL), BF16), pltpu.VMEM((tm, D_MODEL), F32),
                        pltpu.VMEM((2 * FFN_CHUNK // LANES, tm + FFN_HALO, LANES), F32),
                        pltpu.VMEM((2 * FFN_CHUNK // LANES, tm + FFN_HALO, LANES), F32)],
        compiler_params=_params(("arbitrary",)),
        name="proj_conv_ffn",
    )(x2, x2, x2, o2, o2, o2, w_o, gain.reshape(1, D_MODEL), w_up, conv_w.reshape(3, 2 * D_FF),
      conv_b.reshape(1, 2 * D_FF), w_down, *next_weights)
    return y, cast


def _dup_heads(w, n_heads, dim):
    w = w.reshape(D_MODEL, n_heads, 1, dim)
    return jnp.broadcast_to(w, (D_MODEL, n_heads, 2, dim)).reshape(D_MODEL, 2 * n_heads * dim)


def _a_weight_layout(w_qkv):
    nq = A_HEADS * A_HEAD_DIM
    nk = A_KV_HEADS * A_HEAD_DIM
    return jnp.concatenate([w_qkv[:, :nq], _dup_heads(w_qkv[:, nq:nq + nk], A_KV_HEADS, A_HEAD_DIM),
                            _dup_heads(w_qkv[:, nq + nk:], A_KV_HEADS, A_HEAD_DIM)], axis=1)


def _mixer_a(x2, norm_gain, w, q_gain, k_gain, sink, bias, batch, seq):
    nq = A_HEADS * A_HEAD_DIM
    nk = A_KV_HEADS * A_HEAD_DIM
    colscale = jnp.concatenate([jnp.tile(q_gain, A_HEADS) * (LOG2E * A_HEAD_DIM ** -0.5), jnp.tile(k_gain, 2 * A_KV_HEADS),
                                jnp.ones((2 * nk,), F32)])
    (qkv,) = _project(x2, norm_gain, w, colscale, [2 * D_MODEL], nq + 2 * nk, A_HEAD_DIM)
    return _attention_a(qkv, bias, sink, batch, seq).reshape(batch * seq, D_MODEL)


def _mixer_b(x2, norm_gain, w, q_gain, k_gain, lam_q1, lam_k1, lam_q2, lam_k2, sub_gain, bias,
             lambda_init, batch, seq):
    colscale = jnp.concatenate([jnp.tile(q_gain, 2 * B_HEADS) * (LOG2E * B_HEAD_DIM ** -0.5), jnp.tile(k_gain, 2 * B_HEADS),
                                jnp.ones((D_MODEL,), F32)])
    (qkv,) = _project(x2, norm_gain, w, colscale, [3 * D_MODEL], 2 * D_MODEL, B_HEAD_DIM)
    lam_vecs = jnp.stack([lam_q1, lam_k1, lam_q2, lam_k2])
    return _attention_b(qkv, bias, lam_vecs, sub_gain, lambda_init, batch, seq).reshape(batch * seq, D_MODEL)


def _mixer_c(x2, norm_gain, w, q_gain, k_gain, bias, batch, seq):
    width = C_KV_HEADS * C_HEAD_DIM
    n_groups = len(C_BRANCHES)
    colscale = jnp.concatenate([jnp.tile(q_gain, n_groups * C_KV_HEADS) * (LOG2E * C_HEAD_DIM ** -0.5),
                                jnp.tile(k_gain, C_KV_HEADS), jnp.ones((width,), F32)])
    assert all(window // (2 * dil) == C_HALF for window, dil in C_BRANCHES)
    q, k, v = _project(x2, norm_gain, w, colscale, [n_groups * width, width, width],
                       (n_groups + 1) * width, C_HEAD_DIM, slabs=True)
    return _attention_c(q, k, v, bias, batch, seq)


def _lambda_init(layer):
    return 0.8 - 0.6 * math.exp(-0.3 * layer)


def kernel(x, rel_bias, l0_attn_norm, l0_w_qkv, l0_q_gain, l0_k_gain, l0_sink, l0_w_o, l0_ffn_norm, l0_w_up, l0_conv_w, l0_conv_b, l0_w_down, l1_attn_norm, l1_w_qkv, l1_q_gain, l1_k_gain, l1_lambda_q1, l1_lambda_k1, l1_lambda_q2, l1_lambda_k2, l1_sub_gain, l1_w_o, l1_ffn_norm, l1_w_up, l1_conv_w, l1_conv_b, l1_w_down, l2_attn_norm, l2_w_qkv, l2_q_gain, l2_k_gain, l2_w_o, l2_ffn_norm, l2_w_up, l2_conv_w, l2_conv_b, l2_w_down, l3_attn_norm, l3_w_qkv, l3_q_gain, l3_k_gain, l3_sink, l3_w_o, l3_ffn_norm, l3_w_up, l3_conv_w, l3_conv_b, l3_w_down):
    batch, seq, _ = x.shape
    assert seq % (2 * LANES * C_BRANCHES[-1][1]) == 0
    bias_a = _bias_tiles(rel_bias, A_HEADS, [1], [LANES * d for d in range(-2, 3)], A_WINDOW)
    bias_b = _bias_tiles(rel_bias, 2 * B_HEADS, [1], [LANES * d for d in range(-B_FAR_TILES, B_FAR_TILES + 1)], None)
    bias_c = _bias_tiles(rel_bias, len(C_BRANCHES) * C_KV_HEADS, [dil for _, dil in C_BRANCHES],
                         [C_HALF * d for d in range(-2, 3)], C_HALF)
    bias_b = bias_b.reshape(B_HEADS, 2, 2 * B_FAR_TILES + 1, LANES, LANES)

    w_qkv, w_o, w_up, w_down = (w.astype(BF16) for w in (_a_weight_layout(l0_w_qkv), l0_w_o, l0_w_up, l0_w_down))
    x2 = x.reshape(batch * seq, D_MODEL)
    o = _mixer_a(x2, l0_attn_norm, w_qkv, l0_q_gain, l0_k_gain, l0_sink, bias_a, batch, seq)
    x2, (w_qkv, w_o, w_up, w_down) = _proj_ffn(x2, o, w_o, l0_ffn_norm, w_up, l0_conv_w, l0_conv_b, w_down,
                                               [l1_w_qkv, l1_w_o, l1_w_up, l1_w_down], seq)
    o = _mixer_b(x2, l1_attn_norm, w_qkv, l1_q_gain, l1_k_gain, l1_lambda_q1, l1_lambda_k1, l1_lambda_q2,
                 l1_lambda_k2, l1_sub_gain, bias_b, _lambda_init(1), batch, seq)
    x2, (w_qkv, w_o, w_up, w_down) = _proj_ffn(x2, o, w_o, l1_ffn_norm, w_up, l1_conv_w, l1_conv_b, w_down,
                                               [l2_w_qkv, l2_w_o, l2_w_up, l2_w_down], seq)
    o = _mixer_c(x2, l2_attn_norm, w_qkv, l2_q_gain, l2_k_gain, bias_c, batch, seq)
    x2, (w_qkv, w_o, w_up, w_down) = _proj_ffn(x2, o, w_o, l2_ffn_norm, w_up, l2_conv_w, l2_conv_b, w_down,
                                               [_a_weight_layout(l3_w_qkv), l3_w_o, l3_w_up, l3_w_down], seq)
    o = _mixer_a(x2, l3_attn_norm, w_qkv, l3_q_gain, l3_k_gain, l3_sink, bias_a, batch, seq)
    x2, _ = _proj_ffn(x2, o, w_o, l3_ffn_norm, w_up, l3_conv_w, l3_conv_b, w_down, [], seq)
    return x2.reshape(batch, seq, D_MODEL)
```

```python
import functools
import math

import jax
import jax.numpy as jnp
from jax import lax
from jax.experimental import pallas as pl
from jax.experimental.pallas import tpu as pltpu

F32 = jnp.float32
BF16 = jnp.bfloat16

D_MODEL = 1024
EPS = 1e-6
NEG_INF = -1e30
LOG2E = math.log2(math.e)
LANES = 128
MXU_TILE = 256
VMEM_LIMIT = 56 * 1024 * 1024

NUM_BUCKETS = 32
MAX_DISTANCE = 1024

A_HEADS, A_KV_HEADS, A_HEAD_DIM, A_WINDOW = 16, 4, 64, 128
B_HEADS, B_HEAD_DIM = 8, 64
B_FAR_TILES = 6
C_BRANCHES = ((128, 1), (512, 4), (2048, 16))
C_KV_HEADS, C_HEAD_DIM = 4, 128
C_HALF = 64
D_FF = 2816
FFN_CHUNK = 256
FFN_HALO = 16


def _params(sem, vmem=VMEM_LIMIT):
    return pltpu.CompilerParams(dimension_semantics=sem, vmem_limit_bytes=vmem)


def _rms(x, gain):
    return x * lax.rsqrt(jnp.mean(x * x, axis=-1, keepdims=True) + EPS) * gain


def _bias_tile_kernel(tab_ref, dil_ref, off_ref, out_ref, *, half, heads_per_group):
    grp = pl.program_id(0)
    t = pl.program_id(1)
    dil = dil_ref[grp]
    off = off_ref[t]
    q = lax.broadcasted_iota(jnp.int32, (LANES, LANES), 0)
    c = lax.broadcasted_iota(jnp.int32, (LANES, LANES), 1)
    x = c - q + off
    rel = x * dil
    nb = NUM_BUCKETS // 2
    max_exact = nb // 2
    n = jnp.abs(rel)
    nf = jnp.maximum(n, 1).astype(F32)
    large = max_exact + (jnp.log(nf * (1.0 / max_exact)) / math.log(MAX_DISTANCE / max_exact)
                         * (nb - max_exact)).astype(jnp.int32)
    large = jnp.minimum(large, nb - 1)
    bucket = jnp.where(rel > 0, nb, 0) + jnp.where(n < max_exact, n, large)
    inside = None if half is None else jnp.abs(x) <= half
    for hh in range(heads_per_group):
        row = tab_ref[pl.ds(grp * heads_per_group + hh, 1), :]
        bias = jnp.take_along_axis(jnp.broadcast_to(row, (LANES, LANES)), bucket, axis=1) * LOG2E
        if inside is not None:
            bias = jnp.where(inside, bias, NEG_INF)
        out_ref[hh] = bias


def _bias_tiles(rel_bias, n_heads, group_dils, offs, half):
    n_groups, nt = len(group_dils), len(offs)
    hpg = n_heads // n_groups
    smem = pl.BlockSpec(memory_space=pltpu.SMEM)
    n_buckets, n_cols = rel_bias.shape
    table_rows = jnp.pad(rel_bias.T, ((0, 0), (0, LANES - n_buckets)))
    return pl.pallas_call(
        functools.partial(_bias_tile_kernel, half=half, heads_per_group=hpg),
        out_shape=jax.ShapeDtypeStruct((n_heads, nt, LANES, LANES), F32),
        grid=(n_groups, nt),
        in_specs=[pl.BlockSpec((n_cols, LANES), lambda g, t: (0, 0)), smem, smem],
        out_specs=pl.BlockSpec((hpg, None, LANES, LANES), lambda g, t: (g, t, 0, 0)),
        compiler_params=_params(("arbitrary", "arbitrary")),
        name="bias_tiles",
    )(table_rows, jnp.asarray(group_dils, jnp.int32), jnp.asarray(offs, jnp.int32))


def _proj_kernel(x_ref, g_ref, w_ref, cs_ref, p_ref, *out_refs, widths, n_norm_cols, group, slabs):
    h = _rms(x_ref[...], g_ref[...]).astype(BF16)
    group_w = 2 * MXU_TILE
    tiles = [(o_ref, off) for o_ref, width in zip(out_refs, widths) for off in range(0, width, group_w)]

    def project(t):
        return jnp.dot(h, w_ref[:, t * group_w:(t + 1) * group_w], preferred_element_type=F32)

    def finish(t, y):
        o_ref, off = tiles[t]
        for s in range(group_w // MXU_TILE):
            col = t * group_w + s * MXU_TILE
            ys = y[:, s * MXU_TILE:(s + 1) * MXU_TILE]
            if col < n_norm_cols:
                ss = jnp.dot((ys * ys).astype(BF16), p_ref[...], preferred_element_type=F32)
                ys = ys * lax.rsqrt(ss * (1.0 / group) + EPS) * cs_ref[:, col:col + MXU_TILE]
            lo = off + s * MXU_TILE
            if slabs:
                for half in range(MXU_TILE // LANES):
                    o_ref[lo // LANES + half] = ys[:, half * LANES:(half + 1) * LANES]
            else:
                o_ref[:, lo:lo + MXU_TILE] = ys.astype(BF16)

    y = project(0)
    for t in range(len(tiles)):
        y_next = project(t + 1) if t + 1 < len(tiles) else None
        finish(t, y)
        y = y_next


def _project(x2, gain, w, colscale, widths, n_norm_cols, group, slabs=False, tm=1024):
    m = x2.shape[0]
    n = w.shape[1]
    assert sum(widths) == n and m % tm == 0 and n_norm_cols % MXU_TILE == 0
    assert all(wd % (2 * MXU_TILE) == 0 for wd in widths)
    r = jnp.arange(MXU_TILE) // group
    ones_bd = (r[:, None] == r[None, :]).astype(BF16)
    const = lambda i: (0, 0)
    if slabs:
        out_shape = [jax.ShapeDtypeStruct((wd // LANES, m, LANES), F32) for wd in widths]
        out_specs = [pl.BlockSpec((wd // LANES, tm, LANES), lambda i: (0, i, 0)) for wd in widths]
    else:
        out_shape = [jax.ShapeDtypeStruct((m, wd), BF16) for wd in widths]
        out_specs = [pl.BlockSpec((tm, wd), lambda i: (i, 0)) for wd in widths]
    return pl.pallas_call(
        functools.partial(_proj_kernel, widths=tuple(widths), n_norm_cols=n_norm_cols, group=group, slabs=slabs),
        out_shape=out_shape,
        grid=(m // tm,),
        in_specs=[pl.BlockSpec((tm, D_MODEL), lambda i: (i, 0)),
                  pl.BlockSpec((1, D_MODEL), const),
                  pl.BlockSpec((D_MODEL, n), const),
                  pl.BlockSpec((1, n), const),
                  pl.BlockSpec((MXU_TILE, MXU_TILE), const)],
        out_specs=out_specs,
        compiler_params=_params(("arbitrary",)),
        name="qkv_proj",
    )(x2, gain.reshape(1, D_MODEL), w, colscale.reshape(1, n), ones_bd)


def _attn_a_kernel(sink_ref, q_ref, k_ref, v_ref, bias_ref, o_ref, *, nq, n_tiles):
    i = pl.program_id(1)
    lane = lax.broadcasted_iota(jnp.int32, (LANES, LANES), 1)
    lo = lane < A_HEAD_DIM
    group = A_HEADS // A_KV_HEADS

    def window(s):
        t = i * nq + s
        base = jnp.clip(t - 1, 0, n_tiles - 3)
        return t, base, pl.ds(pl.multiple_of(base * LANES, LANES), 3 * LANES)

    def scores(s, hk):
        rows = slice(s * LANES, (s + 1) * LANES)
        parts = []
        for c in (2 * hk, 2 * hk + 1):
            qc = q_ref[rows, c * LANES:(c + 1) * LANES]
            parts += [jnp.where(lo, qc, jnp.zeros_like(qc)), jnp.where(lo, jnp.zeros_like(qc), qc)]
        qs = jnp.concatenate(parts, axis=0)
        kw = k_ref[window(s)[2], hk * LANES:(hk + 1) * LANES]
        return lax.dot_general(qs, kw, (((1,), (1,)), ((), ())), preferred_element_type=F32)

    def finish(s, hk, sc):
        t, base, krows = window(s)
        rows = slice(s * LANES, (s + 1) * LANES)
        blocks = []
        for g in range(group):
            hq = hk * group + g
            blocks.append(jnp.concatenate(
                [sc[g * LANES:(g + 1) * LANES, w * LANES:(w + 1) * LANES] + bias_ref[hq, base + w - t + 2]
                 for w in range(3)], axis=1))
        sc = jnp.concatenate(blocks, axis=0)
        mx = jnp.max(sc, axis=-1, keepdims=True)
        p = jnp.exp2(sc - mx)
        den = jnp.sum(p, axis=-1, keepdims=True)
        pv = jnp.dot(p.astype(BF16), v_ref[krows, hk * LANES:(hk + 1) * LANES],
                     preferred_element_type=F32)
        outs = []
        for g in range(group):
            hq = hk * group + g
            gr = slice(g * LANES, (g + 1) * LANES)
            outs.append(pv[gr] / (den[gr] + jnp.exp2(sink_ref[hq] * LOG2E - mx[gr])))
        o_ref[rows, (2 * hk) * LANES:(2 * hk + 1) * LANES] = jnp.where(lo, outs[0], outs[1]).astype(BF16)
        o_ref[rows, (2 * hk + 1) * LANES:(2 * hk + 2) * LANES] = jnp.where(lo, outs[2], outs[3]).astype(BF16)

    blocks_todo = [(s, hk) for s in range(nq) for hk in range(A_KV_HEADS)]
    sc = scores(*blocks_todo[0])
    for n, blk in enumerate(blocks_todo):
        sc_next = scores(*blocks_todo[n + 1]) if n + 1 < len(blocks_todo) else None
        finish(*blk, sc)
        sc = sc_next


def _attention_a(qkv, bias, sink, batch, seq, tq=512):
    nq = tq // LANES
    n_tiles = seq // LANES
    qkv3 = qkv.reshape(batch, seq, 2 * D_MODEL)
    return pl.pallas_call(
        functools.partial(_attn_a_kernel, nq=nq, n_tiles=n_tiles),
        out_shape=jax.ShapeDtypeStruct((batch, seq, D_MODEL), BF16),
        grid=(batch, seq // tq),
        in_specs=[pl.BlockSpec(memory_space=pltpu.SMEM),
                  pl.BlockSpec((None, tq, D_MODEL), lambda b, i: (b, i, 0)),
                  pl.BlockSpec((None, seq, 512), lambda b, i: (b, 0, 2)),
                  pl.BlockSpec((None, seq, 512), lambda b, i: (b, 0, 3)),
                  pl.BlockSpec((A_HEADS, 5, LANES, LANES), lambda b, i: (0, 0, 0, 0))],
        out_specs=pl.BlockSpec((None, tq, D_MODEL), lambda b, i: (b, i, 0)),
        compiler_params=_params(("arbitrary", "arbitrary")),
        name="attn_window",
    )(sink, qkv3, qkv3, qkv3, bias)


def _attn_b_kernel(lam_ref, sg_ref, q_ref, k_ref, v_ref, bias_ref, o_ref, qs_ref, m_ref, acc_ref,
                   *, tq, tk, n_kc, lambda_init):
    i = pl.program_id(2)
    nqt = tq // LANES
    nkt = tk // LANES
    lo = lax.broadcasted_iota(jnp.int32, (tq, LANES), 1) < B_HEAD_DIM
    q = q_ref[...]
    zero = jnp.zeros_like(q)
    qs_ref[0:tq, :] = jnp.where(lo, q, zero)
    qs_ref[tq:2 * tq, :] = jnp.where(lo, zero, q)
    m_ref[...] = jnp.full(m_ref.shape, -jnp.inf, F32)
    acc_ref[...] = jnp.zeros(acc_ref.shape, F32)

    def map_chunk(mp, kc):
        mrows = slice(mp * tq, (mp + 1) * tq)
        keys = slice(kc * tk, (kc + 1) * tk)
        sc = lax.dot_general(qs_ref[mrows, :], k_ref[keys, :], (((1,), (1,)), ((), ())),
                             preferred_element_type=F32)
        probs = []
        for a in range(nqt):
            rows = slice(mp * tq + a * LANES, mp * tq + (a + 1) * LANES)
            tiles = []
            for w in range(nkt):
                d = jnp.clip(kc * nkt + w - (i * nqt + a), -B_FAR_TILES, B_FAR_TILES) + B_FAR_TILES
                tiles.append(sc[a * LANES:(a + 1) * LANES, w * LANES:(w + 1) * LANES] + bias_ref[mp, d])
            m_old = m_ref[rows, :]
            m_new = jnp.maximum(m_old, jnp.max(functools.reduce(jnp.maximum, tiles), axis=-1, keepdims=True))
            alpha = jnp.exp2(m_old - m_new)
            p = [jnp.exp2((t - m_new).astype(BF16)) for t in tiles]
            acc_ref[rows, :] = jnp.concatenate([alpha, alpha], axis=1) * acc_ref[rows, :]
            m_ref[rows, :] = m_new
            probs.append(jnp.concatenate(p, axis=1))
        vw = v_ref[keys, :]
        acc_ref[mrows, :] += jnp.dot(jnp.concatenate(probs, axis=0), jnp.concatenate([vw, jnp.ones_like(vw)], axis=1),
                                     preferred_element_type=F32)

    for kc in range(n_kc):
        for mp in range(2):
            map_chunk(mp, kc)
    lv = lam_ref[...]
    lam = (jnp.exp(jnp.sum(lv[0:1] * lv[1:2], axis=-1, keepdims=True))
           - jnp.exp(jnp.sum(lv[2:3] * lv[3:4], axis=-1, keepdims=True)) + lambda_init)
    o = acc_ref[:, 0:LANES] / acc_ref[:, LANES:2 * LANES]
    o = o[:tq] - lam * o[tq:]
    o_ref[...] = (_rms(o, sg_ref[...]) * (1.0 - lambda_init)).astype(BF16)


def _attention_b(qkv, bias, lam_vecs, sub_gain, lambda_init, batch, seq, tq=2048, tk=256):
    qkv3 = qkv.reshape(batch, seq, 3 * D_MODEL)
    nb = 2 * B_FAR_TILES + 1
    return pl.pallas_call(
        functools.partial(_attn_b_kernel, tq=tq, tk=tk, n_kc=seq // tk, lambda_init=lambda_init),
        out_shape=jax.ShapeDtypeStruct((batch, seq, D_MODEL), BF16),
        grid=(batch, B_HEADS, seq // tq),
        in_specs=[pl.BlockSpec((4, B_HEAD_DIM), lambda b, h, i: (0, 0)),
                  pl.BlockSpec((1, LANES), lambda b, h, i: (0, 0)),
                  pl.BlockSpec((None, tq, LANES), lambda b, h, i: (b, i, h)),
                  pl.BlockSpec((None, seq, LANES), lambda b, h, i: (b, 0, B_HEADS + h)),
                  pl.BlockSpec((None, seq, LANES), lambda b, h, i: (b, 0, 2 * B_HEADS + h)),
                  pl.BlockSpec((None, 2, nb, LANES, LANES), lambda b, h, i: (h, 0, 0, 0, 0))],
        out_specs=pl.BlockSpec((None, tq, LANES), lambda b, h, i: (b, i, h)),
        scratch_shapes=[pltpu.VMEM((2 * tq, LANES), BF16), pltpu.VMEM((2 * tq, LANES), F32),
                        pltpu.VMEM((2 * tq, 2 * LANES), F32)],
        compiler_params=_params(("arbitrary", "arbitrary", "arbitrary")),
        name="attn_diff",
    )(lam_vecs, sub_gain.reshape(1, LANES), qkv3, qkv3, qkv3, bias)


def _attn_c_kernel(q_ref, k_ref, v_ref, bias_ref, o_ref, og_ref, lg_ref, *, seq, npos, merge_rows):
    c = pl.program_id(2)

    def rows_of(start, n, dil):
        return pl.ds(start, n) if dil == 1 else pl.ds(start, n, stride=dil)

    def window(g, r, tt):
        dil = C_BRANCHES[g][1]
        t0 = c * (npos // dil) + tt * LANES
        ws = jnp.clip(t0 - C_HALF, 0, seq // dil - 2 * LANES)
        return t0, ws, rows_of(r + dil * LANES * tt, LANES, dil), rows_of(r + dil * ws, 2 * LANES, dil)

    def scores(g, r, tt):
        _, _, qrows, krows = window(g, r, tt)
        return lax.dot_general(q_ref[g, qrows, :].astype(BF16), k_ref[krows, :].astype(BF16),
                               (((1,), (1,)), ((), ())), preferred_element_type=F32)

    def finish(g, r, tt, sc):
        t0, ws, qrows, krows = window(g, r, tt)
        sc = jnp.concatenate(
            [sc[:, w * LANES:(w + 1) * LANES] + bias_ref[g, (ws + w * LANES - t0) // C_HALF + 2]
             for w in range(2)], axis=1)
        mx = jnp.max(sc, axis=-1, keepdims=True)
        p = jnp.exp2((sc - mx).astype(BF16))
        vw = v_ref[krows, :].astype(BF16)
        pv = jnp.dot(p, jnp.concatenate([vw, jnp.ones_like(vw)], axis=1), preferred_element_type=F32)
        den = pv[:, LANES:2 * LANES]
        og_ref[g, qrows, :] = pv[:, 0:LANES] / den
        lg_ref[g, qrows, :] = mx + jnp.log2(den)

    blocks_todo = [(g, r, tt) for g, (_, dil) in enumerate(C_BRANCHES) for r in range(dil)
                   for tt in range(npos // (dil * LANES))]
    sc = scores(*blocks_todo[0])
    for n, blk in enumerate(blocks_todo):
        sc_next = scores(*blocks_todo[n + 1]) if n + 1 < len(blocks_todo) else None
        finish(*blk, sc)
        sc = sc_next
    for blk in range(npos // merge_rows):
        rows = slice(blk * merge_rows, (blk + 1) * merge_rows)
        lses = [lg_ref[g, rows, :] for g in range(len(C_BRANCHES))]
        mx = functools.reduce(jnp.maximum, lses)
        es = [jnp.exp2(l - mx) for l in lses]
        num = functools.reduce(jnp.add, [e * og_ref[g, rows, :] for g, e in enumerate(es)])
        o_ref[rows, :] = (num / functools.reduce(jnp.add, es)).astype(BF16)


def _attention_c(q, k, v, bias, batch, seq, npos=4096, merge_rows=256):
    n_groups = len(C_BRANCHES)
    m = batch * seq
    chunks = seq // npos
    q4 = q.reshape(n_groups, C_KV_HEADS, m, LANES)
    bias5 = bias.reshape(n_groups, C_KV_HEADS, 5, LANES, LANES)
    return pl.pallas_call(
        functools.partial(_attn_c_kernel, seq=seq, npos=npos, merge_rows=merge_rows),
        out_shape=jax.ShapeDtypeStruct((m, C_KV_HEADS * C_HEAD_DIM), BF16),
        grid=(batch, C_KV_HEADS, chunks),
        in_specs=[pl.BlockSpec((n_groups, None, npos, LANES), lambda b, j, c: (0, j, b * chunks + c, 0)),
                  pl.BlockSpec((None, seq, LANES), lambda b, j, c: (j, b, 0)),
                  pl.BlockSpec((None, seq, LANES), lambda b, j, c: (j, b, 0)),
                  pl.BlockSpec((n_groups, None, 5, LANES, LANES), lambda b, j, c: (0, j, 0, 0, 0))],
        out_specs=pl.BlockSpec((npos, LANES), lambda b, j, c: (b * chunks + c, j)),
        scratch_shapes=[pltpu.VMEM((n_groups, npos, LANES), F32), pltpu.VMEM((n_groups, npos, LANES), F32)],
        compiler_params=_params(("arbitrary", "arbitrary", "arbitrary")),
        name="attn_dilated",
    )(q4, k, v, bias5)


def _ffn_kernel(x_ref, xn_ref, xp_ref, o_ref, on_ref, op_ref, wo_ref, g_ref, wup_ref, cw_ref, cb_ref, wdn_ref,
                *rest, tm, tiles_per_seq, n_chunks, cast_blocks):
    n_cast = len(cast_blocks)
    cast_in, y_ref, cast_out = rest[:n_cast], rest[n_cast], rest[n_cast + 1:2 * n_cast + 1]
    oe_ref, h_ref, acc_ref, ua_ref, ub_ref = rest[2 * n_cast + 1:]
    i = pl.program_id(0)
    for src, dst, blocks in zip(cast_in, cast_out, cast_blocks):
        @pl.when(i < blocks)
        def _():
            dst[...] = src[...].astype(BF16)
    pos = i % tiles_per_seq
    kdim = o_ref.shape[1]
    halo_row = lax.broadcasted_iota(jnp.int32, (FFN_HALO, kdim), 0)
    oe_ref[0:tm, :] = o_ref[...]
    oe_ref[tm:tm + FFN_HALO, :] = jnp.where(halo_row < 8, on_ref[...], op_ref[...])
    attn = jnp.dot(oe_ref[...], wo_ref[...], preferred_element_type=F32)
    x1 = x_ref[...] + attn[0:tm]
    x1n = xn_ref[...] + attn[tm:tm + 8]
    x1p = xp_ref[...] + attn[tm + 8:tm + FFN_HALO]
    g = g_ref[...]
    h_ref[0:tm, :] = _rms(x1, g).astype(BF16)
    hn = jnp.where(pos == tiles_per_seq - 1, 0.0, _rms(x1n, g))
    hp = jnp.where(pos == 0, 0.0, _rms(x1p, g))
    h_ref[tm:tm + FFN_HALO, :] = jnp.concatenate([hn, hp], axis=0).astype(BF16)
    acc_ref[...] = x1
    slabs = FFN_CHUNK // LANES

    def chunk_cols(k):
        return slice(k * FFN_CHUNK, (k + 1) * FFN_CHUNK)

    def up_proj(j, u_ref):
        h = h_ref[...]
        for half, k in enumerate((j, j + n_chunks)):
            u = jnp.dot(h, wup_ref[:, chunk_cols(k)], preferred_element_type=F32)
            for s in range(slabs):
                cols = slice(s * LANES, (s + 1) * LANES)
                slab = half * slabs + s
                u_ref[slab, 8:8 + tm, :] = u[0:tm, cols]
                u_ref[slab, 0:8, :] = u[tm + 8:tm + 16, cols]
                u_ref[slab, tm + 8:tm + 16, :] = u[tm:tm + 8, cols]

    def conv(u_ref, slab, k, cols):
        w = cw_ref[:, chunk_cols(k)]
        return (w[0:1, cols] * u_ref[slab, pl.ds(7, tm), :] + w[1:2, cols] * u_ref[slab, pl.ds(8, tm), :]
                + w[2:3, cols] * u_ref[slab, pl.ds(9, tm), :] + cb_ref[:, chunk_cols(k)][:, cols])

    def gate(j, u_ref):
        acts = []
        for s in range(slabs):
            cols = slice(s * LANES, (s + 1) * LANES)
            g_ = conv(u_ref, s, j, cols)
            val = conv(u_ref, slabs + s, j + n_chunks, cols)
            acts.append((g_ / (1.0 + jnp.exp(-g_)) * val).astype(BF16))
        return acts

    def down(j, acts):
        acc_ref[...] += jnp.dot(jnp.concatenate(acts, axis=1), wdn_ref[chunk_cols(j), :],
                                preferred_element_type=F32)

    bufs = (ua_ref, ub_ref)
    up_proj(0, bufs[0])
    pending = None
    for j in range(n_chunks):
        if j + 1 < n_chunks:
            up_proj(j + 1, bufs[(j + 1) % 2])
        if pending is not None:
            down(j - 1, pending)
        pending = gate(j, bufs[j % 2])
    down(n_chunks - 1, pending)
    y_ref[...] = acc_ref[...]


def _cast_block_rows(rows, n_steps):
    return next(rb for rb in range(16, rows + 1, 16) if rows % rb == 0 and rows // rb <= n_steps)


def _proj_ffn(x2, o2, w_o, gain, w_up, conv_w, conv_b, w_down, next_weights, seq, tm=512):
    m = x2.shape[0]
    kdim = o2.shape[1]
    n_chunks = D_FF // FFN_CHUNK
    tiles_per_seq = seq // tm
    n_steps = m // tm
    x_halo = tm // 8
    o_halo = tm // FFN_HALO
    const = lambda i: (0, 0)
    once = pl.Buffered(1)
    nxt = lambda blocks, per_tile: (lambda i: (jnp.minimum((i + 1) * per_tile, blocks - 1), 0))
    prv = lambda per_tile: (lambda i: (jnp.maximum(i * per_tile - 1, 0), 0))
    cast_rows = [_cast_block_rows(w.shape[0], n_steps) for w in next_weights]
    cast_blocks = tuple(w.shape[0] // rb for w, rb in zip(next_weights, cast_rows))
    cast_specs = [pl.BlockSpec((rb, w.shape[1]), (lambda i, nb=nb: (jnp.minimum(i, nb - 1), 0)))
                  for w, rb, nb in zip(next_weights, cast_rows, cast_blocks)]
    y, *cast = pl.pallas_call(
        functools.partial(_ffn_kernel, tm=tm, tiles_per_seq=tiles_per_seq, n_chunks=n_chunks,
                          cast_blocks=cast_blocks),
        out_shape=[jax.ShapeDtypeStruct((m, D_MODEL), F32)]
                  + [jax.ShapeDtypeStruct(w.shape, BF16) for w in next_weights],
        grid=(n_steps,),
        in_specs=[pl.BlockSpec((tm, D_MODEL), lambda i: (i, 0)),
                  pl.BlockSpec((8, D_MODEL), nxt(m // 8, x_halo)),
                  pl.BlockSpec((8, D_MODEL), prv(x_halo)),
                  pl.BlockSpec((tm, kdim), lambda i: (i, 0)),
                  pl.BlockSpec((FFN_HALO, kdim), nxt(m // FFN_HALO, o_halo)),
                  pl.BlockSpec((FFN_HALO, kdim), prv(o_halo)),
                  pl.BlockSpec((kdim, D_MODEL), const, pipeline_mode=once),
                  pl.BlockSpec((1, D_MODEL), const),
                  pl.BlockSpec((D_MODEL, 2 * D_FF), const, pipeline_mode=once),
                  pl.BlockSpec((3, 2 * D_FF), const),
                  pl.BlockSpec((1, 2 * D_FF), const),
                  pl.BlockSpec((D_FF, D_MODEL), const, pipeline_mode=once)] + cast_specs,
        out_specs=[pl.BlockSpec((tm, D_MODEL), lambda i: (i, 0))] + cast_specs,
        scratch_shapes=[pltpu.VMEM((tm + FFN_HALO, kdim), BF16),
                        pltpu.VMEM((tm + FFN_HALO, D_MODEL), BF16), pltpu.VMEM((tm, D_MODEL), F32),
                        pltpu.VMEM((2 * FFN_CHUNK // LANES, tm + FFN_HALO, LANES), F32),
                        pltpu.VMEM((2 * FFN_CHUNK // LANES, tm + FFN_HALO, LANES), F32)],
        compiler_params=_params(("arbitrary",)),
        name="proj_conv_ffn",
    )(x2, x2, x2, o2, o2, o2, w_o, gain.reshape(1, D_MODEL), w_up, conv_w.reshape(3, 2 * D_FF),
      conv_b.reshape(1, 2 * D_FF), w_down, *next_weights)
    return y, cast


def _dup_heads(w, n_heads, dim):
    w = w.reshape(D_MODEL, n_heads, 1, dim)
    return jnp.broadcast_to(w, (D_MODEL, n_heads, 2, dim)).reshape(D_MODEL, 2 * n_heads * dim)


def _a_weight_layout(w_qkv):
    nq = A_HEADS * A_HEAD_DIM
    nk = A_KV_HEADS * A_HEAD_DIM
    return jnp.concatenate([w_qkv[:, :nq], _dup_heads(w_qkv[:, nq:nq + nk], A_KV_HEADS, A_HEAD_DIM),
                            _dup_heads(w_qkv[:, nq + nk:], A_KV_HEADS, A_HEAD_DIM)], axis=1)


def _mixer_a(x2, norm_gain, w, q_gain, k_gain, sink, bias, batch, seq):
    nq = A_HEADS * A_HEAD_DIM
    nk = A_KV_HEADS * A_HEAD_DIM
    colscale = jnp.concatenate([jnp.tile(q_gain, A_HEADS) * (LOG2E * A_HEAD_DIM ** -0.5), jnp.tile(k_gain, 2 * A_KV_HEADS),
                                jnp.ones((2 * nk,), F32)])
    (qkv,) = _project(x2, norm_gain, w, colscale, [2 * D_MODEL], nq + 2 * nk, A_HEAD_DIM)
    return _attention_a(qkv, bias, sink, batch, seq).reshape(batch * seq, D_MODEL)


def _mixer_b(x2, norm_gain, w, q_gain, k_gain, lam_q1, lam_k1, lam_q2, lam_k2, sub_gain, bias,
             lambda_init, batch, seq):
    colscale = jnp.concatenate([jnp.tile(q_gain, 2 * B_HEADS) * (LOG2E * B_HEAD_DIM ** -0.5), jnp.tile(k_gain, 2 * B_HEADS),
                                jnp.ones((D_MODEL,), F32)])
    (qkv,) = _project(x2, norm_gain, w, colscale, [3 * D_MODEL], 2 * D_MODEL, B_HEAD_DIM)
    lam_vecs = jnp.stack([lam_q1, lam_k1, lam_q2, lam_k2])
    return _attention_b(qkv, bias, lam_vecs, sub_gain, lambda_init, batch, seq).reshape(batch * seq, D_MODEL)


def _mixer_c(x2, norm_gain, w, q_gain, k_gain, bias, batch, seq):
    width = C_KV_HEADS * C_HEAD_DIM
    n_groups = len(C_BRANCHES)
    colscale = jnp.concatenate([jnp.tile(q_gain, n_groups * C_KV_HEADS) * (LOG2E * C_HEAD_DIM ** -0.5),
                                jnp.tile(k_gain, C_KV_HEADS), jnp.ones((width,), F32)])
    assert all(window // (2 * dil) == C_HALF for window, dil in C_BRANCHES)
    q, k, v = _project(x2, norm_gain, w, colscale, [n_groups * width, width, width],
                       (n_groups + 1) * width, C_HEAD_DIM, slabs=True)
    return _attention_c(q, k, v, bias, batch, seq)


def _lambda_init(layer):
    return 0.8 - 0.6 * math.exp(-0.3 * layer)


def kernel(x, rel_bias, l0_attn_norm, l0_w_qkv, l0_q_gain, l0_k_gain, l0_sink, l0_w_o, l0_ffn_norm, l0_w_up, l0_conv_w, l0_conv_b, l0_w_down, l1_attn_norm, l1_w_qkv, l1_q_gain, l1_k_gain, l1_lambda_q1, l1_lambda_k1, l1_lambda_q2, l1_lambda_k2, l1_sub_gain, l1_w_o, l1_ffn_norm, l1_w_up, l1_conv_w, l1_conv_b, l1_w_down, l2_attn_norm, l2_w_qkv, l2_q_gain, l2_k_gain, l2_w_o, l2_ffn_norm, l2_w_up, l2_conv_w, l2_conv_b, l2_w_down, l3_attn_norm, l3_w_qkv, l3_q_gain, l3_k_gain, l3_sink, l3_w_o, l3_ffn_norm, l3_w_up, l3_conv_w, l3_conv_b, l3_w_down):
    batch, seq, _ = x.shape
    assert seq % (2 * LANES * C_BRANCHES[-1][1]) == 0
    bias_a = _bias_tiles(rel_bias, A_HEADS, [1], [LANES * d for d in range(-2, 3)], A_WINDOW)
    bias_b = _bias_tiles(rel_bias, 2 * B_HEADS, [1], [LANES * d for d in range(-B_FAR_TILES, B_FAR_TILES + 1)], None)
    bias_c = _bias_tiles(rel_bias, len(C_BRANCHES) * C_KV_HEADS, [dil for _, dil in C_BRANCHES],
                         [C_HALF * d for d in range(-2, 3)], C_HALF)
    bias_b = bias_b.reshape(B_HEADS, 2, 2 * B_FAR_TILES + 1, LANES, LANES)

    w_qkv, w_o, w_up, w_down = (w.astype(BF16) for w in (_a_weight_layout(l0_w_qkv), l0_w_o, l0_w_up, l0_w_down))
    x2 = x.reshape(batch * seq, D_MODEL)
    o = _mixer_a(x2, l0_attn_norm, w_qkv, l0_q_gain, l0_k_gain, l0_sink, bias_a, batch, seq)
    x2, (w_qkv, w_o, w_up, w_down) = _proj_ffn(x2, o, w_o, l0_ffn_norm, w_up, l0_conv_w, l0_conv_b, w_down,
                                               [l1_w_qkv, l1_w_o, l1_w_up, l1_w_down], seq)
    o = _mixer_b(x2, l1_attn_norm, w_qkv, l1_q_gain, l1_k_gain, l1_lambda_q1, l1_lambda_k1, l1_lambda_q2,
                 l1_lambda_k2, l1_sub_gain, bias_b, _lambda_init(1), batch, seq)
    x2, (w_qkv, w_o, w_up, w_down) = _proj_ffn(x2, o, w_o, l1_ffn_norm, w_up, l1_conv_w, l1_conv_b, w_down,
                                               [l2_w_qkv, l2_w_o, l2_w_up, l2_w_down], seq)
    o = _mixer_c(x2, l2_attn_norm, w_qkv, l2_q_gain, l2_k_gain, bias_c, batch, seq)
    x2, (w_qkv, w_o, w_up, w_down) = _proj_ffn(x2, o, w_o, l2_ffn_norm, w_up, l2_conv_w, l2_conv_b, w_down,
                                               [_a_weight_layout(l3_w_qkv), l3_w_o, l3_w_up, l3_w_down], seq)
    o = _mixer_a(x2, l3_attn_norm, w_qkv, l3_q_gain, l3_k_gain, l3_sink, bias_a, batch, seq)
    x2, _ = _proj_ffn(x2, o, w_o, l3_ffn_norm, w_up, l3_conv_w, l3_conv_b, w_down, [], seq)
    return x2.reshape(batch, seq, D_MODEL)
```
